```python
import jax, jax.numpy as jnp
from jax import lax
import numpy as np


D_MODEL = 1024
BATCH = 4
SEQ = 8192
DEPTH = 2

PLE_DIM = 256
D_FF = 2816
EPS = 1e-6
MLA_HEADS = 8
MLA_NOPE = 64
MLA_ROPE = 32
MLA_V = 64
MLA_Q_RANK = 384
MLA_KV_RANK = 256
ROPE_BASE = 10000.0
Q_BLOCK = 128
GLA_HEADS = 4
GLA_DK = 64
GLA_DV = 128
GLA_GATE_RANK = 16
GLA_TAU = 16.0
GLA_CHUNK = 64
RG_WIDTH = 512
RG_BLOCKS = 8
RG_CONV = 4
RG_C = 8.0
S5_GROUP = 16
S5_GROUPS = 32
S5_STATE = 64
S5_WIDTH = S5_GROUP * S5_GROUPS

EVEN_MIX_WIDTH = MLA_HEADS * MLA_V + GLA_HEADS * GLA_DV
ODD_MIX_WIDTH = RG_WIDTH + S5_WIDTH
EVEN_IN_SPLITS = [MLA_Q_RANK, MLA_KV_RANK, MLA_ROPE,
                  GLA_HEADS * GLA_DK, GLA_HEADS * GLA_DK, GLA_HEADS * GLA_DV,
                  GLA_GATE_RANK, GLA_HEADS * GLA_DV]
EVEN_IN_WIDTH = sum(EVEN_IN_SPLITS)
ODD_IN_SPLITS = [RG_WIDTH, RG_WIDTH, S5_WIDTH]
ODD_IN_WIDTH = sum(ODD_IN_SPLITS)
N_EVEN = (DEPTH + 1) // 2
N_ODD = DEPTH // 2

kernel_name = 'hybrid_mla_gla_rglru_s5_macaron'


def rms_norm(x, g):
    x32 = x.astype(jnp.float32)
    y = x32 * lax.rsqrt(jnp.mean(x32 * x32, axis=-1, keepdims=True) + EPS)
    return y.astype(x.dtype) * g


def split_cols(y, sizes):
    offs = [sum(sizes[:n]) for n in range(1, len(sizes))]
    return jnp.split(y, offs, axis=-1)


def swiglu(x, w1, w3, w2):
    return (jax.nn.silu(x @ w1) * (x @ w3)) @ w2


def rope(x, positions):
    half = MLA_ROPE // 2
    inv = ROPE_BASE ** (-jnp.arange(half, dtype=jnp.float32) / half)
    ang = positions.astype(jnp.float32)[:, None] * inv[None, :]
    cos = jnp.cos(ang)[:, None, :]
    sin = jnp.sin(ang)[:, None, :]
    x1, x2 = x[..., :half], x[..., half:]
    return jnp.concatenate([x1 * cos - x2 * sin, x1 * sin + x2 * cos], axis=-1).astype(x.dtype)


def mla(c_q, c_kv, k_rope_in, q_norm, w_q_up, kv_norm, w_kv_up):
    B, S, _ = c_q.shape
    pos = jnp.arange(S)
    q = (rms_norm(c_q, q_norm) @ w_q_up).reshape(B, S, MLA_HEADS, MLA_NOPE + MLA_ROPE)
    scale = (MLA_NOPE + MLA_ROPE) ** -0.5
    q_nope = q[..., :MLA_NOPE] * scale
    q_rope = rope(q[..., MLA_NOPE:], pos) * scale
    kv = (rms_norm(c_kv, kv_norm) @ w_kv_up).reshape(B, S, MLA_HEADS, MLA_NOPE + MLA_V)
    k_nope, v = kv[..., :MLA_NOPE], kv[..., MLA_NOPE:]
    k_rope = rope(k_rope_in[:, :, None, :], pos)[:, :, 0, :]
    outs = []
    for blk in range(S // Q_BLOCK):
        s0 = blk * Q_BLOCK
        s1 = s0 + Q_BLOCK
        sc = (jnp.einsum('bqhd,bkhd->bhqk', q_nope[:, s0:s1], k_nope[:, :s1])
              + jnp.einsum('bqhr,bkr->bhqk', q_rope[:, s0:s1], k_rope[:, :s1]))
        mask = jnp.arange(s1)[None, :] <= jnp.arange(s0, s1)[:, None]
        sc = jnp.where(mask, sc.astype(jnp.float32), -jnp.inf)
        pr = jax.nn.softmax(sc, axis=-1).astype(v.dtype)
        outs.append(jnp.einsum('bhqk,bkhd->bqhd', pr, v[:, :s1]))
    o = jnp.concatenate(outs, axis=1)
    return o.reshape(B, S, MLA_HEADS * MLA_V)


def gla(q, k, v, g_low, r, w_gate_up, b_gate, out_norm):
    B, S, _ = q.shape
    H, dk, dv, C = GLA_HEADS, GLA_DK, GLA_DV, GLA_CHUNK
    N = S // C
    f32 = jnp.float32

    def heads(t, d):
        return t.astype(f32).reshape(B, N, C, H, d).transpose(0, 3, 1, 2, 4)

    qh = heads(q, dk) * (dk ** -0.5)
    kh = heads(k, dk)
    vh = heads(v, dv)
    g = jax.nn.log_sigmoid((g_low @ w_gate_up + b_gate).astype(f32)) / GLA_TAU
    b = jnp.cumsum(heads(g, dk), axis=3)
    b_last = b[:, :, :, -1:, :]
    q_e = qh * jnp.exp(b)
    k_e = kh * jnp.exp(-b)
    causal = jnp.tril(jnp.ones((C, C), dtype=bool))
    att = jnp.where(causal, jnp.einsum('bhncd,bhnjd->bhncj', q_e, k_e), 0.0)
    o_intra = jnp.einsum('bhncj,bhnje->bhnce', att, vh)
    k_end = kh * jnp.exp(b_last - b)
    chunk_kv = jnp.einsum('bhncd,bhnce->nbhde', k_end, vh)
    decay = jnp.exp(b_last[:, :, :, 0, :]).transpose(2, 0, 1, 3)

    def step(state, inp):
        dec, kv = inp
        return dec[..., None] * state + kv, state

    _, states = lax.scan(step, jnp.zeros((B, H, dk, dv), f32), (decay, chunk_kv))
    o_inter = jnp.einsum('bhncd,nbhde->bhnce', q_e, states)
    o = (o_intra + o_inter).transpose(0, 2, 3, 1, 4).reshape(B, S, H, dv)
    o = rms_norm(o, out_norm).reshape(B, S, H * dv) * jax.nn.silu(r.astype(f32))
    return o.astype(q.dtype)


def even_mixer(xn, w_in, q_norm, w_q_up, kv_norm, w_kv_up, w_gate_up, b_gate, out_norm, w_out):
    c_q, c_kv, k_r, gq, gk, gv, g_low, g_r = split_cols(xn @ w_in, EVEN_IN_SPLITS)
    y_a = mla(c_q, c_kv, k_r, q_norm, w_q_up, kv_norm, w_kv_up)
    y_b = gla(gq, gk, gv, g_low, g_r, w_gate_up, b_gate, out_norm)
    return jnp.concatenate([y_a, y_b], axis=-1) @ w_out


def rg_lru_branch(x_gate, x_in, conv_w, conv_b, w_a, b_a, w_i, b_i, lam):
    B, S, W = x_in.shape
    f32 = jnp.float32
    xc = lax.conv_general_dilated(x_in, conv_w[:, None, :], window_strides=(1,),
                                  padding=[(RG_CONV - 1, 0)],
                                  dimension_numbers=('NWC', 'WIO', 'NWC'),
                                  feature_group_count=W) + conv_b
    xb = xc.reshape(B, S, RG_BLOCKS, W // RG_BLOCKS)
    r = jax.nn.sigmoid(jnp.einsum('bshi,hij->bshj', xb, w_a).reshape(B, S, W) + b_a)
    i = jax.nn.sigmoid(jnp.einsum('bshi,hij->bshj', xb, w_i).reshape(B, S, W) + b_i)
    log_a = (-RG_C * jax.nn.softplus(-lam) * r).astype(f32)
    a = jnp.exp(log_a)
    bx = jnp.sqrt(-jnp.expm1(2.0 * log_a)) * (i * xc).astype(f32)

    def comb(e1, e2):
        a1, b1 = e1
        a2, b2 = e2
        return a1 * a2, a2 * b1 + b2

    _, h = lax.associative_scan(comb, (a, bx), axis=1)
    return h.astype(x_in.dtype) * jax.nn.gelu(x_gate)


def s5_branch(u, a_re, a_im, log_dt, b_re, b_im, c_re, c_im, d, w_glu, b_glu):
    B, S, _ = u.shape
    f32 = jnp.float32
    ug = u.astype(f32).reshape(B, S, S5_GROUPS, S5_GROUP)
    dt = jnp.exp(log_dt.astype(f32))[:, None]
    lr, li = a_re.astype(f32), a_im.astype(f32)
    mag = jnp.exp(lr * dt)
    ab_re = mag * jnp.cos(li * dt)
    ab_im = mag * jnp.sin(li * dt)
    den = lr * lr + li * li
    nr, ni = ab_re - 1.0, ab_im
    coef_re = (nr * lr + ni * li) / den
    coef_im = (ni * lr - nr * li) / den
    bb_re = coef_re[..., None] * b_re - coef_im[..., None] * b_im
    bb_im = coef_re[..., None] * b_im + coef_im[..., None] * b_re
    bu_re = jnp.einsum('bsgc,gpc->bsgp', ug, bb_re)
    bu_im = jnp.einsum('bsgc,gpc->bsgp', ug, bb_im)
    at_re = jnp.broadcast_to(ab_re, (1, S) + ab_re.shape)
    at_im = jnp.broadcast_to(ab_im, (1, S) + ab_im.shape)

    def comb(e1, e2):
        a1r, a1i, b1r, b1i = e1
        a2r, a2i, b2r, b2i = e2
        return (a2r * a1r - a2i * a1i, a2r * a1i + a2i * a1r,
                a2r * b1r - a2i * b1i + b2r, a2r * b1i + a2i * b1r + b2i)

    _, _, x_re, x_im = lax.associative_scan(comb, (at_re, at_im, bu_re, bu_im), axis=1)
    y = (jnp.einsum('bsgp,gcp->bsgc', x_re, c_re) - jnp.einsum('bsgp,gcp->bsgc', x_im, c_im)
         + d * ug)
    y = jax.nn.gelu(y.reshape(B, S, S5_WIDTH))
    y = y * jax.nn.sigmoid(y @ w_glu + b_glu)
    return y.astype(u.dtype)


def odd_mixer(xn, w_in, conv_w, conv_b, w_a, b_a, w_i, b_i, lam,
              a_re, a_im, log_dt, b_re, b_im, c_re, c_im, d, w_glu, b_glu, w_out):
    x_gate, x_rg, u = split_cols(xn @ w_in, ODD_IN_SPLITS)
    y_c = rg_lru_branch(x_gate, x_rg, conv_w, conv_b, w_a, b_a, w_i, b_i, lam)
    y_d = s5_branch(u, a_re, a_im, log_dt, b_re, b_im, c_re, c_im, d, w_glu, b_glu)
    return jnp.concatenate([y_c, y_d], axis=-1) @ w_out


def setup_inputs(seed: int = 0) -> dict:
    key = jax.random.key(seed)
    ks = iter(jax.random.split(key, 64))
    f32 = jnp.float32
    D, L, NE, NO = D_MODEL, DEPTH, N_EVEN, N_ODD
    G, P = S5_GROUPS, S5_STATE

    def nrm(shape, fan_in):
        return jax.random.normal(next(ks), shape, f32) * (fan_in ** -0.5)

    def gain(shape):
        return 1.0 + 0.01 * jax.random.normal(next(ks), shape, f32)

    def small(shape):
        return 0.01 * jax.random.normal(next(ks), shape, f32)

    x = jax.random.normal(next(ks), (BATCH, SEQ, D), f32)
    p = jax.random.normal(next(ks), (DEPTH, BATCH, SEQ, PLE_DIM), f32)
    u_lam = jax.random.uniform(next(ks), (NO, RG_WIDTH), f32, 0.9, 0.999)
    a_lam = u_lam ** (1.0 / RG_C)
    rg_lambda = jnp.log(a_lam) - jnp.log1p(-a_lam)
    s5_a_re = -0.5 + small((NO, G, P))
    s5_a_im = jnp.pi * jnp.arange(P, dtype=f32) + small((NO, G, P))
    s5_log_dt = jnp.log(jax.random.uniform(next(ks), (NO, G), f32, 0.001, 0.1))
    return {
        'x': x, 'p': p,
        'ffn_a_norm': gain((L, D)), 'ffn_a_w1': nrm((L, D, D_FF), D),
        'ffn_a_w3': nrm((L, D, D_FF), D), 'ffn_a_w2': nrm((L, D_FF, D), D_FF),
        'mix_norm': gain((L, D)),
        'ffn_b_norm': gain((L, D)), 'ffn_b_w1': nrm((L, D, D_FF), D),
        'ffn_b_w3': nrm((L, D, D_FF), D), 'ffn_b_w2': nrm((L, D_FF, D), D_FF),
        'ple_norm': gain((L, D)), 'ple_w_gate': nrm((L, D, D), D),
        'ple_w_up': nrm((L, PLE_DIM, D), PLE_DIM),
        'ev_w_in': nrm((NE, D, EVEN_IN_WIDTH), D),
        'mla_q_norm': gain((NE, MLA_Q_RANK)),
        'mla_w_q_up': nrm((NE, MLA_Q_RANK, MLA_HEADS * (MLA_NOPE + MLA_ROPE)), MLA_Q_RANK),
        'mla_kv_norm': gain((NE, MLA_KV_RANK)),
        'mla_w_kv_up': nrm((NE, MLA_KV_RANK, MLA_HEADS * (MLA_NOPE + MLA_V)), MLA_KV_RANK),
        'gla_w_gate_up': nrm((NE, GLA_GATE_RANK, GLA_HEADS * GLA_DK), GLA_GATE_RANK),
        'gla_b_gate': small((NE, GLA_HEADS * GLA_DK)),
        'gla_out_norm': gain((NE, GLA_DV)),
        'ev_w_out': nrm((NE, EVEN_MIX_WIDTH, D), EVEN_MIX_WIDTH),
        'od_w_in': nrm((NO, D, ODD_IN_WIDTH), D),
        'rg_conv_w': nrm((NO, RG_CONV, RG_WIDTH), RG_CONV),
        'rg_conv_b': small((NO, RG_WIDTH)),
        'rg_w_a': nrm((NO, RG_BLOCKS, RG_WIDTH // RG_BLOCKS, RG_WIDTH // RG_BLOCKS), RG_WIDTH // RG_BLOCKS),
        'rg_b_a': small((NO, RG_WIDTH)),
        'rg_w_i': nrm((NO, RG_BLOCKS, RG_WIDTH // RG_BLOCKS, RG_WIDTH // RG_BLOCKS), RG_WIDTH // RG_BLOCKS),
        'rg_b_i': small((NO, RG_WIDTH)),
        'rg_lambda': rg_lambda,
        's5_a_re': s5_a_re, 's5_a_im': s5_a_im, 's5_log_dt': s5_log_dt,
        's5_b_re': nrm((NO, G, P, S5_GROUP), 2 * S5_GROUP),
        's5_b_im': nrm((NO, G, P, S5_GROUP), 2 * S5_GROUP),
        's5_c_re': nrm((NO, G, S5_GROUP, P), P),
        's5_c_im': nrm((NO, G, S5_GROUP, P), P),
        's5_d': jax.random.normal(next(ks), (NO, G, S5_GROUP), f32),
        's5_w_glu': nrm((NO, S5_WIDTH, S5_WIDTH), S5_WIDTH),
        's5_b_glu': small((NO, S5_WIDTH)),
        'od_w_out': nrm((NO, ODD_MIX_WIDTH, D), ODD_MIX_WIDTH),
        'final_norm': gain((D,)),
    }


def reference(x, p, ffn_a_norm, ffn_a_w1, ffn_a_w3, ffn_a_w2, mix_norm,
              ffn_b_norm, ffn_b_w1, ffn_b_w3, ffn_b_w2, ple_norm, ple_w_gate, ple_w_up,
              ev_w_in, mla_q_norm, mla_w_q_up, mla_kv_norm, mla_w_kv_up,
              gla_w_gate_up, gla_b_gate, gla_out_norm, ev_w_out,
              od_w_in, rg_conv_w, rg_conv_b, rg_w_a, rg_b_a, rg_w_i, rg_b_i, rg_lambda,
              s5_a_re, s5_a_im, s5_log_dt, s5_b_re, s5_b_im, s5_c_re, s5_c_im, s5_d,
              s5_w_glu, s5_b_glu, od_w_out, final_norm):
    h = x
    for i in range(DEPTH):
        h = h + 0.5 * swiglu(rms_norm(h, ffn_a_norm[i]), ffn_a_w1[i], ffn_a_w3[i], ffn_a_w2[i])
        hn = rms_norm(h, mix_norm[i])
        j = i // 2
        if i % 2 == 0:
            h = h + even_mixer(hn, ev_w_in[j], mla_q_norm[j], mla_w_q_up[j], mla_kv_norm[j],
                               mla_w_kv_up[j], gla_w_gate_up[j], gla_b_gate[j],
                               gla_out_norm[j], ev_w_out[j])
        else:
            h = h + odd_mixer(hn, od_w_in[j], rg_conv_w[j], rg_conv_b[j], rg_w_a[j], rg_b_a[j],
                              rg_w_i[j], rg_b_i[j], rg_lambda[j], s5_a_re[j], s5_a_im[j],
                              s5_log_dt[j], s5_b_re[j], s5_b_im[j], s5_c_re[j], s5_c_im[j],
                              s5_d[j], s5_w_glu[j], s5_b_glu[j], od_w_out[j])
        h = h + 0.5 * swiglu(rms_norm(h, ffn_b_norm[i]), ffn_b_w1[i], ffn_b_w3[i], ffn_b_w2[i])
        h = h + (p[i] @ ple_w_up[i]) * jax.nn.sigmoid(rms_norm(h, ple_norm[i]) @ ple_w_gate[i])
    return rms_norm(h, final_norm)
```

```python
import functools
import math

import jax
import jax.numpy as jnp
from jax import lax
from jax.experimental import pallas as pl
from jax.experimental.pallas import tpu as pltpu

MXU_DTYPE = jnp.bfloat16
F32 = jnp.float32

EPS = 1e-6
PLE_DIM = 256
MLA_HEADS = 8
MLA_NOPE = 64
MLA_ROPE = 32
MLA_V = 64
MLA_Q_RANK = 384
MLA_KV_RANK = 256
ROPE_BASE = 10000.0
GLA_HEADS = 4
GLA_DK = 64
GLA_DV = 128
GLA_GATE_RANK = 16
GLA_TAU = 16.0
GLA_CHUNK = 64
RG_WIDTH = 512
RG_BLOCKS = 8
RG_CONV = 4
RG_C = 8.0
S5_GROUP = 16
S5_GROUPS = 32
S5_STATE = 64
S5_WIDTH = S5_GROUP * S5_GROUPS
S5_LANES = S5_GROUPS * S5_STATE

LANES = 128
SUBLANES = 8
HEAD_PAD = LANES
VMEM_LIMIT = 56 * 1024 * 1024

FFN_ROWS = 512
FFN_CHUNKS = 2
PROJ_ROWS = 512
ATTN_Q = 512
GLA_ROWS = 256
ODD_ROWS = 512
SCAN_LANES = 512


def _rms(x, g):
    return x * lax.rsqrt(jnp.mean(x * x, axis=-1, keepdims=True) + EPS) * g


def _dot(a, b):
    return jnp.dot(a, b, preferred_element_type=F32)


def _dot_nt(a, b):
    return lax.dot_general(a, b, (((1,), (1,)), ((), ())), preferred_element_type=F32)


def _dot_tn(a, b):
    return lax.dot_general(a, b, (((0,), (0,)), ((), ())), preferred_element_type=F32)


def _sigmoid(x):
    return 1.0 / (1.0 + jnp.exp(-x))


def _gelu_tanh(x):
    return 0.5 * x * (1.0 + jnp.tanh(math.sqrt(2.0 / math.pi) * (x + 0.044715 * (x * x * x))))


def _log_sigmoid(x):
    return jnp.minimum(x, 0.0) - jnp.log1p(jnp.exp(-jnp.abs(x)))


def _resident(shape):
    return pl.BlockSpec(shape, lambda *_: (0,) * len(shape), pipeline_mode=pl.Buffered(1))


def _params(*semantics):
    return pltpu.CompilerParams(dimension_semantics=semantics, vmem_limit_bytes=VMEM_LIMIT)


def _ffn_kernel(*refs, ple, final):
    h_ref, g_ref, w1_ref, w3_ref, w2_ref = refs[:5]
    rest = list(refs[5:])
    o_ref = rest.pop()
    h = h_ref[...]
    xn = _rms(h, g_ref[...]).astype(MXU_DTYPE)
    d_ff = w1_ref.shape[1]
    fc = d_ff // FFN_CHUNKS
    y = None
    for c in range(FFN_CHUNKS):
        a = _dot(xn, w1_ref[:, c * fc:(c + 1) * fc])
        b = _dot(xn, w3_ref[:, c * fc:(c + 1) * fc])
        t = (a * _sigmoid(a) * b).astype(MXU_DTYPE)
        yc = _dot(t, w2_ref[c * fc:(c + 1) * fc, :])
        y = yc if y is None else y + yc
    h = h + 0.5 * y
    if ple:
        p_ref, gp_ref, wg_ref, wu_ref = rest[:4]
        rest = rest[4:]
        gate = _sigmoid(_dot(_rms(h, gp_ref[...]).astype(MXU_DTYPE), wg_ref[...]))
        up = _dot(p_ref[...].astype(MXU_DTYPE), wu_ref[...])
        h = h + up * gate
    if final:
        h = _rms(h, rest[0][...])
    o_ref[...] = h


def _ffn(h, norm, w1, w3, w2, *, batch, in_tb, out_tb, ple=None, final_norm=None):
    d = norm.shape[-1]
    tokens = h.size // d
    seq = tokens // batch
    tm = FFN_ROWS
    n_s = seq // tm

    def hspec(tb):
        if tb:
            return pl.BlockSpec((tm, d), lambda r: (r % n_s, r // n_s))
        return pl.BlockSpec((tm, d), lambda r: (r, 0))

    cast = lambda w: w.astype(MXU_DTYPE)
    args = [h, norm.reshape(1, d), cast(w1), cast(w3), cast(w2)]
    specs = [hspec(in_tb), _resident((1, d)), _resident(w1.shape), _resident(w3.shape),
             _resident(w2.shape)]
    if ple is not None:
        p, gp, wg, wu = ple
        args += [p.reshape(tokens, -1), gp.reshape(1, d), cast(wg), cast(wu)]
        specs += [pl.BlockSpec((tm, p.shape[-1]), lambda r: (r, 0)), _resident((1, d)),
                  _resident(wg.shape), _resident(wu.shape)]
    if final_norm is not None:
        args.append(final_norm.reshape(1, d))
        specs.append(_resident((1, d)))
    out_shape = (seq, batch * d) if out_tb else (tokens, d)
    return pl.pallas_call(
        functools.partial(_ffn_kernel, ple=ple is not None, final=final_norm is not None),
        grid=(tokens // tm,),
        in_specs=specs,
        out_specs=hspec(out_tb),
        out_shape=jax.ShapeDtypeStruct(out_shape, F32),
        compiler_params=_params("parallel"),
        name="ffn",
    )(*args)


_EV_COLS = (MLA_Q_RANK, MLA_KV_RANK, HEAD_PAD, HEAD_PAD, GLA_HEADS * HEAD_PAD,
            GLA_HEADS * HEAD_PAD, GLA_HEADS * GLA_DV, HEAD_PAD, GLA_HEADS * GLA_DV)


def _even_proj_kernel(h_ref, gm_ref, win_ref, gq_ref, wq_ref, wqr_ref, gkv_ref, wk_ref, wv_ref,
                      wg_ref, bg_ref, cq_ref, sq_ref, ck_ref, sk_ref,
                      q_ref, k_ref, v_ref, lq_ref, lk_ref, lv_ref, lg_ref, lr_ref):
    xn = _rms(h_ref[...], gm_ref[...]).astype(MXU_DTYPE)
    y = _dot(xn, win_ref[...])
    offs = [0]
    for w in _EV_COLS:
        offs.append(offs[-1] + w)
    c_q, c_kv, kr, kr_rot, lq, lk, lv, g_low, lr = (
        y[:, offs[i]:offs[i + 1]] for i in range(len(_EV_COLS)))

    cqn = _rms(c_q, gq_ref[...]).astype(MXU_DTYPE)
    qf = _dot(cqn, wq_ref[...])
    qr = _dot(cqn, wqr_ref[...])
    ckn = _rms(c_kv, gkv_ref[...]).astype(MXU_DTYPE)
    kn = _dot(ckn, wk_ref[...])
    v_ref[...] = _dot(ckn, wv_ref[...]).astype(v_ref.dtype)
    k_rope = kr * ck_ref[...] + kr_rot * sk_ref[...]
    cq, sq = cq_ref[...], sq_ref[...]
    for hd in range(MLA_HEADS):
        sl = slice(hd * HEAD_PAD, (hd + 1) * HEAD_PAD)
        q_ref[:, sl] = (qf[:, sl] * cq + qr[:, sl] * sq).astype(q_ref.dtype)
        k_ref[:, sl] = (kn[:, sl] + k_rope).astype(k_ref.dtype)

    gate = _dot(g_low.astype(MXU_DTYPE), wg_ref[...]) + bg_ref[...]
    lg_ref[...] = _log_sigmoid(gate) * (1.0 / GLA_TAU)
    lq_ref[...] = lq
    lk_ref[...] = lk
    lv_ref[...] = lv.astype(lv_ref.dtype)
    lr_ref[...] = lr


def _pad_heads(w, heads, width):
    lead = w.shape[:-1]
    w = w.reshape(lead + (heads, width))
    w = jnp.pad(w, [(0, 0)] * len(lead) + [(0, 0), (0, HEAD_PAD - width)])
    return w.reshape(lead + (heads * HEAD_PAD,))


def _rot_half(w):
    half = MLA_ROPE // 2
    return jnp.concatenate([-w[..., half:], w[..., :half]], axis=-1)


def _rope_slot(w):
    pad = [(0, 0)] * (w.ndim - 1) + [(MLA_NOPE, HEAD_PAD - MLA_NOPE - MLA_ROPE)]
    return jnp.pad(w, pad)


def _even_proj(h, mix_norm, w_in, q_norm, w_q_up, kv_norm, w_kv_up, w_gate_up, b_gate, *,
               batch):
    tokens, d = h.shape
    seq = tokens // batch
    tm = PROJ_ROWS
    n_s = seq // tm
    splits = [MLA_Q_RANK, MLA_KV_RANK, MLA_ROPE, GLA_HEADS * GLA_DK, GLA_HEADS * GLA_DK,
              GLA_HEADS * GLA_DV, GLA_GATE_RANK, GLA_HEADS * GLA_DV]
    offs = [0]
    for s in splits:
        offs.append(offs[-1] + s)
    wc_q, wc_kv, w_kr, w_lq, w_lk, w_lv, w_gl, w_lr = (
        w_in[:, offs[i]:offs[i + 1]] for i in range(len(splits)))
    win = jnp.concatenate([
        wc_q, wc_kv, _rope_slot(w_kr), _rope_slot(_rot_half(w_kr)),
        _pad_heads(w_lq, GLA_HEADS, GLA_DK), _pad_heads(w_lk, GLA_HEADS, GLA_DK), w_lv,
        jnp.pad(w_gl, [(0, 0), (0, HEAD_PAD - GLA_GATE_RANK)]), w_lr], axis=1)

    wq = w_q_up.reshape(MLA_Q_RANK, MLA_HEADS, MLA_NOPE + MLA_ROPE)
    wq_nope, wq_rope = wq[..., :MLA_NOPE], wq[..., MLA_NOPE:]
    zeros_nope = jnp.zeros_like(wq_nope)
    wq_full = _pad_heads(jnp.concatenate([wq_nope, wq_rope], -1).reshape(MLA_Q_RANK, -1),
                         MLA_HEADS, MLA_NOPE + MLA_ROPE)
    wq_rot = _pad_heads(jnp.concatenate([zeros_nope, _rot_half(wq_rope)], -1)
                        .reshape(MLA_Q_RANK, -1), MLA_HEADS, MLA_NOPE + MLA_ROPE)
    wkv = w_kv_up.reshape(MLA_KV_RANK, MLA_HEADS, MLA_NOPE + MLA_V)
    wk = _pad_heads(wkv[..., :MLA_NOPE].reshape(MLA_KV_RANK, -1), MLA_HEADS, MLA_NOPE)
    wv = wkv[..., MLA_NOPE:].reshape(MLA_KV_RANK, MLA_HEADS * MLA_V)
    wg = jnp.pad(_pad_heads(w_gate_up, GLA_HEADS, GLA_DK),
                 [(0, HEAD_PAD - GLA_GATE_RANK), (0, 0)])
    bg = _pad_heads(b_gate.reshape(1, -1), GLA_HEADS, GLA_DK)

    half = MLA_ROPE // 2
    inv = ROPE_BASE ** (-jnp.arange(half, dtype=F32) / half)
    ang = jnp.arange(seq).astype(F32)[:, None] * inv[None, :]
    cos2 = jnp.tile(jnp.cos(ang), (1, 2))
    sin2 = jnp.tile(jnp.sin(ang), (1, 2))
    scale = (MLA_NOPE + MLA_ROPE) ** -0.5
    ones = jnp.ones((seq, MLA_NOPE), F32)
    tail = jnp.zeros((seq, HEAD_PAD - MLA_NOPE - MLA_ROPE), F32)
    cq = jnp.concatenate([ones, cos2, tail], axis=1) * scale
    sq = jnp.concatenate([0 * ones, sin2, tail], axis=1) * scale
    ck = jnp.concatenate([0 * ones, cos2, tail], axis=1)
    sk = jnp.concatenate([0 * ones, sin2, tail], axis=1)

    cast = lambda w: w.astype(MXU_DTYPE)
    row = lambda w: pl.BlockSpec((tm, w), lambda r: (r, 0))
    tab = pl.BlockSpec((tm, HEAD_PAD), lambda r: (r % n_s, 0))
    gl_w = GLA_HEADS * HEAD_PAD
    gv_w = GLA_HEADS * GLA_DV
    qk_w = MLA_HEADS * HEAD_PAD
    outs = [(qk_w, MXU_DTYPE), (qk_w, MXU_DTYPE), (MLA_HEADS * MLA_V, MXU_DTYPE),
            (gl_w, F32), (gl_w, F32), (gv_w, MXU_DTYPE), (gl_w, F32), (gv_w, F32)]
    weights = [mix_norm.reshape(1, d), cast(win), q_norm.reshape(1, -1), cast(wq_full),
               cast(wq_rot), kv_norm.reshape(1, -1), cast(wk), cast(wv), cast(wg), bg]
    return pl.pallas_call(
        _even_proj_kernel,
        grid=(tokens // tm,),
        in_specs=[row(d)] + [_resident(w.shape) for w in weights] + [tab] * 4,
        out_specs=[row(w) for w, _ in outs],
        out_shape=[jax.ShapeDtypeStruct((tokens, w), dt) for w, dt in outs],
        compiler_params=_params("parallel"),
        name="even_proj",
    )(h, *weights, cq, sq, ck, sk)


def _attn_kernel(q_ref, k_ref, v_ref, o_ref, *, tq):
    qi = pl.program_id(2)
    row = lax.broadcasted_iota(jnp.int32, (tq, tq), 0)
    col = lax.broadcasted_iota(jnp.int32, (tq, tq), 1)
    causal = col <= row
    outs = []
    for hd in range(2):
        sl = slice(hd * HEAD_PAD, (hd + 1) * HEAD_PAD)
        q = q_ref[:, sl]

        def block(j, carry, masked):
            m, l, acc = carry
            start = pl.multiple_of(j * tq, tq)
            s = _dot_nt(q, k_ref[pl.ds(start, tq), sl])
            if masked:
                s = jnp.where(causal, s, -jnp.inf)
            m_new = jnp.maximum(m, jnp.max(s, axis=1, keepdims=True))
            alpha = jnp.exp(m - m_new)
            p = jnp.exp(s - m_new)
            l = alpha * l + jnp.sum(p, axis=1, keepdims=True)
            acc = alpha * acc + _dot(p.astype(MXU_DTYPE), v_ref[pl.ds(start, tq), :])
            return m_new, l, acc

        init = (jnp.full((tq, 1), -jnp.inf, F32), jnp.zeros((tq, 1), F32),
                jnp.zeros((tq, 2 * MLA_V), F32))
        carry = lax.fori_loop(0, qi, lambda j, c: block(j, c, False), init)
        _, l, acc = block(qi, carry, True)
        outs.append(acc / l)
    lane = lax.broadcasted_iota(jnp.int32, (tq, 2 * MLA_V), 1)
    o_ref[...] = jnp.where(lane < MLA_V, outs[0], outs[1]).astype(o_ref.dtype)


def _attention(q, k, v, *, batch):
    tokens = q.shape[0]
    seq = tokens // batch
    tq = ATTN_Q
    q3 = q.reshape(batch, seq, -1)
    k3 = k.reshape(batch, seq, -1)
    v3 = v.reshape(batch, seq, -1)
    pair = 2 * HEAD_PAD
    out = pl.pallas_call(
        functools.partial(_attn_kernel, tq=tq),
        grid=(batch, MLA_HEADS // 2, seq // tq),
        in_specs=[pl.BlockSpec((None, tq, pair), lambda b, hp, i: (b, i, hp)),
                  pl.BlockSpec((None, seq, pair), lambda b, hp, i: (b, 0, hp)),
                  pl.BlockSpec((None, seq, 2 * MLA_V), lambda b, hp, i: (b, 0, hp))],
        out_specs=pl.BlockSpec((None, tq, 2 * MLA_V), lambda b, hp, i: (b, i, hp)),
        out_shape=jax.ShapeDtypeStruct(v3.shape, MXU_DTYPE),
        compiler_params=_params("parallel", "parallel", "arbitrary"),
        name="attention",
    )(q3, k3, v3)
    return out.reshape(tokens, -1)


def _gla_out_kernel(lq_ref, lk_ref, lv_ref, lg_ref, lr_ref, oa_ref, h_ref, gn_ref, wo_ref,
                    o_ref, state_ref, y_ref):
    @pl.when(pl.program_id(1) == 0)
    def _():
        state_ref[...] = jnp.zeros_like(state_ref)

    tt, width = lg_ref.shape
    chunk = GLA_CHUNK
    pos = lax.broadcasted_iota(jnp.int32, (tt, width), 0) & (chunk - 1)
    b = lg_ref[...]
    d = 1
    while d < chunk:
        b = b + jnp.where(pos >= d, pltpu.roll(b, d, axis=0), 0.0)
        d *= 2
    lk = lk_ref[...]
    q_e = (lq_ref[...] * (GLA_DK ** -0.5) * jnp.exp(b)).astype(MXU_DTYPE)
    k_e = (lk * jnp.exp(-b)).astype(MXU_DTYPE)
    tri = (lax.broadcasted_iota(jnp.int32, (chunk, chunk), 1)
           <= lax.broadcasted_iota(jnp.int32, (chunk, chunk), 0))
    gn = gn_ref[...]
    for c in range(tt // chunk):
        rows = slice(c * chunk, (c + 1) * chunk)
        b_c = b[rows]
        b_last = b_c[chunk - 1:chunk]
        k_end = (lk[rows] * jnp.exp(b_last - b_c)).astype(MXU_DTYPE)
        decay = jnp.exp(b_last)
        for hd in range(GLA_HEADS):
            sl = slice(hd * HEAD_PAD, (hd + 1) * HEAD_PAD)
            vs = slice(hd * GLA_DV, (hd + 1) * GLA_DV)
            v_c = lv_ref[rows, vs]
            att = jnp.where(tri, _dot_nt(q_e[rows, sl], k_e[rows, sl]), 0.0)
            st = state_ref[hd]
            o = _dot(att.astype(MXU_DTYPE), v_c) + _dot_nt(q_e[rows, sl], st.astype(MXU_DTYPE))
            state_ref[hd] = st * decay[:, sl] + _dot_tn(v_c, k_end[:, sl])
            o = _rms(o, gn)
            r = lr_ref[rows, vs]
            y_ref[rows, vs] = (o * (r * _sigmoid(r))).astype(y_ref.dtype)
    n_a = oa_ref.shape[1]
    o_ref[...] = (h_ref[...] + _dot(oa_ref[...], wo_ref[:n_a, :])
                  + _dot(y_ref[...], wo_ref[n_a:, :]))


def _gla_out(lq, lk, lv, lg, lr, o_attn, h, out_norm, w_out, *, batch):
    tokens, d = h.shape
    seq = tokens // batch
    tt = GLA_ROWS
    n_s = seq // tt
    row = lambda w: pl.BlockSpec((tt, w), lambda b, i: (b * n_s + i, 0))
    ins = [lq, lk, lv, lg, lr, o_attn, h]
    return pl.pallas_call(
        _gla_out_kernel,
        grid=(batch, n_s),
        in_specs=[row(a.shape[1]) for a in ins] + [_resident((1, GLA_DV)),
                                                   _resident(w_out.shape)],
        out_specs=row(d),
        out_shape=jax.ShapeDtypeStruct((tokens, d), F32),
        scratch_shapes=[pltpu.VMEM((GLA_HEADS, GLA_DV, HEAD_PAD), F32),
                        pltpu.VMEM((tt, GLA_HEADS * GLA_DV), MXU_DTYPE)],
        compiler_params=_params("parallel", "arbitrary"),
        name="gla_out",
    )(*ins, out_norm.reshape(1, -1), w_out.astype(MXU_DTYPE))


def _prev_step(cur, prev_rolled, sub):
    cur_rolled = pltpu.roll(cur, SUBLANES // 2, axis=0)
    return jnp.where(sub < SUBLANES // 2, prev_rolled, cur_rolled), cur_rolled


def _odd_kernel(h_ref, gm_ref, win_ref, cw_ref, cb_ref, wgate_ref, bgate_ref, lam_ref,
                sa_ref, bw_ref, cwt_ref, sd_ref, wglu_ref, bglu_ref, wo_ref,
                o_ref,
                halo_ref, rg_a_ref, rg_b_ref, rg_x_ref, rg_carry_ref,
                bu_ref, xs_ref, s5_carry_ref):
    first = pl.program_id(0) == 0

    @pl.when(first)
    def _():
        halo_ref[...] = jnp.zeros_like(halo_ref)
        rg_carry_ref[...] = jnp.zeros_like(rg_carry_ref)
        s5_carry_ref[...] = jnp.zeros_like(s5_carry_ref)

    tm = h_ref.shape[0]
    groups = tm // SUBLANES
    h = h_ref[...]
    xn = _rms(h, gm_ref[...]).astype(MXU_DTYPE)
    y = _dot(xn, win_ref[...])
    x_gate, x_rg, u = (y[:, i * RG_WIDTH:(i + 1) * RG_WIDTH] for i in range(3))

    step = SUBLANES // 2
    halo_rows = halo_ref.shape[0]
    ext = jnp.concatenate([halo_ref[...], x_rg], axis=0)
    halo_ref[...] = x_rg[tm - halo_rows:, :]
    cw = cw_ref[...]
    xc = cb_ref[...] + x_rg * cw[RG_CONV - 1:RG_CONV]
    for j in range(RG_CONV - 1):
        back = (RG_CONV - 1 - j) * step
        xc = xc + ext[halo_rows - back:halo_rows - back + tm, :] * cw[j:j + 1]

    gates = _sigmoid(_dot(xc.astype(MXU_DTYPE), wgate_ref[...]) + bgate_ref[...])
    r_gate, i_gate = gates[:, :RG_WIDTH], gates[:, RG_WIDTH:]
    lam = lam_ref[...]
    softplus_neg = jnp.maximum(-lam, 0.0) + jnp.log1p(jnp.exp(-jnp.abs(lam)))
    log_a = (-RG_C * softplus_neg) * r_gate
    a = jnp.exp(log_a)
    rg_a_ref[...] = a
    rg_b_ref[...] = jnp.sqrt(-jnp.tanh(log_a) * (a * a + 1.0)) * (i_gate * xc)

    sub = lax.broadcasted_iota(jnp.int32, (SUBLANES, RG_WIDTH), 0)

    def rg_body(g, carry):
        x, ra_prev, rb_prev = carry
        rows = pl.ds(pl.multiple_of(g * SUBLANES, SUBLANES), SUBLANES)
        a = rg_a_ref[rows, :]
        bx = rg_b_ref[rows, :]
        a_prev, ra = _prev_step(a, ra_prev, sub)
        b_prev, rb = _prev_step(bx, rb_prev, sub)
        x = (a * a_prev) * x + (a * b_prev + bx)
        rg_x_ref[rows, :] = x
        return x, ra, rb

    carry = lax.fori_loop(0, groups, rg_body,
                          (rg_carry_ref[0], rg_carry_ref[1], rg_carry_ref[2]))
    for i in range(3):
        rg_carry_ref[i] = carry[i]
    y_c = rg_x_ref[...] * _gelu_tanh(x_gate)

    ub = u.astype(MXU_DTYPE)
    bu_ref[...] = _dot(ub, bw_ref[...])
    sub_s = lax.broadcasted_iota(jnp.int32, (SUBLANES, SCAN_LANES), 0)
    for lc in range(S5_LANES // SCAN_LANES):
        re = slice(lc * SCAN_LANES, (lc + 1) * SCAN_LANES)
        im = slice(S5_LANES + lc * SCAN_LANES, S5_LANES + (lc + 1) * SCAN_LANES)
        a_re, a_im = sa_ref[0, :, re], sa_ref[1, :, re]
        a2_re, a2_im = sa_ref[2, :, re], sa_ref[3, :, re]

        def s5_body(g, carry):
            x_re, x_im, pr_re, pr_im = carry
            rows = pl.ds(pl.multiple_of(g * SUBLANES, SUBLANES), SUBLANES)
            c_re = bu_ref[rows, re]
            c_im = bu_ref[rows, im]
            p_re, pr_re = _prev_step(c_re, pr_re, sub_s)
            p_im, pr_im = _prev_step(c_im, pr_im, sub_s)
            c_re = c_re + (a_re * p_re - a_im * p_im)
            c_im = c_im + (a_re * p_im + a_im * p_re)
            n_re = a2_re * x_re - a2_im * x_im + c_re
            n_im = a2_re * x_im + a2_im * x_re + c_im
            xs_ref[rows, re] = n_re
            xs_ref[rows, im] = n_im
            return n_re, n_im, pr_re, pr_im

        carry = lax.fori_loop(0, groups, s5_body,
                              (s5_carry_ref[0, :, re], s5_carry_ref[1, :, re],
                               s5_carry_ref[2, :, re], s5_carry_ref[3, :, re]))
        for i in range(4):
            s5_carry_ref[i, :, re] = carry[i]

    ys = _dot(xs_ref[...].astype(MXU_DTYPE), cwt_ref[...]) + sd_ref[...] * u
    ys = _gelu_tanh(ys)
    y_d = ys * _sigmoid(_dot(ys.astype(MXU_DTYPE), wglu_ref[...]) + bglu_ref[...])
    o_ref[...] = (h + _dot(y_c.astype(MXU_DTYPE), wo_ref[:RG_WIDTH, :])
                  + _dot(y_d.astype(MXU_DTYPE), wo_ref[RG_WIDTH:, :]))


def _block_diag(blocks):
    n, r, c = blocks.shape
    eye = jnp.eye(n, dtype=blocks.dtype)
    return (eye[:, None, :, None] * blocks[:, :, None, :]).reshape(n * r, n * c)


def _odd_mixer(h_tb, mix_norm, w_in, conv_w, conv_b, w_a, b_a, w_i, b_i, lam,
               a_re, a_im, log_dt, b_re, b_im, c_re, c_im, d_skip, w_glu, b_glu, w_out, *,
               batch):
    assert batch == SUBLANES // 2
    d = mix_norm.shape[-1]
    h = h_tb.reshape(-1, d)
    rows_total = h.shape[0]
    tm = ODD_ROWS

    dt = jnp.exp(log_dt.astype(F32))[:, None]
    lr, li = a_re.astype(F32), a_im.astype(F32)
    mag = jnp.exp(lr * dt)
    ab_re = mag * jnp.cos(li * dt)
    ab_im = mag * jnp.sin(li * dt)
    den = lr * lr + li * li
    nr, ni = ab_re - 1.0, ab_im
    coef_re = (nr * lr + ni * li) / den
    coef_im = (ni * lr - nr * li) / den
    bb_re = coef_re[..., None] * b_re - coef_im[..., None] * b_im
    bb_im = coef_re[..., None] * b_im + coef_im[..., None] * b_re
    bw = jnp.concatenate([_block_diag(bb_re.transpose(0, 2, 1)),
                          _block_diag(bb_im.transpose(0, 2, 1))], axis=1)
    cwt = jnp.concatenate([_block_diag(c_re.transpose(0, 2, 1)),
                           -_block_diag(c_im.transpose(0, 2, 1))], axis=0)
    a2_re = ab_re * ab_re - ab_im * ab_im
    a2_im = 2.0 * ab_re * ab_im
    sa = jnp.stack([jnp.broadcast_to(v.reshape(1, -1), (SUBLANES, S5_LANES))
                    for v in (ab_re, ab_im, a2_re, a2_im)])
    wgate = jnp.concatenate([_block_diag(w_a), _block_diag(w_i)], axis=1)
    bgate = jnp.concatenate([b_a, b_i]).reshape(1, -1)

    cast = lambda w: w.astype(MXU_DTYPE)
    weights = [mix_norm.reshape(1, d), cast(w_in), conv_w, conv_b.reshape(1, -1), cast(wgate),
               bgate, lam.reshape(1, -1), sa, cast(bw), cast(cwt), d_skip.reshape(1, -1),
               cast(w_glu), b_glu.reshape(1, -1), cast(w_out)]
    row = pl.BlockSpec((tm, d), lambda r: (r, 0))
    out = pl.pallas_call(
        _odd_kernel,
        grid=(rows_total // tm,),
        in_specs=[row] + [_resident(w.shape) for w in weights],
        out_specs=row,
        out_shape=jax.ShapeDtypeStruct((rows_total, d), F32),
        scratch_shapes=[
            pltpu.VMEM((2 * SUBLANES, RG_WIDTH), F32),
            pltpu.VMEM((tm, RG_WIDTH), F32),
            pltpu.VMEM((tm, RG_WIDTH), F32),
            pltpu.VMEM((tm, RG_WIDTH), F32),
            pltpu.VMEM((3, SUBLANES, RG_WIDTH), F32),
            pltpu.VMEM((tm, 2 * S5_LANES), F32),
            pltpu.VMEM((tm, 2 * S5_LANES), F32),
            pltpu.VMEM((4, SUBLANES, S5_LANES), F32),
        ],
        compiler_params=_params("arbitrary"),
        name="odd_mixer",
    )(h, *weights)
    return out.reshape(h_tb.shape)


def kernel(x, p, ffn_a_norm, ffn_a_w1, ffn_a_w3, ffn_a_w2, mix_norm, ffn_b_norm, ffn_b_w1, ffn_b_w3, ffn_b_w2, ple_norm, ple_w_gate, ple_w_up, ev_w_in, mla_q_norm, mla_w_q_up, mla_kv_norm, mla_w_kv_up, gla_w_gate_up, gla_b_gate, gla_out_norm, ev_w_out, od_w_in, rg_conv_w, rg_conv_b, rg_w_a, rg_b_a, rg_w_i, rg_b_i, rg_lambda, s5_a_re, s5_a_im, s5_log_dt, s5_b_re, s5_b_im, s5_c_re, s5_c_im, s5_d, s5_w_glu, s5_b_glu, od_w_out, final_norm):
    batch, seq, d = x.shape
    depth = p.shape[0]
    h = x.reshape(batch * seq, d)
    tb = False
    for i in range(depth):
        j = i // 2
        odd = i % 2 == 1
        h = _ffn(h, ffn_a_norm[i], ffn_a_w1[i], ffn_a_w3[i], ffn_a_w2[i],
                 batch=batch, in_tb=tb, out_tb=odd)
        tb = odd
        if not odd:
            q, k, v, lq, lk, lv, lg, lr = _even_proj(
                h, mix_norm[i], ev_w_in[j], mla_q_norm[j], mla_w_q_up[j], mla_kv_norm[j],
                mla_w_kv_up[j], gla_w_gate_up[j], gla_b_gate[j], batch=batch)
            o_attn = _attention(q, k, v, batch=batch)
            h = _gla_out(lq, lk, lv, lg, lr, o_attn, h, gla_out_norm[j], ev_w_out[j],
                         batch=batch)
        else:
            h = _odd_mixer(h, mix_norm[i], od_w_in[j], rg_conv_w[j], rg_conv_b[j], rg_w_a[j],
                           rg_b_a[j], rg_w_i[j], rg_b_i[j], rg_lambda[j], s5_a_re[j],
                           s5_a_im[j], s5_log_dt[j], s5_b_re[j], s5_b_im[j], s5_c_re[j],
                           s5_c_im[j], s5_d[j], s5_w_glu[j], s5_b_glu[j], od_w_out[j],
                           batch=batch)
        last = i == depth - 1
        h = _ffn(h, ffn_b_norm[i], ffn_b_w1[i], ffn_b_w3[i], ffn_b_w2[i],
                 batch=batch, in_tb=tb, out_tb=False,
                 ple=(p[i], ple_norm[i], ple_w_gate[i], ple_w_up[i]),
                 final_norm=final_norm if last else None)
        tb = False
    return h.reshape(batch, seq, d)
```

```python
import functools
import math

import jax
import jax.numpy as jnp
from jax import lax
from jax.experimental import pallas as pl
from jax.experimental.pallas import tpu as pltpu

MXU_DTYPE = jnp.bfloat16
F32 = jnp.float32

EPS = 1e-6
PLE_DIM = 256
MLA_HEADS = 8
MLA_NOPE = 64
MLA_ROPE = 32
MLA_V = 64
MLA_Q_RANK = 384
MLA_KV_RANK = 256
ROPE_BASE = 10000.0
GLA_HEADS = 4
GLA_DK = 64
GLA_DV = 128
GLA_GATE_RANK = 16
GLA_TAU = 16.0
GLA_CHUNK = 64
RG_WIDTH = 512
RG_BLOCKS = 8
RG_CONV = 4
RG_C = 8.0
S5_GROUP = 16
S5_GROUPS = 32
S5_STATE = 64
S5_WIDTH = S5_GROUP * S5_GROUPS
S5_LANES = S5_GROUPS * S5_STATE

LANES = 128
SUBLANES = 8
HEAD_PAD = LANES
VMEM_LIMIT = 56 * 1024 * 1024

FFN_ROWS = 512
FFN_CHUNKS = 2
PROJ_ROWS = 512
ATTN_Q = 512
GLA_ROWS = 256
ODD_ROWS = 512
SCAN_LANES = 512


def _rms(x, g):
    return x * lax.rsqrt(jnp.mean(x * x, axis=-1, keepdims=True) + EPS) * g


def _dot(a, b):
    return jnp.dot(a, b, preferred_element_type=F32)


def _dot_nt(a, b):
    return lax.dot_general(a, b, (((1,), (1,)), ((), ())), preferred_element_type=F32)


def _dot_tn(a, b):
    return lax.dot_general(a, b, (((0,), (0,)), ((), ())), preferred_element_type=F32)


def _sigmoid(x):
    return 1.0 / (1.0 + jnp.exp(-x))


def _gelu_tanh(x):
    return 0.5 * x * (1.0 + jnp.tanh(math.sqrt(2.0 / math.pi) * (x + 0.044715 * (x * x * x))))


def _log_sigmoid(x):
    return jnp.minimum(x, 0.0) - jnp.log1p(jnp.exp(-jnp.abs(x)))


def _resident(shape):
    return pl.BlockSpec(shape, lambda *_: (0,) * len(shape), pipeline_mode=pl.Buffered(1))


def _params(*semantics):
    return pltpu.CompilerParams(dimension_semantics=semantics, vmem_limit_bytes=VMEM_LIMIT)


def _ffn_kernel(*refs, ple, final):
    h_ref, g_ref, w1_ref, w3_ref, w2_ref = refs[:5]
    rest = list(refs[5:])
    o_ref = rest.pop()
    h = h_ref[...]
    xn = _rms(h, g_ref[...]).astype(MXU_DTYPE)
    d_ff = w1_ref.shape[1]
    fc = d_ff // FFN_CHUNKS
    y = None
    for c in range(FFN_CHUNKS):
        a = _dot(xn, w1_ref[:, c * fc:(c + 1) * fc])
        b = _dot(xn, w3_ref[:, c * fc:(c + 1) * fc])
        t = (a * _sigmoid(a) * b).astype(MXU_DTYPE)
        yc = _dot(t, w2_ref[c * fc:(c + 1) * fc, :])
        y = yc if y is None else y + yc
    h = h + 0.5 * y
    if ple:
        p_ref, gp_ref, wg_ref, wu_ref = rest[:4]
        rest = rest[4:]
        gate = _sigmoid(_dot(_rms(h, gp_ref[...]).astype(MXU_DTYPE), wg_ref[...]))
        up = _dot(p_ref[...].astype(MXU_DTYPE), wu_ref[...])
        h = h + up * gate
    if final:
        h = _rms(h, rest[0][...])
    o_ref[...] = h


def _ffn(h, norm, w1, w3, w2, *, ple=None, final_norm=None):
    tokens, d = h.shape
    tm = FFN_ROWS
    hspec = pl.BlockSpec((tm, d), lambda r: (r, 0))
    cast = lambda w: w.astype(MXU_DTYPE)
    args = [h, norm.reshape(1, d), cast(w1), cast(w3), cast(w2)]
    specs = [hspec, _resident((1, d)), _resident(w1.shape), _resident(w3.shape),
             _resident(w2.shape)]
    if ple is not None:
        p, gp, wg, wu = ple
        args += [p.reshape(tokens, -1), gp.reshape(1, d), cast(wg), cast(wu)]
        specs += [pl.BlockSpec((tm, p.shape[-1]), lambda r: (r, 0)), _resident((1, d)),
                  _resident(wg.shape), _resident(wu.shape)]
    if final_norm is not None:
        args.append(final_norm.reshape(1, d))
        specs.append(_resident((1, d)))
    return pl.pallas_call(
        functools.partial(_ffn_kernel, ple=ple is not None, final=final_norm is not None),
        grid=(tokens // tm,),
        in_specs=specs,
        out_specs=hspec,
        out_shape=jax.ShapeDtypeStruct((tokens, d), F32),
        compiler_params=_params("parallel"),
        name="ffn",
    )(*args)


_EV_COLS = (MLA_Q_RANK, MLA_KV_RANK, HEAD_PAD, HEAD_PAD, GLA_HEADS * HEAD_PAD,
            GLA_HEADS * HEAD_PAD, GLA_HEADS * GLA_DV, HEAD_PAD, GLA_HEADS * GLA_DV)


def _even_proj_kernel(h_ref, gm_ref, win_ref, gq_ref, wq_ref, wqr_ref, gkv_ref, wk_ref, wv_ref,
                      wg_ref, bg_ref, cq_ref, sq_ref, ck_ref, sk_ref,
                      q_ref, k_ref, v_ref, lq_ref, lk_ref, lv_ref, lg_ref, lr_ref):
    xn = _rms(h_ref[...], gm_ref[...]).astype(MXU_DTYPE)
    y = _dot(xn, win_ref[...])
    offs = [0]
    for w in _EV_COLS:
        offs.append(offs[-1] + w)
    c_q, c_kv, kr, kr_rot, lq, lk, lv, g_low, lr = (
        y[:, offs[i]:offs[i + 1]] for i in range(len(_EV_COLS)))

    cqn = _rms(c_q, gq_ref[...]).astype(MXU_DTYPE)
    qf = _dot(cqn, wq_ref[...])
    qr = _dot(cqn, wqr_ref[...])
    ckn = _rms(c_kv, gkv_ref[...]).astype(MXU_DTYPE)
    kn = _dot(ckn, wk_ref[...])
    v_ref[...] = _dot(ckn, wv_ref[...]).astype(v_ref.dtype)
    k_rope = kr * ck_ref[...] + kr_rot * sk_ref[...]
    cq, sq = cq_ref[...], sq_ref[...]
    for hd in range(MLA_HEADS):
        sl = slice(hd * HEAD_PAD, (hd + 1) * HEAD_PAD)
        q_ref[:, sl] = (qf[:, sl] * cq + qr[:, sl] * sq).astype(q_ref.dtype)
        k_ref[:, sl] = (kn[:, sl] + k_rope).astype(k_ref.dtype)

    gate = _dot(g_low.astype(MXU_DTYPE), wg_ref[...]) + bg_ref[...]
    lg_ref[...] = _log_sigmoid(gate) * (1.0 / GLA_TAU)
    lq_ref[...] = lq
    lk_ref[...] = lk
    lv_ref[...] = lv.astype(lv_ref.dtype)
    lr_ref[...] = lr


def _pad_heads(w, heads, width):
    lead = w.shape[:-1]
    w = w.reshape(lead + (heads, width))
    w = jnp.pad(w, [(0, 0)] * len(lead) + [(0, 0), (0, HEAD_PAD - width)])
    return w.reshape(lead + (heads * HEAD_PAD,))


def _rot_half(w):
    half = MLA_ROPE // 2
    return jnp.concatenate([-w[..., half:], w[..., :half]], axis=-1)


def _rope_slot(w):
    pad = [(0, 0)] * (w.ndim - 1) + [(MLA_NOPE, HEAD_PAD - MLA_NOPE - MLA_ROPE)]
    return jnp.pad(w, pad)


def _even_proj(h, mix_norm, w_in, q_norm, w_q_up, kv_norm, w_kv_up, w_gate_up, b_gate, *,
               batch):
    tokens, d = h.shape
    seq = tokens // batch
    tm = PROJ_ROWS
    n_s = seq // tm
    splits = [MLA_Q_RANK, MLA_KV_RANK, MLA_ROPE, GLA_HEADS * GLA_DK, GLA_HEADS * GLA_DK,
              GLA_HEADS * GLA_DV, GLA_GATE_RANK, GLA_HEADS * GLA_DV]
    offs = [0]
    for s in splits:
        offs.append(offs[-1] + s)
    wc_q, wc_kv, w_kr, w_lq, w_lk, w_lv, w_gl, w_lr = (
        w_in[:, offs[i]:offs[i + 1]] for i in range(len(splits)))
    win = jnp.concatenate([
        wc_q, wc_kv, _rope_slot(w_kr), _rope_slot(_rot_half(w_kr)),
        _pad_heads(w_lq, GLA_HEADS, GLA_DK), _pad_heads(w_lk, GLA_HEADS, GLA_DK), w_lv,
        jnp.pad(w_gl, [(0, 0), (0, HEAD_PAD - GLA_GATE_RANK)]), w_lr], axis=1)

    wq = w_q_up.reshape(MLA_Q_RANK, MLA_HEADS, MLA_NOPE + MLA_ROPE)
    wq_nope, wq_rope = wq[..., :MLA_NOPE], wq[..., MLA_NOPE:]
    zeros_nope = jnp.zeros_like(wq_nope)
    wq_full = _pad_heads(jnp.concatenate([wq_nope, wq_rope], -1).reshape(MLA_Q_RANK, -1),
                         MLA_HEADS, MLA_NOPE + MLA_ROPE)
    wq_rot = _pad_heads(jnp.concatenate([zeros_nope, _rot_half(wq_rope)], -1)
                        .reshape(MLA_Q_RANK, -1), MLA_HEADS, MLA_NOPE + MLA_ROPE)
    wkv = w_kv_up.reshape(MLA_KV_RANK, MLA_HEADS, MLA_NOPE + MLA_V)
    wk = _pad_heads(wkv[..., :MLA_NOPE].reshape(MLA_KV_RANK, -1), MLA_HEADS, MLA_NOPE)
    wv = wkv[..., MLA_NOPE:].reshape(MLA_KV_RANK, MLA_HEADS * MLA_V)
    wg = jnp.pad(_pad_heads(w_gate_up, GLA_HEADS, GLA_DK),
                 [(0, HEAD_PAD - GLA_GATE_RANK), (0, 0)])
    bg = _pad_heads(b_gate.reshape(1, -1), GLA_HEADS, GLA_DK)

    half = MLA_ROPE // 2
    inv = ROPE_BASE ** (-jnp.arange(half, dtype=F32) / half)
    ang = jnp.arange(seq).astype(F32)[:, None] * inv[None, :]
    cos2 = jnp.tile(jnp.cos(ang), (1, 2))
    sin2 = jnp.tile(jnp.sin(ang), (1, 2))
    scale = (MLA_NOPE + MLA_ROPE) ** -0.5 * math.log2(math.e)
    ones = jnp.ones((seq, MLA_NOPE), F32)
    tail = jnp.zeros((seq, HEAD_PAD - MLA_NOPE - MLA_ROPE), F32)
    cq = jnp.concatenate([ones, cos2, tail], axis=1) * scale
    sq = jnp.concatenate([0 * ones, sin2, tail], axis=1) * scale
    ck = jnp.concatenate([0 * ones, cos2, tail], axis=1)
    sk = jnp.concatenate([0 * ones, sin2, tail], axis=1)

    cast = lambda w: w.astype(MXU_DTYPE)
    row = lambda w: pl.BlockSpec((tm, w), lambda r: (r, 0))
    tab = pl.BlockSpec((tm, HEAD_PAD), lambda r: (r % n_s, 0))
    gl_w = GLA_HEADS * HEAD_PAD
    gv_w = GLA_HEADS * GLA_DV
    qk_w = MLA_HEADS * HEAD_PAD
    outs = [(qk_w, MXU_DTYPE), (qk_w, MXU_DTYPE), (MLA_HEADS * MLA_V, MXU_DTYPE),
            (gl_w, F32), (gl_w, F32), (gv_w, MXU_DTYPE), (gl_w, F32), (gv_w, F32)]
    weights = [mix_norm.reshape(1, d), cast(win), q_norm.reshape(1, -1), cast(wq_full),
               cast(wq_rot), kv_norm.reshape(1, -1), cast(wk), cast(wv), cast(wg), bg]
    return pl.pallas_call(
        _even_proj_kernel,
        grid=(tokens // tm,),
        in_specs=[row(d)] + [_resident(w.shape) for w in weights] + [tab] * 4,
        out_specs=[row(w) for w, _ in outs],
        out_shape=[jax.ShapeDtypeStruct((tokens, w), dt) for w, dt in outs],
        compiler_params=_params("parallel"),
        name="even_proj",
    )(h, *weights, cq, sq, ck, sk)


def _attn_kernel(q_ref, k_ref, v_ref, o_ref, s_ref, m_ref, l_ref, acc_ref, *, tq):
    qi = pl.program_id(2)
    row = lax.broadcasted_iota(jnp.int32, (tq, tq), 0)
    col = lax.broadcasted_iota(jnp.int32, (tq, tq), 1)
    causal = col <= row
    sls = [slice(hd * HEAD_PAD, (hd + 1) * HEAD_PAD) for hd in range(2)]
    m_ref[...] = jnp.full(m_ref.shape, -jnp.inf, F32)
    l_ref[...] = jnp.zeros(l_ref.shape, F32)
    acc_ref[...] = jnp.zeros(acc_ref.shape, F32)

    def scores(hd, j):
        start = pl.multiple_of(j * tq, tq)
        return _dot_nt(q_ref[:, sls[hd]], k_ref[pl.ds(start, tq), sls[hd]])

    def softmax_pv(hd, s, j, masked):
        start = pl.multiple_of(j * tq, tq)
        if masked:
            s = jnp.where(causal, s, -jnp.inf)
        m = m_ref[hd]
        m_new = jnp.maximum(m, jnp.max(s, axis=1, keepdims=True))
        alpha = jnp.exp2(m - m_new)
        p = jnp.exp2(s - jnp.concatenate([m_new] * (tq // LANES), axis=1))
        l_ref[hd] = alpha * l_ref[hd] + jnp.sum(p, axis=1, keepdims=True)
        m_ref[hd] = m_new
        acc_ref[hd] = alpha * acc_ref[hd] + _dot(p.astype(MXU_DTYPE),
                                                 v_ref[pl.ds(start, tq), :])

    s_ref[...] = scores(0, 0)

    def body(j, _):
        s_b = scores(1, j)
        softmax_pv(0, s_ref[...], j, False)
        s_ref[...] = scores(0, j + 1)
        softmax_pv(1, s_b, j, False)
        return 0

    lax.fori_loop(0, qi, body, 0)
    s_b = scores(1, qi)
    softmax_pv(0, s_ref[...], qi, True)
    softmax_pv(1, s_b, qi, True)
    lane = lax.broadcasted_iota(jnp.int32, (tq, 2 * MLA_V), 1)
    o_ref[...] = jnp.where(lane < MLA_V, acc_ref[0] / l_ref[0],
                           acc_ref[1] / l_ref[1]).astype(o_ref.dtype)


def _attention(q, k, v, *, batch):
    tokens = q.shape[0]
    seq = tokens // batch
    tq = ATTN_Q
    q3 = q.reshape(batch, seq, -1)
    k3 = k.reshape(batch, seq, -1)
    v3 = v.reshape(batch, seq, -1)
    pair = 2 * HEAD_PAD
    out = pl.pallas_call(
        functools.partial(_attn_kernel, tq=tq),
        grid=(batch, MLA_HEADS // 2, seq // tq),
        in_specs=[pl.BlockSpec((None, tq, pair), lambda b, hp, i: (b, i, hp)),
                  pl.BlockSpec((None, seq, pair), lambda b, hp, i: (b, 0, hp)),
                  pl.BlockSpec((None, seq, 2 * MLA_V), lambda b, hp, i: (b, 0, hp))],
        out_specs=pl.BlockSpec((None, tq, 2 * MLA_V), lambda b, hp, i: (b, i, hp)),
        out_shape=jax.ShapeDtypeStruct(v3.shape, MXU_DTYPE),
        scratch_shapes=[pltpu.VMEM((tq, tq), F32), pltpu.VMEM((2, tq, LANES), F32),
                        pltpu.VMEM((2, tq, LANES), F32), pltpu.VMEM((2, tq, 2 * MLA_V), F32)],
        compiler_params=_params("parallel", "parallel", "arbitrary"),
        name="attention",
    )(q3, k3, v3)
    return out.reshape(tokens, -1)


def _gla_out_kernel(lq_ref, lk_ref, lv_ref, lg_ref, lr_ref, oa_ref, h_ref, gn_ref, wo_ref,
                    o_ref, state_ref, y_ref):
    @pl.when(pl.program_id(1) == 0)
    def _():
        state_ref[...] = jnp.zeros_like(state_ref)

    tt, width = lg_ref.shape
    chunk = GLA_CHUNK
    pos = lax.broadcasted_iota(jnp.int32, (tt, width), 0) & (chunk - 1)
    b = lg_ref[...]
    d = 1
    while d < chunk:
        b = b + jnp.where(pos >= d, pltpu.roll(b, d, axis=0), 0.0)
        d *= 2
    lk = lk_ref[...]
    q_e = (lq_ref[...] * (GLA_DK ** -0.5) * jnp.exp(b)).astype(MXU_DTYPE)
    k_e = (lk * jnp.exp(-b)).astype(MXU_DTYPE)
    tri = (lax.broadcasted_iota(jnp.int32, (chunk, chunk), 1)
           <= lax.broadcasted_iota(jnp.int32, (chunk, chunk), 0))
    gn = gn_ref[...]
    for c in range(tt // chunk):
        rows = slice(c * chunk, (c + 1) * chunk)
        b_c = b[rows]
        b_last = b_c[chunk - 1:chunk]
        k_end = (lk[rows] * jnp.exp(b_last - b_c)).astype(MXU_DTYPE)
        decay = jnp.exp(b_last)
        for hd in range(GLA_HEADS):
            sl = slice(hd * HEAD_PAD, (hd + 1) * HEAD_PAD)
            vs = slice(hd * GLA_DV, (hd + 1) * GLA_DV)
            v_c = lv_ref[rows, vs]
            att = jnp.where(tri, _dot_nt(q_e[rows, sl], k_e[rows, sl]), 0.0)
            st = state_ref[hd]
            o = _dot(att.astype(MXU_DTYPE), v_c) + _dot_nt(q_e[rows, sl], st.astype(MXU_DTYPE))
            state_ref[hd] = st * decay[:, sl] + _dot_tn(v_c, k_end[:, sl])
            o = _rms(o, gn)
            r = lr_ref[rows, vs]
            y_ref[rows, vs] = (o * (r * _sigmoid(r))).astype(y_ref.dtype)
    n_a = oa_ref.shape[1]
    o_ref[...] = (h_ref[...] + _dot(oa_ref[...], wo_ref[:n_a, :])
                  + _dot(y_ref[...], wo_ref[n_a:, :]))


def _gla_out(lq, lk, lv, lg, lr, o_attn, h, out_norm, w_out, *, batch):
    tokens, d = h.shape
    seq = tokens // batch
    tt = GLA_ROWS
    n_s = seq // tt
    row = lambda w: pl.BlockSpec((tt, w), lambda b, i: (b * n_s + i, 0))
    ins = [lq, lk, lv, lg, lr, o_attn, h]
    return pl.pallas_call(
        _gla_out_kernel,
        grid=(batch, n_s),
        in_specs=[row(a.shape[1]) for a in ins] + [_resident((1, GLA_DV)),
                                                   _resident(w_out.shape)],
        out_specs=row(d),
        out_shape=jax.ShapeDtypeStruct((tokens, d), F32),
        scratch_shapes=[pltpu.VMEM((GLA_HEADS, GLA_DV, HEAD_PAD), F32),
                        pltpu.VMEM((tt, GLA_HEADS * GLA_DV), MXU_DTYPE)],
        compiler_params=_params("parallel", "arbitrary"),
        name="gla_out",
    )(*ins, out_norm.reshape(1, -1), w_out.astype(MXU_DTYPE))


def _interleave(src_ref, slab_ref):
    batch, steps, d = src_ref.shape
    for s in range(d // LANES):
        for b in range(batch):
            slab_ref[s, pl.ds(b, steps, stride=batch), :] = src_ref[b, :, s * LANES:(s + 1) * LANES]
    return jnp.concatenate([slab_ref[s] for s in range(d // LANES)], axis=1)


def _deinterleave(x, slab_ref, dst_ref):
    batch, steps, d = dst_ref.shape
    for s in range(d // LANES):
        slab_ref[s] = x[:, s * LANES:(s + 1) * LANES]
    for s in range(d // LANES):
        for b in range(batch):
            dst_ref[b, :, s * LANES:(s + 1) * LANES] = slab_ref[s, pl.ds(b, steps, stride=batch), :]

def _prev_step(cur, prev_rolled, sub):
    cur_rolled = pltpu.roll(cur, SUBLANES // 2, axis=0)
    return jnp.where(sub < SUBLANES // 2, prev_rolled, cur_rolled), cur_rolled


def _odd_kernel(h_ref, gm_ref, win_ref, cw_ref, cb_ref, wgate_ref, bgate_ref, lam_ref,
                sa_ref, bw_ref, cwt_ref, sd_ref, wglu_ref, bglu_ref, wo_ref,
                o_ref,
                halo_ref, rg_a_ref, rg_b_ref, rg_x_ref, rg_carry_ref,
                bu_ref, xs_ref, s5_carry_ref, slab_ref):
    first = pl.program_id(0) == 0

    @pl.when(first)
    def _():
        halo_ref[...] = jnp.zeros_like(halo_ref)
        rg_carry_ref[...] = jnp.zeros_like(rg_carry_ref)
        s5_carry_ref[...] = jnp.zeros_like(s5_carry_ref)

    tm = h_ref.shape[0] * h_ref.shape[1]
    groups = tm // SUBLANES
    h = _interleave(h_ref, slab_ref)
    xn = _rms(h, gm_ref[...]).astype(MXU_DTYPE)
    y = _dot(xn, win_ref[...])
    x_gate, x_rg, u = (y[:, i * RG_WIDTH:(i + 1) * RG_WIDTH] for i in range(3))

    step = SUBLANES // 2
    halo_rows = halo_ref.shape[0]
    ext = jnp.concatenate([halo_ref[...], x_rg], axis=0)
    halo_ref[...] = x_rg[tm - halo_rows:, :]
    cw = cw_ref[...]
    xc = cb_ref[...] + x_rg * cw[RG_CONV - 1:RG_CONV]
    for j in range(RG_CONV - 1):
        back = (RG_CONV - 1 - j) * step
        xc = xc + ext[halo_rows - back:halo_rows - back + tm, :] * cw[j:j + 1]

    gates = _sigmoid(_dot(xc.astype(MXU_DTYPE), wgate_ref[...]) + bgate_ref[...])
    r_gate, i_gate = gates[:, :RG_WIDTH], gates[:, RG_WIDTH:]
    lam = lam_ref[...]
    softplus_neg = jnp.maximum(-lam, 0.0) + jnp.log1p(jnp.exp(-jnp.abs(lam)))
    log_a = (-RG_C * softplus_neg) * r_gate
    a = jnp.exp(log_a)
    rg_a_ref[...] = a
    rg_b_ref[...] = jnp.sqrt(-jnp.tanh(log_a) * (a * a + 1.0)) * (i_gate * xc)

    sub = lax.broadcasted_iota(jnp.int32, (SUBLANES, RG_WIDTH), 0)

    def rg_body(g, carry):
        x, ra_prev, rb_prev = carry
        rows = pl.ds(pl.multiple_of(g * SUBLANES, SUBLANES), SUBLANES)
        a = rg_a_ref[rows, :]
        bx = rg_b_ref[rows, :]
        a_prev, ra = _prev_step(a, ra_prev, sub)
        b_prev, rb = _prev_step(bx, rb_prev, sub)
        x = (a * a_prev) * x + (a * b_prev + bx)
        rg_x_ref[rows, :] = x
        return x, ra, rb

    carry = lax.fori_loop(0, groups, rg_body,
                          (rg_carry_ref[0], rg_carry_ref[1], rg_carry_ref[2]))
    for i in range(3):
        rg_carry_ref[i] = carry[i]
    y_c = rg_x_ref[...] * _gelu_tanh(x_gate)

    ub = u.astype(MXU_DTYPE)
    half_u = S5_WIDTH // 2
    half_x = S5_LANES // 2
    for hf in range(2):
        bu_ref[:, 2 * hf * half_x:2 * (hf + 1) * half_x] = _dot(
            ub[:, hf * half_u:(hf + 1) * half_u], bw_ref[hf])
    sub_s = lax.broadcasted_iota(jnp.int32, (SUBLANES, SCAN_LANES), 0)
    for lc in range(S5_LANES // SCAN_LANES):
        hf, within = divmod(lc * SCAN_LANES, half_x)
        coef = slice(lc * SCAN_LANES, (lc + 1) * SCAN_LANES)
        re = slice(2 * hf * half_x + within, 2 * hf * half_x + within + SCAN_LANES)
        im = slice(re.start + half_x, re.stop + half_x)
        a_re, a_im = sa_ref[0, :, coef], sa_ref[1, :, coef]
        a2_re, a2_im = sa_ref[2, :, coef], sa_ref[3, :, coef]

        def s5_body(g, carry):
            x_re, x_im, pr_re, pr_im = carry
            rows = pl.ds(pl.multiple_of(g * SUBLANES, SUBLANES), SUBLANES)
            c_re = bu_ref[rows, re]
            c_im = bu_ref[rows, im]
            p_re, pr_re = _prev_step(c_re, pr_re, sub_s)
            p_im, pr_im = _prev_step(c_im, pr_im, sub_s)
            c_re = c_re + (a_re * p_re - a_im * p_im)
            c_im = c_im + (a_re * p_im + a_im * p_re)
            n_re = a2_re * x_re - a2_im * x_im + c_re
            n_im = a2_re * x_im + a2_im * x_re + c_im
            xs_ref[rows, re] = n_re
            xs_ref[rows, im] = n_im
            return n_re, n_im, pr_re, pr_im

        carry = lax.fori_loop(0, groups, s5_body,
                              (s5_carry_ref[0, :, coef], s5_carry_ref[1, :, coef],
                               s5_carry_ref[2, :, coef], s5_carry_ref[3, :, coef]))
        for i in range(4):
            s5_carry_ref[i, :, coef] = carry[i]

    ys = jnp.concatenate(
        [_dot(xs_ref[:, 2 * hf * half_x:2 * (hf + 1) * half_x].astype(MXU_DTYPE), cwt_ref[hf])
         for hf in range(2)], axis=1) + sd_ref[...] * u
    ys = _gelu_tanh(ys)
    y_d = ys * _sigmoid(_dot(ys.astype(MXU_DTYPE), wglu_ref[...]) + bglu_ref[...])
    out = (h + _dot(y_c.astype(MXU_DTYPE), wo_ref[:RG_WIDTH, :])
           + _dot(y_d.astype(MXU_DTYPE), wo_ref[RG_WIDTH:, :]))
    _deinterleave(out, slab_ref, o_ref)


def _block_diag(blocks):
    n, r, c = blocks.shape
    eye = jnp.eye(n, dtype=blocks.dtype)
    return (eye[:, None, :, None] * blocks[:, :, None, :]).reshape(n * r, n * c)


def _odd_mixer(h, mix_norm, w_in, conv_w, conv_b, w_a, b_a, w_i, b_i, lam,
               a_re, a_im, log_dt, b_re, b_im, c_re, c_im, d_skip, w_glu, b_glu, w_out, *,
               batch):
    assert batch == SUBLANES // 2
    tokens, d = h.shape
    seq = tokens // batch
    tm = ODD_ROWS
    steps = tm // batch

    dt = jnp.exp(log_dt.astype(F32))[:, None]
    lr, li = a_re.astype(F32), a_im.astype(F32)
    mag = jnp.exp(lr * dt)
    ab_re = mag * jnp.cos(li * dt)
    ab_im = mag * jnp.sin(li * dt)
    den = lr * lr + li * li
    nr, ni = ab_re - 1.0, ab_im
    coef_re = (nr * lr + ni * li) / den
    coef_im = (ni * lr - nr * li) / den
    bb_re = coef_re[..., None] * b_re - coef_im[..., None] * b_im
    bb_im = coef_re[..., None] * b_im + coef_im[..., None] * b_re
    halves = [slice(0, S5_GROUPS // 2), slice(S5_GROUPS // 2, S5_GROUPS)]
    bw = jnp.stack([jnp.concatenate([_block_diag(bb_re[g].transpose(0, 2, 1)),
                                     _block_diag(bb_im[g].transpose(0, 2, 1))], axis=1)
                    for g in halves])
    cwt = jnp.stack([jnp.concatenate([_block_diag(c_re[g].transpose(0, 2, 1)),
                                      -_block_diag(c_im[g].transpose(0, 2, 1))], axis=0)
                     for g in halves])
    a2_re = ab_re * ab_re - ab_im * ab_im
    a2_im = 2.0 * ab_re * ab_im
    sa = jnp.stack([jnp.broadcast_to(v.reshape(1, -1), (SUBLANES, S5_LANES))
                    for v in (ab_re, ab_im, a2_re, a2_im)])
    wgate = jnp.concatenate([_block_diag(w_a), _block_diag(w_i)], axis=1)
    bgate = jnp.concatenate([b_a, b_i]).reshape(1, -1)

    cast = lambda w: w.astype(MXU_DTYPE)
    weights = [mix_norm.reshape(1, d), cast(w_in), conv_w, conv_b.reshape(1, -1), cast(wgate),
               bgate, lam.reshape(1, -1), sa, cast(bw), cast(cwt), d_skip.reshape(1, -1),
               cast(w_glu), b_glu.reshape(1, -1), cast(w_out)]
    row = pl.BlockSpec((batch, steps, d), lambda r: (0, r, 0))
    out = pl.pallas_call(
        _odd_kernel,
        grid=(seq // steps,),
        in_specs=[row] + [_resident(w.shape) for w in weights],
        out_specs=row,
        out_shape=jax.ShapeDtypeStruct((batch, seq, d), F32),
        scratch_shapes=[
            pltpu.VMEM((2 * SUBLANES, RG_WIDTH), F32),
            pltpu.VMEM((tm, RG_WIDTH), F32),
            pltpu.VMEM((tm, RG_WIDTH), F32),
            pltpu.VMEM((tm, RG_WIDTH), F32),
            pltpu.VMEM((3, SUBLANES, RG_WIDTH), F32),
            pltpu.VMEM((tm, 2 * S5_LANES), F32),
            pltpu.VMEM((tm, 2 * S5_LANES), F32),
            pltpu.VMEM((4, SUBLANES, S5_LANES), F32),
            pltpu.VMEM((d // LANES, tm, LANES), F32),
        ],
        compiler_params=_params("arbitrary"),
        name="odd_mixer",
    )(h.reshape(batch, seq, d), *weights)
    return out.reshape(tokens, d)


def kernel(x, p, ffn_a_norm, ffn_a_w1, ffn_a_w3, ffn_a_w2, mix_norm, ffn_b_norm, ffn_b_w1, ffn_b_w3, ffn_b_w2, ple_norm, ple_w_gate, ple_w_up, ev_w_in, mla_q_norm, mla_w_q_up, mla_kv_norm, mla_w_kv_up, gla_w_gate_up, gla_b_gate, gla_out_norm, ev_w_out, od_w_in, rg_conv_w, rg_conv_b, rg_w_a, rg_b_a, rg_w_i, rg_b_i, rg_lambda, s5_a_re, s5_a_im, s5_log_dt, s5_b_re, s5_b_im, s5_c_re, s5_c_im, s5_d, s5_w_glu, s5_b_glu, od_w_out, final_norm):
    batch, seq, d = x.shape
    depth = p.shape[0]
    h = x.reshape(batch * seq, d)
    for i in range(depth):
        j = i // 2
        h = _ffn(h, ffn_a_norm[i], ffn_a_w1[i], ffn_a_w3[i], ffn_a_w2[i])
        if i % 2 == 0:
            q, k, v, lq, lk, lv, lg, lr = _even_proj(
                h, mix_norm[i], ev_w_in[j], mla_q_norm[j], mla_w_q_up[j], mla_kv_norm[j],
                mla_w_kv_up[j], gla_w_gate_up[j], gla_b_gate[j], batch=batch)
            o_attn = _attention(q, k, v, batch=batch)
            h = _gla_out(lq, lk, lv, lg, lr, o_attn, h, gla_out_norm[j], ev_w_out[j],
                         batch=batch)
        else:
            h = _odd_mixer(h, mix_norm[i], od_w_in[j], rg_conv_w[j], rg_conv_b[j], rg_w_a[j],
                           rg_b_a[j], rg_w_i[j], rg_b_i[j], rg_lambda[j], s5_a_re[j],
                           s5_a_im[j], s5_log_dt[j], s5_b_re[j], s5_b_im[j], s5_c_re[j],
                           s5_c_im[j], s5_d[j], s5_w_glu[j], s5_b_glu[j], od_w_out[j],
                           batch=batch)
        last = i == depth - 1
        h = _ffn(h, ffn_b_norm[i], ffn_b_w1[i], ffn_b_w3[i], ffn_b_w2[i],
                 ple=(p[i], ple_norm[i], ple_w_gate[i], ple_w_up[i]),
                 final_norm=final_norm if last else None)
    return h.reshape(batch, seq, d)
```

```python
import functools
import math

import jax
import jax.numpy as jnp
from jax import lax
from jax.experimental import pallas as pl
from jax.experimental.pallas import tpu as pltpu

MXU_DTYPE = jnp.bfloat16
F32 = jnp.float32

EPS = 1e-6
PLE_DIM = 256
MLA_HEADS = 8
MLA_NOPE = 64
MLA_ROPE = 32
MLA_V = 64
MLA_Q_RANK = 384
MLA_KV_RANK = 256
ROPE_BASE = 10000.0
GLA_HEADS = 4
GLA_DK = 64
GLA_DV = 128
GLA_GATE_RANK = 16
GLA_TAU = 16.0
GLA_CHUNK = 64
RG_WIDTH = 512
RG_BLOCKS = 8
RG_CONV = 4
RG_C = 8.0
S5_GROUP = 16
S5_GROUPS = 32
S5_STATE = 64
S5_WIDTH = S5_GROUP * S5_GROUPS
S5_LANES = S5_GROUPS * S5_STATE

LANES = 128
SUBLANES = 8
HEAD_PAD = LANES
VMEM_LIMIT = 56 * 1024 * 1024

FFN_ROWS = 512
FFN_CHUNKS = 2
PROJ_ROWS = 512
ATTN_Q = 512
GLA_ROWS = 256
ODD_ROWS = 512
SCAN_LANES = 512


def _rms(x, g):
    return x * lax.rsqrt(jnp.mean(x * x, axis=-1, keepdims=True) + EPS) * g


def _dot(a, b):
    return jnp.dot(a, b, preferred_element_type=F32)


def _dot_nt(a, b):
    return lax.dot_general(a, b, (((1,), (1,)), ((), ())), preferred_element_type=F32)


def _dot_tn(a, b):
    return lax.dot_general(a, b, (((0,), (0,)), ((), ())), preferred_element_type=F32)


def _sigmoid(x):
    return 1.0 / (1.0 + jnp.exp(-x))


def _gelu_tanh(x):
    return 0.5 * x * (1.0 + jnp.tanh(math.sqrt(2.0 / math.pi) * (x + 0.044715 * (x * x * x))))


def _log_sigmoid(x):
    return jnp.minimum(x, 0.0) - jnp.log1p(jnp.exp(-jnp.abs(x)))


def _resident(shape):
    return pl.BlockSpec(shape, lambda *_: (0,) * len(shape), pipeline_mode=pl.Buffered(1))


def _params(*semantics):
    return pltpu.CompilerParams(dimension_semantics=semantics, vmem_limit_bytes=VMEM_LIMIT)


def _ffn_kernel(*refs, ple, final):
    h_ref, g_ref, w1_ref, w3_ref, w2_ref = refs[:5]
    rest = list(refs[5:])
    o_ref = rest.pop()
    h = h_ref[...]
    xn = _rms(h, g_ref[...]).astype(MXU_DTYPE)
    d_ff = w1_ref.shape[1]
    fc = d_ff // FFN_CHUNKS
    y = None
    for c in range(FFN_CHUNKS):
        a = _dot(xn, w1_ref[:, c * fc:(c + 1) * fc])
        b = _dot(xn, w3_ref[:, c * fc:(c + 1) * fc])
        t = (a * _sigmoid(a) * b).astype(MXU_DTYPE)
        yc = _dot(t, w2_ref[c * fc:(c + 1) * fc, :])
        y = yc if y is None else y + yc
    h = h + 0.5 * y
    if ple:
        p_ref, gp_ref, wg_ref, wu_ref = rest[:4]
        rest = rest[4:]
        gate = _sigmoid(_dot(_rms(h, gp_ref[...]).astype(MXU_DTYPE), wg_ref[...]))
        up = _dot(p_ref[...].astype(MXU_DTYPE), wu_ref[...])
        h = h + up * gate
    if final:
        h = _rms(h, rest[0][...])
    o_ref[...] = h


def _ffn(h, layer, norm, w1, w3, w2, *, ple=None, final_norm=None):
    tokens, d = h.shape
    tm = FFN_ROWS
    hspec = pl.BlockSpec((tm, d), lambda r: (r, 0))

    def of_layer(a):
        rest = a.shape[1:]
        return pl.BlockSpec((None,) + rest, lambda r: (layer,) + (0,) * len(rest),
                            pipeline_mode=pl.Buffered(1))

    gains = lambda g: g.reshape(g.shape[0], 1, d)
    args = [h, gains(norm), w1, w3, w2]
    specs = [hspec] + [of_layer(a) for a in args[1:]]
    if ple is not None:
        p, gp, wg, wu = ple
        p = p.reshape(p.shape[0], tokens, p.shape[-1])
        args += [p, gains(gp), wg, wu]
        specs += [pl.BlockSpec((None, tm, p.shape[-1]), lambda r: (layer, r, 0)),
                  of_layer(args[-3]), of_layer(wg), of_layer(wu)]
    if final_norm is not None:
        args.append(final_norm.reshape(1, d))
        specs.append(_resident((1, d)))
    return pl.pallas_call(
        functools.partial(_ffn_kernel, ple=ple is not None, final=final_norm is not None),
        grid=(tokens // tm,),
        in_specs=specs,
        out_specs=hspec,
        out_shape=jax.ShapeDtypeStruct((tokens, d), F32),
        compiler_params=_params("parallel"),
        name="ffn",
    )(*args)


_EV_COLS = (MLA_Q_RANK, MLA_KV_RANK, HEAD_PAD, GLA_HEADS * GLA_DK, GLA_HEADS * GLA_DK,
            GLA_HEADS * GLA_DV, HEAD_PAD, GLA_HEADS * GLA_DV)
ROT_SHIFT = HEAD_PAD - MLA_ROPE


def _even_proj_kernel(h_ref, gm_ref, win_ref, gq_ref, wq_ref, gkv_ref, wk_ref, wv_ref,
                      wg_ref, bg_ref, cq_ref, sq_ref, ck_ref, sk_ref,
                      q_ref, k_ref, v_ref, lq_ref, lk_ref, lv_ref, lg_ref, lr_ref):
    xn = _rms(h_ref[...], gm_ref[...]).astype(MXU_DTYPE)
    y = _dot(xn, win_ref[...])
    offs = [0]
    for w in _EV_COLS:
        offs.append(offs[-1] + w)
    c_q, c_kv, kr, lq, lk, lv, g_low, lr = (
        y[:, offs[i]:offs[i + 1]] for i in range(len(_EV_COLS)))

    cqn = _rms(c_q, gq_ref[...]).astype(MXU_DTYPE)
    qf = _dot(cqn, wq_ref[...])
    ckn = _rms(c_kv, gkv_ref[...]).astype(MXU_DTYPE)
    kn = _dot(ckn, wk_ref[...])
    v_ref[...] = _dot(ckn, wv_ref[...]).astype(v_ref.dtype)
    k_rope = kr * ck_ref[...] + pltpu.roll(kr, ROT_SHIFT, axis=1) * sk_ref[...]
    cq, sq = cq_ref[...], sq_ref[...]
    for hd in range(MLA_HEADS):
        sl = slice(hd * HEAD_PAD, (hd + 1) * HEAD_PAD)
        q_h = qf[:, sl]
        q_ref[:, sl] = (q_h * cq + pltpu.roll(q_h, ROT_SHIFT, axis=1) * sq).astype(q_ref.dtype)
        k_ref[:, sl] = (kn[:, sl] + k_rope).astype(k_ref.dtype)

    gate = _dot(g_low.astype(MXU_DTYPE), wg_ref[...]) + bg_ref[...]
    lg_ref[...] = _log_sigmoid(gate) * (1.0 / GLA_TAU)
    lq_ref[...] = lq
    lk_ref[...] = lk
    lv_ref[...] = lv.astype(lv_ref.dtype)
    lr_ref[...] = lr


def _pad_heads(w, heads, width):
    lead = w.shape[:-1]
    w = w.reshape(lead + (heads, width))
    w = jnp.pad(w, [(0, 0)] * len(lead) + [(0, 0), (0, HEAD_PAD - width)])
    return w.reshape(lead + (heads * HEAD_PAD,))


def _rot_half(w):
    half = MLA_ROPE // 2
    return jnp.concatenate([-w[..., half:], w[..., :half]], axis=-1)


def _head_slot(w_nope, w_rope):
    return jnp.concatenate([w_nope, w_rope, _rot_half(w_rope)], axis=-1)


def _even_proj(h, mix_norm, w_in, q_norm, w_q_up, kv_norm, w_kv_up, w_gate_up, b_gate, *,
               batch):
    tokens, d = h.shape
    seq = tokens // batch
    tm = PROJ_ROWS
    n_s = seq // tm
    splits = [MLA_Q_RANK, MLA_KV_RANK, MLA_ROPE, GLA_HEADS * GLA_DK, GLA_HEADS * GLA_DK,
              GLA_HEADS * GLA_DV, GLA_GATE_RANK, GLA_HEADS * GLA_DV]
    offs = [0]
    for s in splits:
        offs.append(offs[-1] + s)
    wc_q, wc_kv, w_kr, w_lq, w_lk, w_lv, w_gl, w_lr = (
        w_in[:, offs[i]:offs[i + 1]] for i in range(len(splits)))
    win = jnp.concatenate([
        wc_q, wc_kv, _head_slot(jnp.zeros((d, MLA_NOPE), w_kr.dtype), w_kr), w_lq, w_lk, w_lv,
        jnp.pad(w_gl, [(0, 0), (0, HEAD_PAD - GLA_GATE_RANK)]), w_lr], axis=1)

    wq = w_q_up.reshape(MLA_Q_RANK, MLA_HEADS, MLA_NOPE + MLA_ROPE)
    wq_full = _head_slot(wq[..., :MLA_NOPE], wq[..., MLA_NOPE:]).reshape(MLA_Q_RANK, -1)
    wkv = w_kv_up.reshape(MLA_KV_RANK, MLA_HEADS, MLA_NOPE + MLA_V)
    wk = _pad_heads(wkv[..., :MLA_NOPE].reshape(MLA_KV_RANK, -1), MLA_HEADS, MLA_NOPE)
    wv = wkv[..., MLA_NOPE:].reshape(MLA_KV_RANK, MLA_HEADS * MLA_V)
    wg = jnp.pad(w_gate_up, [(0, HEAD_PAD - GLA_GATE_RANK), (0, 0)])
    bg = b_gate.reshape(1, -1)

    half = MLA_ROPE // 2
    inv = ROPE_BASE ** (-jnp.arange(half, dtype=F32) / half)
    ang = jnp.arange(seq).astype(F32)[:, None] * inv[None, :]
    cos2 = jnp.tile(jnp.cos(ang), (1, 2))
    sin2 = jnp.tile(jnp.sin(ang), (1, 2))
    scale = (MLA_NOPE + MLA_ROPE) ** -0.5 * math.log2(math.e)
    ones = jnp.ones((seq, MLA_NOPE), F32)
    tail = jnp.zeros((seq, HEAD_PAD - MLA_NOPE - MLA_ROPE), F32)
    cq = jnp.concatenate([ones, cos2, tail], axis=1) * scale
    sq = jnp.concatenate([0 * ones, sin2, tail], axis=1) * scale
    ck = jnp.concatenate([0 * ones, cos2, tail], axis=1)
    sk = jnp.concatenate([0 * ones, sin2, tail], axis=1)

    cast = lambda w: w.astype(MXU_DTYPE)
    row = lambda w: pl.BlockSpec((tm, w), lambda r: (r, 0))
    tab = pl.BlockSpec((tm, HEAD_PAD), lambda r: (r % n_s, 0))
    gl_w = GLA_HEADS * GLA_DK
    gv_w = GLA_HEADS * GLA_DV
    qk_w = MLA_HEADS * HEAD_PAD
    outs = [(qk_w, MXU_DTYPE), (qk_w, MXU_DTYPE), (MLA_HEADS * MLA_V, MXU_DTYPE),
            (gl_w, F32), (gl_w, F32), (gv_w, MXU_DTYPE), (gl_w, F32), (gv_w, F32)]
    weights = [mix_norm.reshape(1, d), cast(win), q_norm.reshape(1, -1), cast(wq_full),
               kv_norm.reshape(1, -1), cast(wk), cast(wv), cast(wg), bg]
    return pl.pallas_call(
        _even_proj_kernel,
        grid=(tokens // tm,),
        in_specs=[row(d)] + [_resident(w.shape) for w in weights] + [tab] * 4,
        out_specs=[row(w) for w, _ in outs],
        out_shape=[jax.ShapeDtypeStruct((tokens, w), dt) for w, dt in outs],
        compiler_params=_params("parallel"),
        name="even_proj",
    )(h, *weights, cq, sq, ck, sk)


def _attn_kernel(q_ref, k_ref, v_ref, o_ref, s_ref, m_ref, l_ref, acc_ref, *, tq):
    qi = pl.program_id(2)
    row = lax.broadcasted_iota(jnp.int32, (tq, tq), 0)
    col = lax.broadcasted_iota(jnp.int32, (tq, tq), 1)
    causal = col <= row
    sls = [slice(hd * HEAD_PAD, (hd + 1) * HEAD_PAD) for hd in range(2)]
    m_ref[...] = jnp.full(m_ref.shape, -jnp.inf, F32)
    l_ref[...] = jnp.zeros(l_ref.shape, F32)
    acc_ref[...] = jnp.zeros(acc_ref.shape, F32)

    def scores(hd, j):
        start = pl.multiple_of(j * tq, tq)
        return _dot_nt(q_ref[:, sls[hd]], k_ref[pl.ds(start, tq), sls[hd]])

    def softmax_pv(hd, s, j, masked):
        start = pl.multiple_of(j * tq, tq)
        if masked:
            s = jnp.where(causal, s, -jnp.inf)
        m = m_ref[hd]
        m_new = jnp.maximum(m, jnp.max(s, axis=1, keepdims=True))
        alpha = jnp.exp2(m - m_new)
        p = jnp.exp2(s - jnp.concatenate([m_new] * (tq // LANES), axis=1))
        l_ref[hd] = alpha * l_ref[hd] + jnp.sum(p, axis=1, keepdims=True)
        m_ref[hd] = m_new
        acc_ref[hd] = alpha * acc_ref[hd] + _dot(p.astype(MXU_DTYPE),
                                                 v_ref[pl.ds(start, tq), :])

    s_ref[...] = scores(0, 0)

    def body(j, _):
        s_b = scores(1, j)
        softmax_pv(0, s_ref[...], j, False)
        s_ref[...] = scores(0, j + 1)
        softmax_pv(1, s_b, j, False)
        return 0

    lax.fori_loop(0, qi, body, 0)
    s_b = scores(1, qi)
    softmax_pv(0, s_ref[...], qi, True)
    softmax_pv(1, s_b, qi, True)
    lane = lax.broadcasted_iota(jnp.int32, (tq, 2 * MLA_V), 1)
    o_ref[...] = jnp.where(lane < MLA_V, acc_ref[0] / l_ref[0],
                           acc_ref[1] / l_ref[1]).astype(o_ref.dtype)


def _attention(q, k, v, *, batch):
    tokens = q.shape[0]
    seq = tokens // batch
    tq = ATTN_Q
    q3 = q.reshape(batch, seq, -1)
    k3 = k.reshape(batch, seq, -1)
    v3 = v.reshape(batch, seq, -1)
    pair = 2 * HEAD_PAD
    out = pl.pallas_call(
        functools.partial(_attn_kernel, tq=tq),
        grid=(batch, MLA_HEADS // 2, seq // tq),
        in_specs=[pl.BlockSpec((None, tq, pair), lambda b, hp, i: (b, i, hp)),
                  pl.BlockSpec((None, seq, pair), lambda b, hp, i: (b, 0, hp)),
                  pl.BlockSpec((None, seq, 2 * MLA_V), lambda b, hp, i: (b, 0, hp))],
        out_specs=pl.BlockSpec((None, tq, 2 * MLA_V), lambda b, hp, i: (b, i, hp)),
        out_shape=jax.ShapeDtypeStruct(v3.shape, MXU_DTYPE),
        scratch_shapes=[pltpu.VMEM((tq, tq), F32), pltpu.VMEM((2, tq, LANES), F32),
                        pltpu.VMEM((2, tq, LANES), F32), pltpu.VMEM((2, tq, 2 * MLA_V), F32)],
        compiler_params=_params("parallel", "parallel", "arbitrary"),
        name="attention",
    )(q3, k3, v3)
    return out.reshape(tokens, -1)


def _gla_out_kernel(lq_ref, lk_ref, lv_ref, lg_ref, lr_ref, oa_ref, h_ref, gn_ref, wo_ref,
                    o_ref, state_ref, y_ref):
    @pl.when(pl.program_id(1) == 0)
    def _():
        state_ref[...] = jnp.zeros_like(state_ref)

    tt, width = lg_ref.shape
    chunk = GLA_CHUNK
    pos = lax.broadcasted_iota(jnp.int32, (tt, width), 0) & (chunk - 1)
    b = lg_ref[...]
    d = 1
    while d < chunk:
        b = b + jnp.where(pos >= d, pltpu.roll(b, d, axis=0), 0.0)
        d *= 2
    lk = lk_ref[...]
    parity = lambda rows: (lax.broadcasted_iota(jnp.int32, (rows, width), 1) // GLA_DK) % 2
    q_e = lq_ref[...] * (GLA_DK ** -0.5) * jnp.exp(b)
    q_par = [jnp.where(parity(tt) == par, q_e, 0.0).astype(MXU_DTYPE) for par in range(2)]
    k_e = (lk * jnp.exp(-b)).astype(MXU_DTYPE)
    tri = (lax.broadcasted_iota(jnp.int32, (chunk, chunk), 1)
           <= lax.broadcasted_iota(jnp.int32, (chunk, chunk), 0))
    gn = gn_ref[...]
    for c in range(tt // chunk):
        rows = slice(c * chunk, (c + 1) * chunk)
        b_c = b[rows]
        b_last = b_c[chunk - 1:chunk]
        k_end = lk[rows] * jnp.exp(b_last - b_c)
        k_end_par = [jnp.where(parity(chunk) == par, k_end, 0.0).astype(MXU_DTYPE)
                     for par in range(2)]
        decay = jnp.exp(b_last)
        for hd in range(GLA_HEADS):
            sl = slice((hd // 2) * LANES, (hd // 2 + 1) * LANES)
            vs = slice(hd * GLA_DV, (hd + 1) * GLA_DV)
            v_c = lv_ref[rows, vs]
            q_h = q_par[hd % 2][rows, sl]
            att = jnp.where(tri, _dot_nt(q_h, k_e[rows, sl]), 0.0)
            st = state_ref[hd]
            o = _dot(att.astype(MXU_DTYPE), v_c) + _dot_nt(q_h, st.astype(MXU_DTYPE))
            state_ref[hd] = st * decay[:, sl] + _dot_tn(v_c, k_end_par[hd % 2][:, sl])
            o = _rms(o, gn)
            r = lr_ref[rows, vs]
            y_ref[rows, vs] = (o * (r * _sigmoid(r))).astype(y_ref.dtype)
    n_a = oa_ref.shape[1]
    o_ref[...] = (h_ref[...] + _dot(oa_ref[...], wo_ref[:n_a, :])
                  + _dot(y_ref[...], wo_ref[n_a:, :]))


def _gla_out(lq, lk, lv, lg, lr, o_attn, h, out_norm, w_out, *, batch):
    tokens, d = h.shape
    seq = tokens // batch
    tt = GLA_ROWS
    n_s = seq // tt
    row = lambda w: pl.BlockSpec((tt, w), lambda b, i: (b * n_s + i, 0))
    ins = [lq, lk, lv, lg, lr, o_attn, h]
    return pl.pallas_call(
        _gla_out_kernel,
        grid=(batch, n_s),
        in_specs=[row(a.shape[1]) for a in ins] + [_resident((1, GLA_DV)),
                                                   _resident(w_out.shape)],
        out_specs=row(d),
        out_shape=jax.ShapeDtypeStruct((tokens, d), F32),
        scratch_shapes=[pltpu.VMEM((GLA_HEADS, GLA_DV, HEAD_PAD), F32),
                        pltpu.VMEM((tt, GLA_HEADS * GLA_DV), MXU_DTYPE)],
        compiler_params=_params("parallel", "arbitrary"),
        name="gla_out",
    )(*ins, out_norm.reshape(1, -1), w_out.astype(MXU_DTYPE))


def _interleave(src_ref, slab_ref):
    batch, steps, d = src_ref.shape
    for s in range(d // LANES):
        for b in range(batch):
            slab_ref[s, pl.ds(b, steps, stride=batch), :] = src_ref[b, :, s * LANES:(s + 1) * LANES]
    return jnp.concatenate([slab_ref[s] for s in range(d // LANES)], axis=1)


def _deinterleave(x, slab_ref, dst_ref):
    batch, steps, d = dst_ref.shape
    for s in range(d // LANES):
        slab_ref[s] = x[:, s * LANES:(s + 1) * LANES]
    for s in range(d // LANES):
        for b in range(batch):
            dst_ref[b, :, s * LANES:(s + 1) * LANES] = slab_ref[s, pl.ds(b, steps, stride=batch), :]

def _prev_step(cur, prev_rolled, sub):
    cur_rolled = pltpu.roll(cur, SUBLANES // 2, axis=0)
    return jnp.where(sub < SUBLANES // 2, prev_rolled, cur_rolled), cur_rolled


def _odd_kernel(h_ref, gm_ref, win_ref, cw_ref, cb_ref, wgate_ref, bgate_ref, lam_ref,
                sa_ref, bw_ref, cwt_ref, sd_ref, wglu_ref, bglu_ref, wo_ref,
                o_ref,
                halo_ref, rg_a_ref, rg_b_ref, rg_x_ref, rg_carry_ref,
                uh_ref, cx_ref, s5_carry_ref, slab_ref):
    first = pl.program_id(0) == 0

    @pl.when(first)
    def _():
        halo_ref[...] = jnp.zeros_like(halo_ref)
        rg_carry_ref[...] = jnp.zeros_like(rg_carry_ref)
        s5_carry_ref[...] = jnp.zeros_like(s5_carry_ref)
        uh_ref[...] = jnp.zeros_like(uh_ref)

    tm = h_ref.shape[0] * h_ref.shape[1]
    groups = tm // SUBLANES
    h = _interleave(h_ref, slab_ref)
    xn = _rms(h, gm_ref[...]).astype(MXU_DTYPE)
    y = _dot(xn, win_ref[...])
    x_gate, x_rg, u = (y[:, i * RG_WIDTH:(i + 1) * RG_WIDTH] for i in range(3))

    step = SUBLANES // 2

    halo_rows = halo_ref.shape[0]
    ext = jnp.concatenate([halo_ref[...], x_rg], axis=0)
    halo_ref[...] = x_rg[tm - halo_rows:, :]
    cw = cw_ref[...]
    xc = cb_ref[...] + x_rg * cw[RG_CONV - 1:RG_CONV]
    for j in range(RG_CONV - 1):
        back = (RG_CONV - 1 - j) * step
        xc = xc + ext[halo_rows - back:halo_rows - back + tm, :] * cw[j:j + 1]

    gate_pre = _dot(xc.astype(MXU_DTYPE), wgate_ref[...]) + bgate_ref[...]

    ub = u.astype(MXU_DTYPE)
    half_u = S5_WIDTH // 2
    half_x = S5_LANES // 2
    u_halo = uh_ref.shape[0]
    u_ext = jnp.concatenate([uh_ref[...], u], axis=0)
    uh_ref[...] = u[tm - u_halo:, :]
    ub_prev = u_ext[u_halo - step:u_halo - step + tm, :].astype(MXU_DTYPE)
    for hf in range(2):
        cols = slice(hf * half_u, (hf + 1) * half_u)
        cx_ref[:, 2 * hf * half_x:2 * (hf + 1) * half_x] = _dot(
            jnp.concatenate([ub[:, cols], ub_prev[:, cols]], axis=1), bw_ref[hf])

    gates = _sigmoid(gate_pre)
    r_gate, i_gate = gates[:, :RG_WIDTH], gates[:, RG_WIDTH:]
    lam = lam_ref[...]
    softplus_neg = jnp.maximum(-lam, 0.0) + jnp.log1p(jnp.exp(-jnp.abs(lam)))
    log_a = (-RG_C * softplus_neg) * r_gate
    a = jnp.exp(log_a)
    rg_a_ref[...] = a
    var = -jnp.tanh(log_a) * (a * a + 1.0)
    rg_b_ref[...] = jnp.where(var > 0.0, var * lax.rsqrt(var), 0.0) * (i_gate * xc)

    sub = lax.broadcasted_iota(jnp.int32, (SUBLANES, RG_WIDTH), 0)

    def rg_body(g, carry):
        x, ra_prev, rb_prev = carry
        rows = pl.ds(pl.multiple_of(g * SUBLANES, SUBLANES), SUBLANES)
        a = rg_a_ref[rows, :]
        bx = rg_b_ref[rows, :]
        a_prev, ra = _prev_step(a, ra_prev, sub)
        b_prev, rb = _prev_step(bx, rb_prev, sub)
        x = (a * a_prev) * x + (a * b_prev + bx)
        rg_x_ref[rows, :] = x
        return x, ra, rb

    carry = lax.fori_loop(0, groups, rg_body,
                          (rg_carry_ref[0], rg_carry_ref[1], rg_carry_ref[2]), unroll=2)
    for i in range(3):
        rg_carry_ref[i] = carry[i]
    y_c = rg_x_ref[...] * _gelu_tanh(x_gate)

    for lc in range(S5_LANES // SCAN_LANES):
        hf, within = divmod(lc * SCAN_LANES, half_x)
        coef = slice(lc * SCAN_LANES, (lc + 1) * SCAN_LANES)
        re = slice(2 * hf * half_x + within, 2 * hf * half_x + within + SCAN_LANES)
        im = slice(re.start + half_x, re.stop + half_x)
        a2_re, a2_im = sa_ref[0, :, coef], sa_ref[1, :, coef]

        def s5_body(g, carry):
            x_re, x_im = carry
            rows = pl.ds(pl.multiple_of(g * SUBLANES, SUBLANES), SUBLANES)
            n_re = a2_re * x_re - a2_im * x_im + cx_ref[rows, re]
            n_im = a2_re * x_im + a2_im * x_re + cx_ref[rows, im]
            cx_ref[rows, re] = n_re
            cx_ref[rows, im] = n_im
            return n_re, n_im

        carry = lax.fori_loop(0, groups, s5_body,
                              (s5_carry_ref[0, :, coef], s5_carry_ref[1, :, coef]), unroll=2)
        for i in range(2):
            s5_carry_ref[i, :, coef] = carry[i]

    ys = jnp.concatenate(
        [_dot(cx_ref[:, 2 * hf * half_x:2 * (hf + 1) * half_x].astype(MXU_DTYPE), cwt_ref[hf])
         for hf in range(2)], axis=1) + sd_ref[...] * u
    ys = _gelu_tanh(ys)
    y_d = ys * _sigmoid(_dot(ys.astype(MXU_DTYPE), wglu_ref[...]) + bglu_ref[...])
    out = (h + _dot(y_c.astype(MXU_DTYPE), wo_ref[:RG_WIDTH, :])
           + _dot(y_d.astype(MXU_DTYPE), wo_ref[RG_WIDTH:, :]))
    _deinterleave(out, slab_ref, o_ref)


def _block_diag(blocks):
    n, r, c = blocks.shape
    eye = jnp.eye(n, dtype=blocks.dtype)
    return (eye[:, None, :, None] * blocks[:, :, None, :]).reshape(n * r, n * c)


def _odd_mixer(h, mix_norm, w_in, conv_w, conv_b, w_a, b_a, w_i, b_i, lam,
               a_re, a_im, log_dt, b_re, b_im, c_re, c_im, d_skip, w_glu, b_glu, w_out, *,
               batch):
    assert batch == SUBLANES // 2
    tokens, d = h.shape
    seq = tokens // batch
    tm = ODD_ROWS
    steps = tm // batch

    dt = jnp.exp(log_dt.astype(F32))[:, None]
    lr, li = a_re.astype(F32), a_im.astype(F32)
    mag = jnp.exp(lr * dt)
    ab_re = mag * jnp.cos(li * dt)
    ab_im = mag * jnp.sin(li * dt)
    den = lr * lr + li * li
    nr, ni = ab_re - 1.0, ab_im
    coef_re = (nr * lr + ni * li) / den
    coef_im = (ni * lr - nr * li) / den
    bb_re = coef_re[..., None] * b_re - coef_im[..., None] * b_im
    bb_im = coef_re[..., None] * b_im + coef_im[..., None] * b_re
    abb_re = ab_re[..., None] * bb_re - ab_im[..., None] * bb_im
    abb_im = ab_re[..., None] * bb_im + ab_im[..., None] * bb_re
    halves = [slice(0, S5_GROUPS // 2), slice(S5_GROUPS // 2, S5_GROUPS)]

    def in_map(m_re, m_im, g):
        return jnp.concatenate([_block_diag(m_re[g].transpose(0, 2, 1)),
                                _block_diag(m_im[g].transpose(0, 2, 1))], axis=1)

    bw = jnp.stack([jnp.concatenate([in_map(bb_re, bb_im, g), in_map(abb_re, abb_im, g)],
                                    axis=0) for g in halves])
    cwt = jnp.stack([jnp.concatenate([_block_diag(c_re[g].transpose(0, 2, 1)),
                                      -_block_diag(c_im[g].transpose(0, 2, 1))], axis=0)
                     for g in halves])
    a2_re = ab_re * ab_re - ab_im * ab_im
    a2_im = 2.0 * ab_re * ab_im
    sa = jnp.stack([jnp.broadcast_to(v.reshape(1, -1), (SUBLANES, S5_LANES))
                    for v in (a2_re, a2_im)])
    wgate = jnp.concatenate([_block_diag(w_a), _block_diag(w_i)], axis=1)
    bgate = jnp.concatenate([b_a, b_i]).reshape(1, -1)

    cast = lambda w: w.astype(MXU_DTYPE)
    weights = [mix_norm.reshape(1, d), cast(w_in), conv_w, conv_b.reshape(1, -1), cast(wgate),
               bgate, lam.reshape(1, -1), sa, cast(bw), cast(cwt), d_skip.reshape(1, -1),
               cast(w_glu), b_glu.reshape(1, -1), cast(w_out)]
    row = pl.BlockSpec((batch, steps, d), lambda r: (0, r, 0))
    out = pl.pallas_call(
        _odd_kernel,
        grid=(seq // steps,),
        in_specs=[row] + [_resident(w.shape) for w in weights],
        out_specs=row,
        out_shape=jax.ShapeDtypeStruct((batch, seq, d), F32),
        scratch_shapes=[
            pltpu.VMEM((2 * SUBLANES, RG_WIDTH), F32),
            pltpu.VMEM((tm, RG_WIDTH), F32),
            pltpu.VMEM((tm, RG_WIDTH), F32),
            pltpu.VMEM((tm, RG_WIDTH), F32),
            pltpu.VMEM((3, SUBLANES, RG_WIDTH), F32),
            pltpu.VMEM((SUBLANES, S5_WIDTH), F32),
            pltpu.VMEM((tm, 2 * S5_LANES), F32),
            pltpu.VMEM((2, SUBLANES, S5_LANES), F32),
            pltpu.VMEM((d // LANES, tm, LANES), F32),
        ],
        compiler_params=_params("arbitrary"),
        name="odd_mixer",
    )(h.reshape(batch, seq, d), *weights)
    return out.reshape(tokens, d)


def kernel(x, p, ffn_a_norm, ffn_a_w1, ffn_a_w3, ffn_a_w2, mix_norm, ffn_b_norm, ffn_b_w1, ffn_b_w3, ffn_b_w2, ple_norm, ple_w_gate, ple_w_up, ev_w_in, mla_q_norm, mla_w_q_up, mla_kv_norm, mla_w_kv_up, gla_w_gate_up, gla_b_gate, gla_out_norm, ev_w_out, od_w_in, rg_conv_w, rg_conv_b, rg_w_a, rg_b_a, rg_w_i, rg_b_i, rg_lambda, s5_a_re, s5_a_im, s5_log_dt, s5_b_re, s5_b_im, s5_c_re, s5_c_im, s5_d, s5_w_glu, s5_b_glu, od_w_out, final_norm):
    batch, seq, d = x.shape
    depth = p.shape[0]
    h = x.reshape(batch * seq, d)
    cast = lambda w: w.astype(MXU_DTYPE)
    ffn_a = (ffn_a_norm, cast(ffn_a_w1), cast(ffn_a_w3), cast(ffn_a_w2))
    ffn_b = (ffn_b_norm, cast(ffn_b_w1), cast(ffn_b_w3), cast(ffn_b_w2))
    ple = (p, ple_norm, cast(ple_w_gate), cast(ple_w_up))
    for i in range(depth):
        j = i // 2
        h = _ffn(h, i, *ffn_a)
        if i % 2 == 0:
            q, k, v, lq, lk, lv, lg, lr = _even_proj(
                h, mix_norm[i], ev_w_in[j], mla_q_norm[j], mla_w_q_up[j], mla_kv_norm[j],
                mla_w_kv_up[j], gla_w_gate_up[j], gla_b_gate[j], batch=batch)
            o_attn = _attention(q, k, v, batch=batch)
            h = _gla_out(lq, lk, lv, lg, lr, o_attn, h, gla_out_norm[j], ev_w_out[j],
                         batch=batch)
        else:
            h = _odd_mixer(h, mix_norm[i], od_w_in[j], rg_conv_w[j], rg_conv_b[j], rg_w_a[j],
                           rg_b_a[j], rg_w_i[j], rg_b_i[j], rg_lambda[j], s5_a_re[j],
                           s5_a_im[j], s5_log_dt[j], s5_b_re[j], s5_b_im[j], s5_c_re[j],
                           s5_c_im[j], s5_d[j], s5_w_glu[j], s5_b_glu[j], od_w_out[j],
                           batch=batch)
        last = i == depth - 1
        h = _ffn(h, i, *ffn_b, ple=ple, final_norm=final_norm if last else None)
    return h.reshape(batch, seq, d)
```

```python
import functools
import math

import jax
import jax.numpy as jnp
from jax import lax
from jax.experimental import pallas as pl
from jax.experimental.pallas import tpu as pltpu

MXU_DTYPE = jnp.bfloat16
F32 = jnp.float32

EPS = 1e-6
PLE_DIM = 256
MLA_HEADS = 8
MLA_NOPE = 64
MLA_ROPE = 32
MLA_V = 64
MLA_Q_RANK = 384
MLA_KV_RANK = 256
ROPE_BASE = 10000.0
GLA_HEADS = 4
GLA_DK = 64
GLA_DV = 128
GLA_GATE_RANK = 16
GLA_TAU = 16.0
GLA_CHUNK = 64
RG_WIDTH = 512
RG_BLOCKS = 8
RG_CONV = 4
RG_C = 8.0
S5_GROUP = 16
S5_GROUPS = 32
S5_STATE = 64
S5_WIDTH = S5_GROUP * S5_GROUPS
S5_LANES = S5_GROUPS * S5_STATE

LANES = 128
SUBLANES = 8
MXU_TILE = 256
HEAD_PAD = LANES
VMEM_LIMIT = 56 * 1024 * 1024

FFN_ROWS = 512
FFN_CHUNKS = 2
PROJ_ROWS = 512
ATTN_Q = 512
GLA_ROWS = 512
ODD_ROWS = 512
SCAN_LANES = 512


def _rms(x, g):
    return x * lax.rsqrt(jnp.mean(x * x, axis=-1, keepdims=True) + EPS) * g


def _dot(a, b):
    return jnp.dot(a, b, preferred_element_type=F32)


def _dot_nt(a, b):
    return lax.dot_general(a, b, (((1,), (1,)), ((), ())), preferred_element_type=F32)


def _dot_tn(a, b):
    return lax.dot_general(a, b, (((0,), (0,)), ((), ())), preferred_element_type=F32)


def _sigmoid(x):
    return 1.0 / (1.0 + jnp.exp(-x))


def _gelu_tanh(x):
    return 0.5 * x * (1.0 + jnp.tanh(math.sqrt(2.0 / math.pi) * (x + 0.044715 * (x * x * x))))


def _log_sigmoid(x):
    return jnp.minimum(x, 0.0) - jnp.log1p(jnp.exp(-jnp.abs(x)))


def _resident(shape):
    return pl.BlockSpec(shape, lambda *_: (0,) * len(shape), pipeline_mode=pl.Buffered(1))


def _params(*semantics):
    return pltpu.CompilerParams(dimension_semantics=semantics, vmem_limit_bytes=VMEM_LIMIT)


def _ffn_kernel(*refs, ple, final):
    h_ref, g_ref, w1_ref, w3_ref, w2_ref = refs[:5]
    rest = list(refs[5:])
    o_ref = rest.pop()
    h = h_ref[...]
    xn = _rms(h, g_ref[...]).astype(MXU_DTYPE)
    tiles = w1_ref.shape[1] // MXU_TILE
    bounds = [MXU_TILE * ((tiles * c + FFN_CHUNKS - 1) // FFN_CHUNKS) for c in range(FFN_CHUNKS + 1)]
    y = None
    for lo, hi in zip(bounds[:-1], bounds[1:]):
        a = _dot(xn, w1_ref[:, lo:hi])
        b = _dot(xn, w3_ref[:, lo:hi])
        t = (a * _sigmoid(a) * b).astype(MXU_DTYPE)
        yc = _dot(t, w2_ref[lo:hi, :])
        y = yc if y is None else y + yc
    h = h + 0.5 * y
    if ple:
        p_ref, gp_ref, wg_ref, wu_ref = rest[:4]
        rest = rest[4:]
        gate = _sigmoid(_dot(_rms(h, gp_ref[...]).astype(MXU_DTYPE), wg_ref[...]))
        up = _dot(p_ref[...].astype(MXU_DTYPE), wu_ref[...])
        h = h + up * gate
    if final:
        h = _rms(h, rest[0][...])
    o_ref[...] = h


def _ffn(h, layer, norm, w1, w3, w2, *, ple=None, final_norm=None):
    tokens, d = h.shape
    tm = FFN_ROWS
    hspec = pl.BlockSpec((tm, d), lambda r: (r, 0))

    def of_layer(a):
        rest = a.shape[1:]
        return pl.BlockSpec((None,) + rest, lambda r: (layer,) + (0,) * len(rest),
                            pipeline_mode=pl.Buffered(1))

    gains = lambda g: g.reshape(g.shape[0], 1, d)
    args = [h, gains(norm), w1, w3, w2]
    specs = [hspec] + [of_layer(a) for a in args[1:]]
    if ple is not None:
        p, gp, wg, wu = ple
        p = p.reshape(p.shape[0], tokens, p.shape[-1])
        args += [p, gains(gp), wg, wu]
        specs += [pl.BlockSpec((None, tm, p.shape[-1]), lambda r: (layer, r, 0)),
                  of_layer(args[-3]), of_layer(wg), of_layer(wu)]
    if final_norm is not None:
        args.append(final_norm.reshape(1, d))
        specs.append(_resident((1, d)))
    return pl.pallas_call(
        functools.partial(_ffn_kernel, ple=ple is not None, final=final_norm is not None),
        grid=(tokens // tm,),
        in_specs=specs,
        out_specs=hspec,
        out_shape=jax.ShapeDtypeStruct((tokens, d), F32),
        compiler_params=_params("parallel"),
        name="ffn",
    )(*args)


_EV_COLS = (MLA_Q_RANK, MLA_KV_RANK, HEAD_PAD, GLA_HEADS * GLA_DK, GLA_HEADS * GLA_DK,
            GLA_HEADS * GLA_DV, GLA_HEADS * GLA_DV)
ROT_SHIFT = HEAD_PAD - MLA_ROPE


def _even_proj_kernel(h_ref, gm_ref, win_ref, gq_ref, wq_ref, gkv_ref, wk_ref, wv_ref,
                      wg_ref, bg_ref, cq_ref, sq_ref, ck_ref, sk_ref,
                      q_ref, k_ref, v_ref, lq_ref, lk_ref, lv_ref, lg_ref, lr_ref):
    xn = _rms(h_ref[...], gm_ref[...]).astype(MXU_DTYPE)
    y = _dot(xn, win_ref[...])
    offs = [0]
    for w in _EV_COLS:
        offs.append(offs[-1] + w)
    c_q, c_kv, kr, lq, lk, lv, lr = (
        y[:, offs[i]:offs[i + 1]] for i in range(len(_EV_COLS)))
    g_low = kr

    cqn = _rms(c_q, gq_ref[...]).astype(MXU_DTYPE)
    qf = _dot(cqn, wq_ref[...])
    ckn = _rms(c_kv, gkv_ref[...]).astype(MXU_DTYPE)
    kn = _dot(ckn, wk_ref[...])
    v_ref[...] = _dot(ckn, wv_ref[...]).astype(v_ref.dtype)
    k_rope = kr * ck_ref[...] + pltpu.roll(kr, ROT_SHIFT, axis=1) * sk_ref[...]
    cq, sq = cq_ref[...], sq_ref[...]
    for hd in range(MLA_HEADS):
        sl = slice(hd * HEAD_PAD, (hd + 1) * HEAD_PAD)
        q_h = qf[:, sl]
        q_ref[:, sl] = (q_h * cq + pltpu.roll(q_h, ROT_SHIFT, axis=1) * sq).astype(q_ref.dtype)
        k_ref[:, sl] = (kn[:, sl] + k_rope).astype(k_ref.dtype)

    gate = _dot(g_low.astype(MXU_DTYPE), wg_ref[...]) + bg_ref[...]
    lg_ref[...] = _log_sigmoid(gate) * (1.0 / GLA_TAU)
    lq_ref[...] = lq
    lk_ref[...] = lk
    lv_ref[...] = lv.astype(lv_ref.dtype)
    lr_ref[...] = lr


def _pad_heads(w, heads, width):
    lead = w.shape[:-1]
    w = w.reshape(lead + (heads, width))
    w = jnp.pad(w, [(0, 0)] * len(lead) + [(0, 0), (0, HEAD_PAD - width)])
    return w.reshape(lead + (heads * HEAD_PAD,))


def _rot_half(w):
    half = MLA_ROPE // 2
    return jnp.concatenate([-w[..., half:], w[..., :half]], axis=-1)


def _head_slot(w_nope, w_rope):
    return jnp.concatenate([w_nope, w_rope, _rot_half(w_rope)], axis=-1)


def _even_proj(h, mix_norm, w_in, q_norm, w_q_up, kv_norm, w_kv_up, w_gate_up, b_gate, *,
               batch):
    tokens, d = h.shape
    seq = tokens // batch
    tm = PROJ_ROWS
    n_s = seq // tm
    splits = [MLA_Q_RANK, MLA_KV_RANK, MLA_ROPE, GLA_HEADS * GLA_DK, GLA_HEADS * GLA_DK,
              GLA_HEADS * GLA_DV, GLA_GATE_RANK, GLA_HEADS * GLA_DV]
    offs = [0]
    for s in splits:
        offs.append(offs[-1] + s)
    wc_q, wc_kv, w_kr, w_lq, w_lk, w_lv, w_gl, w_lr = (
        w_in[:, offs[i]:offs[i + 1]] for i in range(len(splits)))
    gl_lanes = jnp.pad(w_gl, [(0, 0), (0, MLA_NOPE - GLA_GATE_RANK)])
    win = jnp.concatenate([
        wc_q, wc_kv, _head_slot(gl_lanes, w_kr), w_lq, w_lk, w_lv, w_lr], axis=1)

    wq = w_q_up.reshape(MLA_Q_RANK, MLA_HEADS, MLA_NOPE + MLA_ROPE)
    wq_full = _head_slot(wq[..., :MLA_NOPE], wq[..., MLA_NOPE:]).reshape(MLA_Q_RANK, -1)
    wkv = w_kv_up.reshape(MLA_KV_RANK, MLA_HEADS, MLA_NOPE + MLA_V)
    wk = _pad_heads(wkv[..., :MLA_NOPE].reshape(MLA_KV_RANK, -1), MLA_HEADS, MLA_NOPE)
    wv = wkv[..., MLA_NOPE:].reshape(MLA_KV_RANK, MLA_HEADS * MLA_V)
    wg = jnp.pad(w_gate_up, [(0, HEAD_PAD - GLA_GATE_RANK), (0, 0)])
    bg = b_gate.reshape(1, -1)

    half = MLA_ROPE // 2
    inv = ROPE_BASE ** (-jnp.arange(half, dtype=F32) / half)
    ang = jnp.arange(seq).astype(F32)[:, None] * inv[None, :]
    cos2 = jnp.tile(jnp.cos(ang), (1, 2))
    sin2 = jnp.tile(jnp.sin(ang), (1, 2))
    scale = (MLA_NOPE + MLA_ROPE) ** -0.5 * math.log2(math.e)
    ones = jnp.ones((seq, MLA_NOPE), F32)
    tail = jnp.zeros((seq, HEAD_PAD - MLA_NOPE - MLA_ROPE), F32)
    cq = jnp.concatenate([ones, cos2, tail], axis=1) * scale
    sq = jnp.concatenate([0 * ones, sin2, tail], axis=1) * scale
    ck = jnp.concatenate([0 * ones, cos2, tail], axis=1)
    sk = jnp.concatenate([0 * ones, sin2, tail], axis=1)

    cast = lambda w: w.astype(MXU_DTYPE)
    row = lambda w: pl.BlockSpec((tm, w), lambda r: (r, 0))
    tab = pl.BlockSpec((tm, HEAD_PAD), lambda r: (r % n_s, 0))
    gl_w = GLA_HEADS * GLA_DK
    gv_w = GLA_HEADS * GLA_DV
    qk_w = MLA_HEADS * HEAD_PAD
    outs = [(qk_w, MXU_DTYPE), (qk_w, MXU_DTYPE), (MLA_HEADS * MLA_V, MXU_DTYPE),
            (gl_w, F32), (gl_w, F32), (gv_w, MXU_DTYPE), (gl_w, F32), (gv_w, F32)]
    weights = [mix_norm.reshape(1, d), cast(win), q_norm.reshape(1, -1), cast(wq_full),
               kv_norm.reshape(1, -1), cast(wk), cast(wv), cast(wg), bg]
    return pl.pallas_call(
        _even_proj_kernel,
        grid=(tokens // tm,),
        in_specs=[row(d)] + [_resident(w.shape) for w in weights] + [tab] * 4,
        out_specs=[row(w) for w, _ in outs],
        out_shape=[jax.ShapeDtypeStruct((tokens, w), dt) for w, dt in outs],
        compiler_params=_params("parallel"),
        name="even_proj",
    )(h, *weights, cq, sq, ck, sk)


def _attn_kernel(q_ref, k_ref, v_ref, o_ref, s_ref, m_ref, l_ref, acc_ref, *, tq):
    qi = pl.program_id(2)
    row = lax.broadcasted_iota(jnp.int32, (tq, tq), 0)
    col = lax.broadcasted_iota(jnp.int32, (tq, tq), 1)
    causal = col <= row
    sls = [slice(hd * HEAD_PAD, (hd + 1) * HEAD_PAD) for hd in range(2)]
    m_ref[...] = jnp.full(m_ref.shape, -jnp.inf, F32)
    l_ref[...] = jnp.zeros(l_ref.shape, F32)
    acc_ref[...] = jnp.zeros(acc_ref.shape, F32)

    def scores(hd, j):
        start = pl.multiple_of(j * tq, tq)
        return _dot_nt(q_ref[:, sls[hd]], k_ref[pl.ds(start, tq), sls[hd]])

    def softmax_pv(hd, s, j, masked):
        start = pl.multiple_of(j * tq, tq)
        if masked:
            s = jnp.where(causal, s, -jnp.inf)
        m = m_ref[hd]
        m_new = jnp.maximum(m, jnp.max(s, axis=1, keepdims=True))
        alpha = jnp.exp2(m - m_new)
        p = jnp.exp2(s - jnp.concatenate([m_new] * (tq // LANES), axis=1))
        l_ref[hd] = alpha * l_ref[hd] + jnp.sum(p, axis=1, keepdims=True)
        m_ref[hd] = m_new
        acc_ref[hd] = alpha * acc_ref[hd] + _dot(p.astype(MXU_DTYPE),
                                                 v_ref[pl.ds(start, tq), :])

    s_ref[...] = scores(0, 0)

    def body(j, _):
        s_b = scores(1, j)
        softmax_pv(0, s_ref[...], j, False)
        s_ref[...] = scores(0, j + 1)
        softmax_pv(1, s_b, j, False)
        return 0

    lax.fori_loop(0, qi, body, 0)
    s_b = scores(1, qi)
    softmax_pv(0, s_ref[...], qi, True)
    softmax_pv(1, s_b, qi, True)
    lane = lax.broadcasted_iota(jnp.int32, (tq, 2 * MLA_V), 1)
    o_ref[...] = jnp.where(lane < MLA_V, acc_ref[0] / l_ref[0],
                           acc_ref[1] / l_ref[1]).astype(o_ref.dtype)


def _attention(q, k, v, *, batch):
    tokens = q.shape[0]
    seq = tokens // batch
    tq = ATTN_Q
    q3 = q.reshape(batch, seq, -1)
    k3 = k.reshape(batch, seq, -1)
    v3 = v.reshape(batch, seq, -1)
    pair = 2 * HEAD_PAD
    out = pl.pallas_call(
        functools.partial(_attn_kernel, tq=tq),
        grid=(batch, MLA_HEADS // 2, seq // tq),
        in_specs=[pl.BlockSpec((None, tq, pair), lambda b, hp, i: (b, i, hp)),
                  pl.BlockSpec((None, seq, pair), lambda b, hp, i: (b, 0, hp)),
                  pl.BlockSpec((None, seq, 2 * MLA_V), lambda b, hp, i: (b, 0, hp))],
        out_specs=pl.BlockSpec((None, tq, 2 * MLA_V), lambda b, hp, i: (b, i, hp)),
        out_shape=jax.ShapeDtypeStruct(v3.shape, MXU_DTYPE),
        scratch_shapes=[pltpu.VMEM((tq, tq), F32), pltpu.VMEM((2, tq, LANES), F32),
                        pltpu.VMEM((2, tq, LANES), F32), pltpu.VMEM((2, tq, 2 * MLA_V), F32)],
        compiler_params=_params("parallel", "parallel", "arbitrary"),
        name="attention",
    )(q3, k3, v3)
    return out.reshape(tokens, -1)


def _gla_out_kernel(lq_ref, lk_ref, lv_ref, lg_ref, lr_ref, oa_ref, h_ref, gn_ref, wo_ref,
                    o_ref, state_ref, y_ref):
    @pl.when(pl.program_id(1) == 0)
    def _():
        state_ref[...] = jnp.zeros_like(state_ref)

    tt, width = lg_ref.shape
    chunk = GLA_CHUNK
    pos = lax.broadcasted_iota(jnp.int32, (tt, width), 0) & (chunk - 1)
    b = lg_ref[...]
    d = 1
    while d < chunk:
        b = b + jnp.where(pos >= d, pltpu.roll(b, d, axis=0), 0.0)
        d *= 2
    lk = lk_ref[...]
    parity = lambda rows: (lax.broadcasted_iota(jnp.int32, (rows, width), 1) // GLA_DK) % 2
    q_e = lq_ref[...] * (GLA_DK ** -0.5) * jnp.exp(b)
    q_par = [jnp.where(parity(tt) == par, q_e, 0.0).astype(MXU_DTYPE) for par in range(2)]
    k_e = (lk * jnp.exp(-b)).astype(MXU_DTYPE)
    tri = (lax.broadcasted_iota(jnp.int32, (chunk, chunk), 1)
           <= lax.broadcasted_iota(jnp.int32, (chunk, chunk), 0))
    gn = gn_ref[...]
    for c in range(tt // chunk):
        rows = slice(c * chunk, (c + 1) * chunk)
        b_c = b[rows]
        b_last = b_c[chunk - 1:chunk]
        k_end = lk[rows] * jnp.exp(b_last - b_c)
        k_end_par = [jnp.where(parity(chunk) == par, k_end, 0.0).astype(MXU_DTYPE)
                     for par in range(2)]
        decay = jnp.exp(b_last)
        for hd in range(GLA_HEADS):
            sl = slice((hd // 2) * LANES, (hd // 2 + 1) * LANES)
            vs = slice(hd * GLA_DV, (hd + 1) * GLA_DV)
            v_c = lv_ref[rows, vs]
            q_h = q_par[hd % 2][rows, sl]
            att = jnp.where(tri, _dot_nt(q_h, k_e[rows, sl]), 0.0)
            st = state_ref[hd]
            o = _dot(att.astype(MXU_DTYPE), v_c) + _dot_nt(q_h, st.astype(MXU_DTYPE))
            state_ref[hd] = st * decay[:, sl] + _dot_tn(v_c, k_end_par[hd % 2][:, sl])
            o = _rms(o, gn)
            r = lr_ref[rows, vs]
            y_ref[rows, vs] = (o * (r * _sigmoid(r))).astype(y_ref.dtype)
    n_a = oa_ref.shape[1]
    o_ref[...] = (h_ref[...] + _dot(oa_ref[...], wo_ref[:n_a, :])
                  + _dot(y_ref[...], wo_ref[n_a:, :]))


def _gla_out(lq, lk, lv, lg, lr, o_attn, h, out_norm, w_out, *, batch):
    tokens, d = h.shape
    seq = tokens // batch
    tt = GLA_ROWS
    n_s = seq // tt
    row = lambda w: pl.BlockSpec((tt, w), lambda b, i: (b * n_s + i, 0))
    ins = [lq, lk, lv, lg, lr, o_attn, h]
    return pl.pallas_call(
        _gla_out_kernel,
        grid=(batch, n_s),
        in_specs=[row(a.shape[1]) for a in ins] + [_resident((1, GLA_DV)),
                                                   _resident(w_out.shape)],
        out_specs=row(d),
        out_shape=jax.ShapeDtypeStruct((tokens, d), F32),
        scratch_shapes=[pltpu.VMEM((GLA_HEADS, GLA_DV, HEAD_PAD), F32),
                        pltpu.VMEM((tt, GLA_HEADS * GLA_DV), MXU_DTYPE)],
        compiler_params=_params("parallel", "arbitrary"),
        name="gla_out",
    )(*ins, out_norm.reshape(1, -1), w_out.astype(MXU_DTYPE))


def _interleave(src_ref, slab_ref):
    batch, steps, d = src_ref.shape
    for s in range(d // LANES):
        for b in range(batch):
            slab_ref[s, pl.ds(b, steps, stride=batch), :] = src_ref[b, :, s * LANES:(s + 1) * LANES]
    return jnp.concatenate([slab_ref[s] for s in range(d // LANES)], axis=1)


def _deinterleave(x, slab_ref, dst_ref):
    batch, steps, d = dst_ref.shape
    for s in range(d // LANES):
        slab_ref[s] = x[:, s * LANES:(s + 1) * LANES]
    for s in range(d // LANES):
        for b in range(batch):
            dst_ref[b, :, s * LANES:(s + 1) * LANES] = slab_ref[s, pl.ds(b, steps, stride=batch), :]

def _prev_step(cur, prev_rolled, sub):
    cur_rolled = pltpu.roll(cur, SUBLANES // 2, axis=0)
    return jnp.where(sub < SUBLANES // 2, prev_rolled, cur_rolled), cur_rolled


def _odd_kernel(h_ref, gm_ref, win_ref, cw_ref, cb_ref, wgate_ref, bgate_ref, lam_ref,
                sa_ref, bw_ref, cwt_ref, sd_ref, wglu_ref, bglu_ref, wo_ref,
                o_ref,
                halo_ref, rg_a_ref, rg_b_ref, rg_x_ref, rg_carry_ref,
                uh_ref, cx_ref, s5_carry_ref, slab_ref):
    first = pl.program_id(0) == 0

    @pl.when(first)
    def _():
        halo_ref[...] = jnp.zeros_like(halo_ref)
        rg_carry_ref[...] = jnp.zeros_like(rg_carry_ref)
        s5_carry_ref[...] = jnp.zeros_like(s5_carry_ref)
        uh_ref[...] = jnp.zeros_like(uh_ref)

    tm = h_ref.shape[0] * h_ref.shape[1]
    groups = tm // SUBLANES
    h = _interleave(h_ref, slab_ref)
    xn = _rms(h, gm_ref[...]).astype(MXU_DTYPE)
    y = _dot(xn, win_ref[...])
    x_gate, x_rg, u = (y[:, i * RG_WIDTH:(i + 1) * RG_WIDTH] for i in range(3))

    step = SUBLANES // 2

    halo_rows = halo_ref.shape[0]
    ext = jnp.concatenate([halo_ref[...], x_rg], axis=0)
    halo_ref[...] = x_rg[tm - halo_rows:, :]
    cw = cw_ref[...]
    xc = cb_ref[...] + x_rg * cw[RG_CONV - 1:RG_CONV]
    for j in range(RG_CONV - 1):
        back = (RG_CONV - 1 - j) * step
        xc = xc + ext[halo_rows - back:halo_rows - back + tm, :] * cw[j:j + 1]

    gate_pre = _dot(xc.astype(MXU_DTYPE), wgate_ref[...]) + bgate_ref[...]

    ub = u.astype(MXU_DTYPE)
    half_u = S5_WIDTH // 2
    half_x = S5_LANES // 2
    u_halo = uh_ref.shape[0]
    u_ext = jnp.concatenate([uh_ref[...], u], axis=0)
    uh_ref[...] = u[tm - u_halo:, :]
    ub_prev = u_ext[u_halo - step:u_halo - step + tm, :].astype(MXU_DTYPE)
    for hf in range(2):
        cols = slice(hf * half_u, (hf + 1) * half_u)
        cx_ref[:, 2 * hf * half_x:2 * (hf + 1) * half_x] = _dot(
            jnp.concatenate([ub[:, cols], ub_prev[:, cols]], axis=1), bw_ref[hf])

    gates = _sigmoid(gate_pre)
    r_gate, i_gate = gates[:, :RG_WIDTH], gates[:, RG_WIDTH:]
    lam = lam_ref[...]
    softplus_neg = jnp.maximum(-lam, 0.0) + jnp.log1p(jnp.exp(-jnp.abs(lam)))
    log_a = (-RG_C * softplus_neg) * r_gate
    a = jnp.exp(log_a)
    rg_a_ref[...] = a
    var = -jnp.tanh(log_a) * (a * a + 1.0)
    rg_b_ref[...] = jnp.where(var > 0.0, var * lax.rsqrt(var), 0.0) * (i_gate * xc)

    sub = lax.broadcasted_iota(jnp.int32, (SUBLANES, RG_WIDTH), 0)

    def rg_body(g, carry):
        x, ra_prev, rb_prev = carry
        rows = pl.ds(pl.multiple_of(g * SUBLANES, SUBLANES), SUBLANES)
        a = rg_a_ref[rows, :]
        bx = rg_b_ref[rows, :]
        a_prev, ra = _prev_step(a, ra_prev, sub)
        b_prev, rb = _prev_step(bx, rb_prev, sub)
        x = (a * a_prev) * x + (a * b_prev + bx)
        rg_x_ref[rows, :] = x
        return x, ra, rb

    carry = lax.fori_loop(0, groups, rg_body,
                          (rg_carry_ref[0], rg_carry_ref[1], rg_carry_ref[2]), unroll=2)
    for i in range(3):
        rg_carry_ref[i] = carry[i]
    y_c = rg_x_ref[...] * _gelu_tanh(x_gate)

    for lc in range(S5_LANES // SCAN_LANES):
        hf, within = divmod(lc * SCAN_LANES, half_x)
        coef = slice(lc * SCAN_LANES, (lc + 1) * SCAN_LANES)
        re = slice(2 * hf * half_x + within, 2 * hf * half_x + within + SCAN_LANES)
        im = slice(re.start + half_x, re.stop + half_x)
        a2_re, a2_im = sa_ref[0, :, coef], sa_ref[1, :, coef]

        def s5_body(g, carry):
            x_re, x_im = carry
            rows = pl.ds(pl.multiple_of(g * SUBLANES, SUBLANES), SUBLANES)
            n_re = a2_re * x_re - a2_im * x_im + cx_ref[rows, re]
            n_im = a2_re * x_im + a2_im * x_re + cx_ref[rows, im]
            cx_ref[rows, re] = n_re
            cx_ref[rows, im] = n_im
            return n_re, n_im

        carry = lax.fori_loop(0, groups, s5_body,
                              (s5_carry_ref[0, :, coef], s5_carry_ref[1, :, coef]), unroll=2)
        for i in range(2):
            s5_carry_ref[i, :, coef] = carry[i]

    ys = jnp.concatenate(
        [_dot(cx_ref[:, 2 * hf * half_x:2 * (hf + 1) * half_x].astype(MXU_DTYPE), cwt_ref[hf])
         for hf in range(2)], axis=1) + sd_ref[...] * u
    ys = _gelu_tanh(ys)
    y_d = ys * _sigmoid(_dot(ys.astype(MXU_DTYPE), wglu_ref[...]) + bglu_ref[...])
    out = (h + _dot(y_c.astype(MXU_DTYPE), wo_ref[:RG_WIDTH, :])
           + _dot(y_d.astype(MXU_DTYPE), wo_ref[RG_WIDTH:, :]))
    _deinterleave(out, slab_ref, o_ref)


def _block_diag(blocks):
    n, r, c = blocks.shape
    eye = jnp.eye(n, dtype=blocks.dtype)
    return (eye[:, None, :, None] * blocks[:, :, None, :]).reshape(n * r, n * c)


def _odd_mixer(h, mix_norm, w_in, conv_w, conv_b, w_a, b_a, w_i, b_i, lam,
               a_re, a_im, log_dt, b_re, b_im, c_re, c_im, d_skip, w_glu, b_glu, w_out, *,
               batch):
    assert batch == SUBLANES // 2
    tokens, d = h.shape
    seq = tokens // batch
    tm = ODD_ROWS
    steps = tm // batch

    dt = jnp.exp(log_dt.astype(F32))[:, None]
    lr, li = a_re.astype(F32), a_im.astype(F32)
    mag = jnp.exp(lr * dt)
    ab_re = mag * jnp.cos(li * dt)
    ab_im = mag * jnp.sin(li * dt)
    den = lr * lr + li * li
    nr, ni = ab_re - 1.0, ab_im
    coef_re = (nr * lr + ni * li) / den
    coef_im = (ni * lr - nr * li) / den
    bb_re = coef_re[..., None] * b_re - coef_im[..., None] * b_im
    bb_im = coef_re[..., None] * b_im + coef_im[..., None] * b_re
    abb_re = ab_re[..., None] * bb_re - ab_im[..., None] * bb_im
    abb_im = ab_re[..., None] * bb_im + ab_im[..., None] * bb_re
    halves = [slice(0, S5_GROUPS // 2), slice(S5_GROUPS // 2, S5_GROUPS)]

    def in_map(m_re, m_im, g):
        return jnp.concatenate([_block_diag(m_re[g].transpose(0, 2, 1)),
                                _block_diag(m_im[g].transpose(0, 2, 1))], axis=1)

    bw = jnp.stack([jnp.concatenate([in_map(bb_re, bb_im, g), in_map(abb_re, abb_im, g)],
                                    axis=0) for g in halves])
    cwt = jnp.stack([jnp.concatenate([_block_diag(c_re[g].transpose(0, 2, 1)),
                                      -_block_diag(c_im[g].transpose(0, 2, 1))], axis=0)
                     for g in halves])
    a2_re = ab_re * ab_re - ab_im * ab_im
    a2_im = 2.0 * ab_re * ab_im
    sa = jnp.stack([jnp.broadcast_to(v.reshape(1, -1), (SUBLANES, S5_LANES))
                    for v in (a2_re, a2_im)])
    wgate = jnp.concatenate([_block_diag(w_a), _block_diag(w_i)], axis=1)
    bgate = jnp.concatenate([b_a, b_i]).reshape(1, -1)

    cast = lambda w: w.astype(MXU_DTYPE)
    weights = [mix_norm.reshape(1, d), cast(w_in), conv_w, conv_b.reshape(1, -1), cast(wgate),
               bgate, lam.reshape(1, -1), sa, cast(bw), cast(cwt), d_skip.reshape(1, -1),
               cast(w_glu), b_glu.reshape(1, -1), cast(w_out)]
    row = pl.BlockSpec((batch, steps, d), lambda r: (0, r, 0))
    out = pl.pallas_call(
        _odd_kernel,
        grid=(seq // steps,),
        in_specs=[row] + [_resident(w.shape) for w in weights],
        out_specs=row,
        out_shape=jax.ShapeDtypeStruct((batch, seq, d), F32),
        scratch_shapes=[
            pltpu.VMEM((2 * SUBLANES, RG_WIDTH), F32),
            pltpu.VMEM((tm, RG_WIDTH), F32),
            pltpu.VMEM((tm, RG_WIDTH), F32),
            pltpu.VMEM((tm, RG_WIDTH), F32),
            pltpu.VMEM((3, SUBLANES, RG_WIDTH), F32),
            pltpu.VMEM((SUBLANES, S5_WIDTH), F32),
            pltpu.VMEM((tm, 2 * S5_LANES), F32),
            pltpu.VMEM((2, SUBLANES, S5_LANES), F32),
            pltpu.VMEM((d // LANES, tm, LANES), F32),
        ],
        compiler_params=_params("arbitrary"),
        name="odd_mixer",
    )(h.reshape(batch, seq, d), *weights)
    return out.reshape(tokens, d)


def kernel(x, p, ffn_a_norm, ffn_a_w1, ffn_a_w3, ffn_a_w2, mix_norm, ffn_b_norm, ffn_b_w1, ffn_b_w3, ffn_b_w2, ple_norm, ple_w_gate, ple_w_up, ev_w_in, mla_q_norm, mla_w_q_up, mla_kv_norm, mla_w_kv_up, gla_w_gate_up, gla_b_gate, gla_out_norm, ev_w_out, od_w_in, rg_conv_w, rg_conv_b, rg_w_a, rg_b_a, rg_w_i, rg_b_i, rg_lambda, s5_a_re, s5_a_im, s5_log_dt, s5_b_re, s5_b_im, s5_c_re, s5_c_im, s5_d, s5_w_glu, s5_b_glu, od_w_out, final_norm):
    batch, seq, d = x.shape
    depth = p.shape[0]
    h = x.reshape(batch * seq, d)
    cast = lambda w: w.astype(MXU_DTYPE)
    ffn_a = (ffn_a_norm, cast(ffn_a_w1), cast(ffn_a_w3), cast(ffn_a_w2))
    ffn_b = (ffn_b_norm, cast(ffn_b_w1), cast(ffn_b_w3), cast(ffn_b_w2))
    ple = (p, ple_norm, cast(ple_w_gate), cast(ple_w_up))
    for i in range(depth):
        j = i // 2
        h = _ffn(h, i, *ffn_a)
        if i % 2 == 0:
            q, k, v, lq, lk, lv, lg, lr = _even_proj(
                h, mix_norm[i], ev_w_in[j], mla_q_norm[j], mla_w_q_up[j], mla_kv_norm[j],
                mla_w_kv_up[j], gla_w_gate_up[j], gla_b_gate[j], batch=batch)
            o_attn = _attention(q, k, v, batch=batch)
            h = _gla_out(lq, lk, lv, lg, lr, o_attn, h, gla_out_norm[j], ev_w_out[j],
                         batch=batch)
        else:
            h = _odd_mixer(h, mix_norm[i], od_w_in[j], rg_conv_w[j], rg_conv_b[j], rg_w_a[j],
                           rg_b_a[j], rg_w_i[j], rg_b_i[j], rg_lambda[j], s5_a_re[j],
                           s5_a_im[j], s5_log_dt[j], s5_b_re[j], s5_b_im[j], s5_c_re[j],
                           s5_c_im[j], s5_d[j], s5_w_glu[j], s5_b_glu[j], od_w_out[j],
                           batch=batch)
        last = i == depth - 1
        h = _ffn(h, i, *ffn_b, ple=ple, final_norm=final_norm if last else None)
    return h.reshape(batch, seq, d)
```

```python
import functools
import math

import jax
import jax.numpy as jnp
from jax import lax
from jax.experimental import pallas as pl
from jax.experimental.pallas import tpu as pltpu

MXU_DTYPE = jnp.bfloat16
F32 = jnp.float32

EPS = 1e-6
PLE_DIM = 256
MLA_HEADS = 8
MLA_NOPE = 64
MLA_ROPE = 32
MLA_V = 64
MLA_Q_RANK = 384
MLA_KV_RANK = 256
ROPE_BASE = 10000.0
GLA_HEADS = 4
GLA_DK = 64
GLA_DV = 128
GLA_GATE_RANK = 16
GLA_TAU = 16.0
GLA_CHUNK = 64
RG_WIDTH = 512
RG_BLOCKS = 8
RG_CONV = 4
RG_C = 8.0
S5_GROUP = 16
S5_GROUPS = 32
S5_STATE = 64
S5_WIDTH = S5_GROUP * S5_GROUPS
S5_LANES = S5_GROUPS * S5_STATE

LANES = 128
SUBLANES = 8
MXU_TILE = 256
HEAD_PAD = LANES
VMEM_LIMIT = 56 * 1024 * 1024

FFN_ROWS = 512
FFN_CHUNKS = 2
PROJ_ROWS = 512
ATTN_Q = 512
GLA_ROWS = 512
ODD_ROWS = 512
SCAN_LANES = 512


def _rms(x, g):
    return x * lax.rsqrt(jnp.mean(x * x, axis=-1, keepdims=True) + EPS) * g


def _dot(a, b):
    return jnp.dot(a, b, preferred_element_type=F32)


def _dot_nt(a, b):
    return lax.dot_general(a, b, (((1,), (1,)), ((), ())), preferred_element_type=F32)


def _dot_tn(a, b):
    return lax.dot_general(a, b, (((0,), (0,)), ((), ())), preferred_element_type=F32)


def _sigmoid(x):
    return 1.0 / (1.0 + jnp.exp(-x))


def _gelu_tanh(x):
    return 0.5 * x * (1.0 + jnp.tanh(math.sqrt(2.0 / math.pi) * (x + 0.044715 * (x * x * x))))


def _log_sigmoid(x):
    return jnp.minimum(x, 0.0) - jnp.log1p(jnp.exp(-jnp.abs(x)))


def _resident(shape):
    return pl.BlockSpec(shape, lambda *_: (0,) * len(shape), pipeline_mode=pl.Buffered(1))


def _params(*semantics):
    return pltpu.CompilerParams(dimension_semantics=semantics, vmem_limit_bytes=VMEM_LIMIT)


def _ffn_kernel(*refs, ple, final):
    h_ref, g_ref, w1_ref, w3_ref, w2_ref = refs[:5]
    rest = list(refs[5:])
    o_ref = rest.pop()
    h = h_ref[...]
    xn = _rms(h, g_ref[...]).astype(MXU_DTYPE)
    tiles = w1_ref.shape[1] // MXU_TILE
    bounds = [MXU_TILE * ((tiles * c + FFN_CHUNKS - 1) // FFN_CHUNKS) for c in range(FFN_CHUNKS + 1)]
    y = None
    for lo, hi in zip(bounds[:-1], bounds[1:]):
        a = _dot(xn, w1_ref[:, lo:hi])
        b = _dot(xn, w3_ref[:, lo:hi])
        t = (a * _sigmoid(a) * b).astype(MXU_DTYPE)
        yc = _dot(t, w2_ref[lo:hi, :])
        y = yc if y is None else y + yc
    h = h + 0.5 * y
    if ple:
        p_ref, gp_ref, wg_ref, wu_ref = rest[:4]
        rest = rest[4:]
        gate = _sigmoid(_dot(_rms(h, gp_ref[...]).astype(MXU_DTYPE), wg_ref[...]))
        up = _dot(p_ref[...].astype(MXU_DTYPE), wu_ref[...])
        h = h + up * gate
    if final:
        h = _rms(h, rest[0][...])
    o_ref[...] = h


def _ffn(h, layer, norm, w1, w3, w2, *, ple=None, final_norm=None):
    tokens, d = h.shape
    tm = FFN_ROWS
    hspec = pl.BlockSpec((tm, d), lambda r: (r, 0))

    def of_layer(a):
        rest = a.shape[1:]
        return pl.BlockSpec((None,) + rest, lambda r: (layer,) + (0,) * len(rest),
                            pipeline_mode=pl.Buffered(1))

    gains = lambda g: g.reshape(g.shape[0], 1, d)
    args = [h, gains(norm), w1, w3, w2]
    specs = [hspec] + [of_layer(a) for a in args[1:]]
    if ple is not None:
        p, gp, wg, wu = ple
        p = p.reshape(p.shape[0], tokens, p.shape[-1])
        args += [p, gains(gp), wg, wu]
        specs += [pl.BlockSpec((None, tm, p.shape[-1]), lambda r: (layer, r, 0)),
                  of_layer(args[-3]), of_layer(wg), of_layer(wu)]
    if final_norm is not None:
        args.append(final_norm.reshape(1, d))
        specs.append(_resident((1, d)))
    return pl.pallas_call(
        functools.partial(_ffn_kernel, ple=ple is not None, final=final_norm is not None),
        grid=(tokens // tm,),
        in_specs=specs,
        out_specs=hspec,
        out_shape=jax.ShapeDtypeStruct((tokens, d), F32),
        compiler_params=_params("parallel"),
        name="ffn",
    )(*args)


_EV_COLS = (MLA_Q_RANK, MLA_KV_RANK, HEAD_PAD, GLA_HEADS * GLA_DK, GLA_HEADS * GLA_DK,
            GLA_HEADS * GLA_DV, GLA_HEADS * GLA_DV)
ROT_SHIFT = HEAD_PAD - MLA_ROPE


def _even_proj_kernel(h_ref, gm_ref, win_ref, gq_ref, wq_ref, gkv_ref, wk_ref, wv_ref,
                      wg_ref, bg_ref, cq_ref, sq_ref, ck_ref, sk_ref,
                      q_ref, k_ref, v_ref, lq_ref, lk_ref, lv_ref, lg_ref, lr_ref):
    xn = _rms(h_ref[...], gm_ref[...]).astype(MXU_DTYPE)
    y = _dot(xn, win_ref[...])
    offs = [0]
    for w in _EV_COLS:
        offs.append(offs[-1] + w)
    c_q, c_kv, kr, lq, lk, lv, lr = (
        y[:, offs[i]:offs[i + 1]] for i in range(len(_EV_COLS)))
    g_low = kr

    cqn = _rms(c_q, gq_ref[...]).astype(MXU_DTYPE)
    qf = _dot(cqn, wq_ref[...])
    ckn = _rms(c_kv, gkv_ref[...]).astype(MXU_DTYPE)
    kn = _dot(ckn, wk_ref[...])
    v_ref[...] = _dot(ckn, wv_ref[...]).astype(v_ref.dtype)
    k_rope = kr * ck_ref[...] + pltpu.roll(kr, ROT_SHIFT, axis=1) * sk_ref[...]
    cq, sq = cq_ref[...], sq_ref[...]
    for hd in range(MLA_HEADS):
        sl = slice(hd * HEAD_PAD, (hd + 1) * HEAD_PAD)
        q_h = qf[:, sl]
        q_ref[:, sl] = (q_h * cq + pltpu.roll(q_h, ROT_SHIFT, axis=1) * sq).astype(q_ref.dtype)
        k_ref[:, sl] = (kn[:, sl] + k_rope).astype(k_ref.dtype)

    gate = _dot(g_low.astype(MXU_DTYPE), wg_ref[...]) + bg_ref[...]
    lg_ref[...] = _log_sigmoid(gate) * (1.0 / GLA_TAU)
    lq_ref[...] = lq
    lk_ref[...] = lk
    lv_ref[...] = lv.astype(lv_ref.dtype)
    lr_ref[...] = lr


def _pad_heads(w, heads, width):
    lead = w.shape[:-1]
    w = w.reshape(lead + (heads, width))
    w = jnp.pad(w, [(0, 0)] * len(lead) + [(0, 0), (0, HEAD_PAD - width)])
    return w.reshape(lead + (heads * HEAD_PAD,))


def _rot_half(w):
    half = MLA_ROPE // 2
    return jnp.concatenate([-w[..., half:], w[..., :half]], axis=-1)


def _head_slot(w_nope, w_rope):
    return jnp.concatenate([w_nope, w_rope, _rot_half(w_rope)], axis=-1)


def _even_proj(h, mix_norm, w_in, q_norm, w_q_up, kv_norm, w_kv_up, w_gate_up, b_gate, *,
               batch):
    tokens, d = h.shape
    seq = tokens // batch
    tm = PROJ_ROWS
    n_s = seq // tm
    splits = [MLA_Q_RANK, MLA_KV_RANK, MLA_ROPE, GLA_HEADS * GLA_DK, GLA_HEADS * GLA_DK,
              GLA_HEADS * GLA_DV, GLA_GATE_RANK, GLA_HEADS * GLA_DV]
    offs = [0]
    for s in splits:
        offs.append(offs[-1] + s)
    wc_q, wc_kv, w_kr, w_lq, w_lk, w_lv, w_gl, w_lr = (
        w_in[:, offs[i]:offs[i + 1]] for i in range(len(splits)))
    gl_lanes = jnp.pad(w_gl, [(0, 0), (0, MLA_NOPE - GLA_GATE_RANK)])
    win = jnp.concatenate([
        wc_q, wc_kv, _head_slot(gl_lanes, w_kr), w_lq, w_lk, w_lv, w_lr], axis=1)

    wq = w_q_up.reshape(MLA_Q_RANK, MLA_HEADS, MLA_NOPE + MLA_ROPE)
    wq_full = _head_slot(wq[..., :MLA_NOPE], wq[..., MLA_NOPE:]).reshape(MLA_Q_RANK, -1)
    wkv = w_kv_up.reshape(MLA_KV_RANK, MLA_HEADS, MLA_NOPE + MLA_V)
    wk = _pad_heads(wkv[..., :MLA_NOPE].reshape(MLA_KV_RANK, -1), MLA_HEADS, MLA_NOPE)
    wv = wkv[..., MLA_NOPE:].reshape(MLA_KV_RANK, MLA_HEADS * MLA_V)
    wg = jnp.pad(w_gate_up, [(0, HEAD_PAD - GLA_GATE_RANK), (0, 0)])
    bg = b_gate.reshape(1, -1)

    half = MLA_ROPE // 2
    inv = ROPE_BASE ** (-jnp.arange(half, dtype=F32) / half)
    ang = jnp.arange(seq).astype(F32)[:, None] * inv[None, :]
    cos2 = jnp.tile(jnp.cos(ang), (1, 2))
    sin2 = jnp.tile(jnp.sin(ang), (1, 2))
    scale = (MLA_NOPE + MLA_ROPE) ** -0.5 * math.log2(math.e)
    ones = jnp.ones((seq, MLA_NOPE), F32)
    tail = jnp.zeros((seq, HEAD_PAD - MLA_NOPE - MLA_ROPE), F32)
    cq = jnp.concatenate([ones, cos2, tail], axis=1) * scale
    sq = jnp.concatenate([0 * ones, sin2, tail], axis=1) * scale
    ck = jnp.concatenate([0 * ones, cos2, tail], axis=1)
    sk = jnp.concatenate([0 * ones, sin2, tail], axis=1)

    cast = lambda w: w.astype(MXU_DTYPE)
    row = lambda w: pl.BlockSpec((tm, w), lambda r: (r, 0))
    tab = pl.BlockSpec((tm, HEAD_PAD), lambda r: (r % n_s, 0))
    gl_w = GLA_HEADS * GLA_DK
    gv_w = GLA_HEADS * GLA_DV
    qk_w = MLA_HEADS * HEAD_PAD
    outs = [(qk_w, MXU_DTYPE), (qk_w, MXU_DTYPE), (MLA_HEADS * MLA_V, MXU_DTYPE),
            (gl_w, F32), (gl_w, F32), (gv_w, MXU_DTYPE), (gl_w, F32), (gv_w, F32)]
    weights = [mix_norm.reshape(1, d), cast(win), q_norm.reshape(1, -1), cast(wq_full),
               kv_norm.reshape(1, -1), cast(wk), cast(wv), cast(wg), bg]
    return pl.pallas_call(
        _even_proj_kernel,
        grid=(tokens // tm,),
        in_specs=[row(d)] + [_resident(w.shape) for w in weights] + [tab] * 4,
        out_specs=[row(w) for w, _ in outs],
        out_shape=[jax.ShapeDtypeStruct((tokens, w), dt) for w, dt in outs],
        compiler_params=_params("parallel"),
        name="even_proj",
    )(h, *weights, cq, sq, ck, sk)


def _attn_kernel(q_ref, k_ref, v_ref, o_ref, s_ref, m_ref, l_ref, acc_ref, *, tq):
    qi = pl.program_id(2)
    row = lax.broadcasted_iota(jnp.int32, (tq, tq), 0)
    col = lax.broadcasted_iota(jnp.int32, (tq, tq), 1)
    causal = col <= row
    sls = [slice(hd * HEAD_PAD, (hd + 1) * HEAD_PAD) for hd in range(2)]
    m_ref[...] = jnp.full(m_ref.shape, -jnp.inf, F32)
    l_ref[...] = jnp.zeros(l_ref.shape, F32)
    acc_ref[...] = jnp.zeros(acc_ref.shape, F32)

    def scores(hd, j):
        start = pl.multiple_of(j * tq, tq)
        return _dot_nt(q_ref[:, sls[hd]], k_ref[pl.ds(start, tq), sls[hd]])

    def softmax_pv(hd, s, j, masked):
        start = pl.multiple_of(j * tq, tq)
        if masked:
            s = jnp.where(causal, s, -jnp.inf)
        m = m_ref[hd]
        m_new = jnp.maximum(m, jnp.max(s, axis=1, keepdims=True))
        alpha = jnp.exp2(m - m_new)
        p = jnp.exp2(s - jnp.concatenate([m_new] * (tq // LANES), axis=1))
        l_ref[hd] = alpha * l_ref[hd] + jnp.sum(p, axis=1, keepdims=True)
        m_ref[hd] = m_new
        acc_ref[hd] = alpha * acc_ref[hd] + _dot(p.astype(MXU_DTYPE),
                                                 v_ref[pl.ds(start, tq), :])

    s_ref[...] = scores(0, 0)

    def body(j, _):
        s_b = scores(1, j)
        softmax_pv(0, s_ref[...], j, False)
        s_ref[...] = scores(0, j + 1)
        softmax_pv(1, s_b, j, False)
        return 0

    lax.fori_loop(0, qi, body, 0)
    s_b = scores(1, qi)
    softmax_pv(0, s_ref[...], qi, True)
    softmax_pv(1, s_b, qi, True)
    lane = lax.broadcasted_iota(jnp.int32, (tq, 2 * MLA_V), 1)
    o_ref[...] = jnp.where(lane < MLA_V, acc_ref[0] / l_ref[0],
                           acc_ref[1] / l_ref[1]).astype(o_ref.dtype)


def _attention(q, k, v, *, batch):
    tokens = q.shape[0]
    seq = tokens // batch
    tq = ATTN_Q
    q3 = q.reshape(batch, seq, -1)
    k3 = k.reshape(batch, seq, -1)
    v3 = v.reshape(batch, seq, -1)
    pair = 2 * HEAD_PAD
    out = pl.pallas_call(
        functools.partial(_attn_kernel, tq=tq),
        grid=(batch, MLA_HEADS // 2, seq // tq),
        in_specs=[pl.BlockSpec((None, tq, pair), lambda b, hp, i: (b, i, hp)),
                  pl.BlockSpec((None, seq, pair), lambda b, hp, i: (b, 0, hp)),
                  pl.BlockSpec((None, seq, 2 * MLA_V), lambda b, hp, i: (b, 0, hp))],
        out_specs=pl.BlockSpec((None, tq, 2 * MLA_V), lambda b, hp, i: (b, i, hp)),
        out_shape=jax.ShapeDtypeStruct(v3.shape, MXU_DTYPE),
        scratch_shapes=[pltpu.VMEM((tq, tq), F32), pltpu.VMEM((2, tq, LANES), F32),
                        pltpu.VMEM((2, tq, LANES), F32), pltpu.VMEM((2, tq, 2 * MLA_V), F32)],
        compiler_params=_params("parallel", "parallel", "arbitrary"),
        name="attention",
    )(q3, k3, v3)
    return out.reshape(tokens, -1)


def _gla_out_kernel(lq_ref, lk_ref, lv_ref, lg_ref, lr_ref, oa_ref, h_ref, gn_ref, wo_ref,
                    o_ref, state_ref, y_ref):
    @pl.when(pl.program_id(1) == 0)
    def _():
        state_ref[...] = jnp.zeros_like(state_ref)

    tt, width = lg_ref.shape
    chunk = GLA_CHUNK
    pos = lax.broadcasted_iota(jnp.int32, (tt, width), 0) & (chunk - 1)
    b = lg_ref[...]
    d = 1
    while d < chunk:
        b = b + jnp.where(pos >= d, pltpu.roll(b, d, axis=0), 0.0)
        d *= 2
    lk = lk_ref[...]
    parity = lambda rows: (lax.broadcasted_iota(jnp.int32, (rows, width), 1) // GLA_DK) % 2
    q_e = lq_ref[...] * (GLA_DK ** -0.5) * jnp.exp(b)
    q_par = [jnp.where(parity(tt) == par, q_e, 0.0).astype(MXU_DTYPE) for par in range(2)]
    k_e = (lk * jnp.exp(-b)).astype(MXU_DTYPE)
    tri = (lax.broadcasted_iota(jnp.int32, (chunk, chunk), 1)
           <= lax.broadcasted_iota(jnp.int32, (chunk, chunk), 0))
    gn = gn_ref[...]
    for c in range(tt // chunk):
        rows = slice(c * chunk, (c + 1) * chunk)
        b_c = b[rows]
        b_last = b_c[chunk - 1:chunk]
        k_end = lk[rows] * jnp.exp(b_last - b_c)
        k_end_par = [jnp.where(parity(chunk) == par, k_end, 0.0).astype(MXU_DTYPE)
                     for par in range(2)]
        decay = jnp.exp(b_last)
        for hd in range(GLA_HEADS):
            sl = slice((hd // 2) * LANES, (hd // 2 + 1) * LANES)
            vs = slice(hd * GLA_DV, (hd + 1) * GLA_DV)
            v_c = lv_ref[rows, vs]
            q_h = q_par[hd % 2][rows, sl]
            att = jnp.where(tri, _dot_nt(q_h, k_e[rows, sl]), 0.0)
            st = state_ref[hd]
            o = _dot(att.astype(MXU_DTYPE), v_c) + _dot_nt(q_h, st.astype(MXU_DTYPE))
            state_ref[hd] = st * decay[:, sl] + _dot_tn(v_c, k_end_par[hd % 2][:, sl])
            o = _rms(o, gn)
            r = lr_ref[rows, vs]
            y_ref[rows, vs] = (o * (r * _sigmoid(r))).astype(y_ref.dtype)
    n_a = oa_ref.shape[1]
    o_ref[...] = (h_ref[...] + _dot(oa_ref[...], wo_ref[:n_a, :])
                  + _dot(y_ref[...], wo_ref[n_a:, :]))


def _gla_out(lq, lk, lv, lg, lr, o_attn, h, out_norm, w_out, *, batch):
    tokens, d = h.shape
    seq = tokens // batch
    tt = GLA_ROWS
    n_s = seq // tt
    row = lambda w: pl.BlockSpec((tt, w), lambda b, i: (b * n_s + i, 0))
    ins = [lq, lk, lv, lg, lr, o_attn, h]
    return pl.pallas_call(
        _gla_out_kernel,
        grid=(batch, n_s),
        in_specs=[row(a.shape[1]) for a in ins] + [_resident((1, GLA_DV)),
                                                   _resident(w_out.shape)],
        out_specs=row(d),
        out_shape=jax.ShapeDtypeStruct((tokens, d), F32),
        scratch_shapes=[pltpu.VMEM((GLA_HEADS, GLA_DV, HEAD_PAD), F32),
                        pltpu.VMEM((tt, GLA_HEADS * GLA_DV), MXU_DTYPE)],
        compiler_params=_params("parallel", "arbitrary"),
        name="gla_out",
    )(*ins, out_norm.reshape(1, -1), w_out.astype(MXU_DTYPE))


def _interleave(src_ref, slab_ref):
    batch, steps, d = src_ref.shape
    for s in range(d // LANES):
        for b in range(batch):
            slab_ref[s, pl.ds(b, steps, stride=batch), :] = src_ref[b, :, s * LANES:(s + 1) * LANES]
    return jnp.concatenate([slab_ref[s] for s in range(d // LANES)], axis=1)


def _deinterleave(x, slab_ref, dst_ref):
    batch, steps, d = dst_ref.shape
    for s in range(d // LANES):
        slab_ref[s] = x[:, s * LANES:(s + 1) * LANES]
    for s in range(d // LANES):
        for b in range(batch):
            dst_ref[b, :, s * LANES:(s + 1) * LANES] = slab_ref[s, pl.ds(b, steps, stride=batch), :]

def _prev_step(cur, prev_rolled, sub):
    cur_rolled = pltpu.roll(cur, SUBLANES // 2, axis=0)
    return jnp.where(sub < SUBLANES // 2, prev_rolled, cur_rolled), cur_rolled


def _odd_kernel(h_ref, gm_ref, win_ref, cw_ref, cb_ref, wgate_ref, bgate_ref, lam_ref,
                sa_ref, bw_ref, cwt_ref, sd_ref, wglu_ref, bglu_ref, wo_ref,
                o_ref,
                halo_ref, rg_a_ref, rg_b_ref, rg_x_ref, rg_carry_ref,
                uh_ref, cx_ref, s5_carry_ref, slab_ref):
    first = pl.program_id(0) == 0

    @pl.when(first)
    def _():
        halo_ref[...] = jnp.zeros_like(halo_ref)
        rg_carry_ref[...] = jnp.zeros_like(rg_carry_ref)
        s5_carry_ref[...] = jnp.zeros_like(s5_carry_ref)
        uh_ref[...] = jnp.zeros_like(uh_ref)

    tm = h_ref.shape[0] * h_ref.shape[1]
    groups = tm // SUBLANES
    h = _interleave(h_ref, slab_ref)
    xn = _rms(h, gm_ref[...]).astype(MXU_DTYPE)
    y = _dot(xn, win_ref[...])
    x_gate, x_rg, u = (y[:, i * RG_WIDTH:(i + 1) * RG_WIDTH] for i in range(3))

    step = SUBLANES // 2

    halo_rows = halo_ref.shape[0]
    ext = jnp.concatenate([halo_ref[...], x_rg], axis=0)
    halo_ref[...] = x_rg[tm - halo_rows:, :]
    cw = cw_ref[...]
    xc = cb_ref[...] + x_rg * cw[RG_CONV - 1:RG_CONV]
    for j in range(RG_CONV - 1):
        back = (RG_CONV - 1 - j) * step
        xc = xc + ext[halo_rows - back:halo_rows - back + tm, :] * cw[j:j + 1]

    gate_pre = _dot(xc.astype(MXU_DTYPE), wgate_ref[...]) + bgate_ref[...]

    ub = u.astype(MXU_DTYPE)
    half_u = S5_WIDTH // 2
    half_x = S5_LANES // 2
    u_halo = uh_ref.shape[0]
    u_ext = jnp.concatenate([uh_ref[...], u], axis=0)
    uh_ref[...] = u[tm - u_halo:, :]
    ub_prev = u_ext[u_halo - step:u_halo - step + tm, :].astype(MXU_DTYPE)
    for hf in range(2):
        cols = slice(hf * half_u, (hf + 1) * half_u)
        cx_ref[:, 2 * hf * half_x:2 * (hf + 1) * half_x] = _dot(
            jnp.concatenate([ub[:, cols], ub_prev[:, cols]], axis=1), bw_ref[hf])

    gates = _sigmoid(gate_pre)
    r_gate, i_gate = gates[:, :RG_WIDTH], gates[:, RG_WIDTH:]
    lam = lam_ref[...]
    softplus_neg = jnp.maximum(-lam, 0.0) + jnp.log1p(jnp.exp(-jnp.abs(lam)))
    log_a = (-RG_C * softplus_neg) * r_gate
    a = jnp.exp(log_a)
    rg_a_ref[...] = a
    var = -jnp.tanh(log_a) * (a * a + 1.0)
    rg_b_ref[...] = jnp.where(var > 0.0, var * lax.rsqrt(var), 0.0) * (i_gate * xc)

    sub = lax.broadcasted_iota(jnp.int32, (SUBLANES, RG_WIDTH), 0)

    x, ra_prev, rb_prev = rg_carry_ref[0], rg_carry_ref[1], rg_carry_ref[2]
    for g in range(groups):
        rows = slice(g * SUBLANES, (g + 1) * SUBLANES)
        a = rg_a_ref[rows, :]
        bx = rg_b_ref[rows, :]
        a_prev, ra_prev = _prev_step(a, ra_prev, sub)
        b_prev, rb_prev = _prev_step(bx, rb_prev, sub)
        x = (a * a_prev) * x + (a * b_prev + bx)
        rg_x_ref[rows, :] = x
    for i, v in enumerate((x, ra_prev, rb_prev)):
        rg_carry_ref[i] = v
    y_c = rg_x_ref[...] * _gelu_tanh(x_gate)

    for lc in range(S5_LANES // SCAN_LANES):
        hf, within = divmod(lc * SCAN_LANES, half_x)
        coef = slice(lc * SCAN_LANES, (lc + 1) * SCAN_LANES)
        re = slice(2 * hf * half_x + within, 2 * hf * half_x + within + SCAN_LANES)
        im = slice(re.start + half_x, re.stop + half_x)
        a2_re, a2_im = sa_ref[0, :, coef], sa_ref[1, :, coef]

        x_re, x_im = s5_carry_ref[0, :, coef], s5_carry_ref[1, :, coef]
        for g in range(groups):
            rows = slice(g * SUBLANES, (g + 1) * SUBLANES)
            x_re, x_im = (a2_re * x_re - a2_im * x_im + cx_ref[rows, re],
                          a2_re * x_im + a2_im * x_re + cx_ref[rows, im])
            cx_ref[rows, re] = x_re
            cx_ref[rows, im] = x_im
        s5_carry_ref[0, :, coef] = x_re
        s5_carry_ref[1, :, coef] = x_im

    ys = jnp.concatenate(
        [_dot(cx_ref[:, 2 * hf * half_x:2 * (hf + 1) * half_x].astype(MXU_DTYPE), cwt_ref[hf])
         for hf in range(2)], axis=1) + sd_ref[...] * u
    ys = _gelu_tanh(ys)
    y_d = ys * _sigmoid(_dot(ys.astype(MXU_DTYPE), wglu_ref[...]) + bglu_ref[...])
    out = (h + _dot(y_c.astype(MXU_DTYPE), wo_ref[:RG_WIDTH, :])
           + _dot(y_d.astype(MXU_DTYPE), wo_ref[RG_WIDTH:, :]))
    _deinterleave(out, slab_ref, o_ref)


def _block_diag(blocks):
    n, r, c = blocks.shape
    eye = jnp.eye(n, dtype=blocks.dtype)
    return (eye[:, None, :, None] * blocks[:, :, None, :]).reshape(n * r, n * c)


def _odd_mixer(h, mix_norm, w_in, conv_w, conv_b, w_a, b_a, w_i, b_i, lam,
               a_re, a_im, log_dt, b_re, b_im, c_re, c_im, d_skip, w_glu, b_glu, w_out, *,
               batch):
    assert batch == SUBLANES // 2
    tokens, d = h.shape
    seq = tokens // batch
    tm = ODD_ROWS
    steps = tm // batch

    dt = jnp.exp(log_dt.astype(F32))[:, None]
    lr, li = a_re.astype(F32), a_im.astype(F32)
    mag = jnp.exp(lr * dt)
    ab_re = mag * jnp.cos(li * dt)
    ab_im = mag * jnp.sin(li * dt)
    den = lr * lr + li * li
    nr, ni = ab_re - 1.0, ab_im
    coef_re = (nr * lr + ni * li) / den
    coef_im = (ni * lr - nr * li) / den
    bb_re = coef_re[..., None] * b_re - coef_im[..., None] * b_im
    bb_im = coef_re[..., None] * b_im + coef_im[..., None] * b_re
    abb_re = ab_re[..., None] * bb_re - ab_im[..., None] * bb_im
    abb_im = ab_re[..., None] * bb_im + ab_im[..., None] * bb_re
    halves = [slice(0, S5_GROUPS // 2), slice(S5_GROUPS // 2, S5_GROUPS)]

    def in_map(m_re, m_im, g):
        return jnp.concatenate([_block_diag(m_re[g].transpose(0, 2, 1)),
                                _block_diag(m_im[g].transpose(0, 2, 1))], axis=1)

    bw = jnp.stack([jnp.concatenate([in_map(bb_re, bb_im, g), in_map(abb_re, abb_im, g)],
                                    axis=0) for g in halves])
    cwt = jnp.stack([jnp.concatenate([_block_diag(c_re[g].transpose(0, 2, 1)),
                                      -_block_diag(c_im[g].transpose(0, 2, 1))], axis=0)
                     for g in halves])
    a2_re = ab_re * ab_re - ab_im * ab_im
    a2_im = 2.0 * ab_re * ab_im
    sa = jnp.stack([jnp.broadcast_to(v.reshape(1, -1), (SUBLANES, S5_LANES))
                    for v in (a2_re, a2_im)])
    wgate = jnp.concatenate([_block_diag(w_a), _block_diag(w_i)], axis=1)
    bgate = jnp.concatenate([b_a, b_i]).reshape(1, -1)

    cast = lambda w: w.astype(MXU_DTYPE)
    weights = [mix_norm.reshape(1, d), cast(w_in), conv_w, conv_b.reshape(1, -1), cast(wgate),
               bgate, lam.reshape(1, -1), sa, cast(bw), cast(cwt), d_skip.reshape(1, -1),
               cast(w_glu), b_glu.reshape(1, -1), cast(w_out)]
    row = pl.BlockSpec((batch, steps, d), lambda r: (0, r, 0))
    out = pl.pallas_call(
        _odd_kernel,
        grid=(seq // steps,),
        in_specs=[row] + [_resident(w.shape) for w in weights],
        out_specs=row,
        out_shape=jax.ShapeDtypeStruct((batch, seq, d), F32),
        scratch_shapes=[
            pltpu.VMEM((2 * SUBLANES, RG_WIDTH), F32),
            pltpu.VMEM((tm, RG_WIDTH), F32),
            pltpu.VMEM((tm, RG_WIDTH), F32),
            pltpu.VMEM((tm, RG_WIDTH), F32),
            pltpu.VMEM((3, SUBLANES, RG_WIDTH), F32),
            pltpu.VMEM((SUBLANES, S5_WIDTH), F32),
            pltpu.VMEM((tm, 2 * S5_LANES), F32),
            pltpu.VMEM((2, SUBLANES, S5_LANES), F32),
            pltpu.VMEM((d // LANES, tm, LANES), F32),
        ],
        compiler_params=_params("arbitrary"),
        name="odd_mixer",
    )(h.reshape(batch, seq, d), *weights)
    return out.reshape(tokens, d)


def kernel(x, p, ffn_a_norm, ffn_a_w1, ffn_a_w3, ffn_a_w2, mix_norm, ffn_b_norm, ffn_b_w1, ffn_b_w3, ffn_b_w2, ple_norm, ple_w_gate, ple_w_up, ev_w_in, mla_q_norm, mla_w_q_up, mla_kv_norm, mla_w_kv_up, gla_w_gate_up, gla_b_gate, gla_out_norm, ev_w_out, od_w_in, rg_conv_w, rg_conv_b, rg_w_a, rg_b_a, rg_w_i, rg_b_i, rg_lambda, s5_a_re, s5_a_im, s5_log_dt, s5_b_re, s5_b_im, s5_c_re, s5_c_im, s5_d, s5_w_glu, s5_b_glu, od_w_out, final_norm):
    batch, seq, d = x.shape
    depth = p.shape[0]
    h = x.reshape(batch * seq, d)
    cast = lambda w: w.astype(MXU_DTYPE)
    ffn_a = (ffn_a_norm, cast(ffn_a_w1), cast(ffn_a_w3), cast(ffn_a_w2))
    ffn_b = (ffn_b_norm, cast(ffn_b_w1), cast(ffn_b_w3), cast(ffn_b_w2))
    ple = (p, ple_norm, cast(ple_w_gate), cast(ple_w_up))
    for i in range(depth):
        j = i // 2
        h = _ffn(h, i, *ffn_a)
        if i % 2 == 0:
            q, k, v, lq, lk, lv, lg, lr = _even_proj(
                h, mix_norm[i], ev_w_in[j], mla_q_norm[j], mla_w_q_up[j], mla_kv_norm[j],
                mla_w_kv_up[j], gla_w_gate_up[j], gla_b_gate[j], batch=batch)
            o_attn = _attention(q, k, v, batch=batch)
            h = _gla_out(lq, lk, lv, lg, lr, o_attn, h, gla_out_norm[j], ev_w_out[j],
                         batch=batch)
        else:
            h = _odd_mixer(h, mix_norm[i], od_w_in[j], rg_conv_w[j], rg_conv_b[j], rg_w_a[j],
                           rg_b_a[j], rg_w_i[j], rg_b_i[j], rg_lambda[j], s5_a_re[j],
                           s5_a_im[j], s5_log_dt[j], s5_b_re[j], s5_b_im[j], s5_c_re[j],
                           s5_c_im[j], s5_d[j], s5_w_glu[j], s5_b_glu[j], od_w_out[j],
                           batch=batch)
        last = i == depth - 1
        h = _ffn(h, i, *ffn_b, ple=ple, final_norm=final_norm if last else None)
    return h.reshape(batch, seq, d)
```

```python
import functools
import math

import jax
import jax.numpy as jnp
from jax import lax
from jax.experimental import pallas as pl
from jax.experimental.pallas import tpu as pltpu

MXU_DTYPE = jnp.bfloat16
F32 = jnp.float32

EPS = 1e-6
PLE_DIM = 256
MLA_HEADS = 8
MLA_NOPE = 64
MLA_ROPE = 32
MLA_V = 64
MLA_Q_RANK = 384
MLA_KV_RANK = 256
ROPE_BASE = 10000.0
GLA_HEADS = 4
GLA_DK = 64
GLA_DV = 128
GLA_GATE_RANK = 16
GLA_TAU = 16.0
GLA_CHUNK = 64
RG_WIDTH = 512
RG_BLOCKS = 8
RG_CONV = 4
RG_C = 8.0
S5_GROUP = 16
S5_GROUPS = 32
S5_STATE = 64
S5_WIDTH = S5_GROUP * S5_GROUPS
S5_LANES = S5_GROUPS * S5_STATE

LANES = 128
SUBLANES = 8
MXU_TILE = 256
HEAD_PAD = LANES
VMEM_LIMIT = 56 * 1024 * 1024

FFN_ROWS = 512
FFN_CHUNKS = 2
PROJ_ROWS = 512
ATTN_Q = 512
GLA_ROWS = 512
ODD_ROWS = 512
SCAN_LANES = 512


def _rms(x, g):
    return x * lax.rsqrt(jnp.mean(x * x, axis=-1, keepdims=True) + EPS) * g


def _dot(a, b):
    return jnp.dot(a, b, preferred_element_type=F32)


def _dot_nt(a, b):
    return lax.dot_general(a, b, (((1,), (1,)), ((), ())), preferred_element_type=F32)


def _dot_tn(a, b):
    return lax.dot_general(a, b, (((0,), (0,)), ((), ())), preferred_element_type=F32)


def _sigmoid(x):
    return 1.0 / (1.0 + jnp.exp(-x))


def _gelu_tanh(x):
    return 0.5 * x * (1.0 + jnp.tanh(math.sqrt(2.0 / math.pi) * (x + 0.044715 * (x * x * x))))


def _log_sigmoid(x):
    return jnp.minimum(x, 0.0) - jnp.log1p(jnp.exp(-jnp.abs(x)))


def _resident(shape):
    return pl.BlockSpec(shape, lambda *_: (0,) * len(shape), pipeline_mode=pl.Buffered(1))


def _params(*semantics):
    return pltpu.CompilerParams(dimension_semantics=semantics, vmem_limit_bytes=VMEM_LIMIT)


def _ffn_kernel(*refs, ple, final):
    h_ref, g_ref, w1_ref, w3_ref, w2_ref = refs[:5]
    rest = list(refs[5:])
    o_ref = rest.pop()
    h = h_ref[...]
    xn = _rms(h, g_ref[...]).astype(MXU_DTYPE)
    tiles = w1_ref.shape[1] // MXU_TILE
    bounds = [MXU_TILE * ((tiles * c + FFN_CHUNKS - 1) // FFN_CHUNKS) for c in range(FFN_CHUNKS + 1)]
    y = None
    for lo, hi in zip(bounds[:-1], bounds[1:]):
        a = _dot(xn, w1_ref[:, lo:hi])
        b = _dot(xn, w3_ref[:, lo:hi])
        t = (a * _sigmoid(a) * b).astype(MXU_DTYPE)
        yc = _dot(t, w2_ref[lo:hi, :])
        y = yc if y is None else y + yc
    h = h + 0.5 * y
    if ple:
        p_ref, gp_ref, wg_ref, wu_ref = rest[:4]
        rest = rest[4:]
        gate = _sigmoid(_dot(_rms(h, gp_ref[...]).astype(MXU_DTYPE), wg_ref[...]))
        up = _dot(p_ref[...].astype(MXU_DTYPE), wu_ref[...])
        h = h + up * gate
    if final:
        h = _rms(h, rest[0][...])
    o_ref[...] = h


def _ffn(h, layer, norm, w1, w3, w2, *, ple=None, final_norm=None):
    tokens, d = h.shape
    tm = FFN_ROWS
    hspec = pl.BlockSpec((tm, d), lambda r: (r, 0))

    def of_layer(a):
        rest = a.shape[1:]
        return pl.BlockSpec((None,) + rest, lambda r: (layer,) + (0,) * len(rest),
                            pipeline_mode=pl.Buffered(1))

    gains = lambda g: g.reshape(g.shape[0], 1, d)
    args = [h, gains(norm), w1, w3, w2]
    specs = [hspec] + [of_layer(a) for a in args[1:]]
    if ple is not None:
        p, gp, wg, wu = ple
        p = p.reshape(p.shape[0], tokens, p.shape[-1])
        args += [p, gains(gp), wg, wu]
        specs += [pl.BlockSpec((None, tm, p.shape[-1]), lambda r: (layer, r, 0)),
                  of_layer(args[-3]), of_layer(wg), of_layer(wu)]
    if final_norm is not None:
        args.append(final_norm.reshape(1, d))
        specs.append(_resident((1, d)))
    return pl.pallas_call(
        functools.partial(_ffn_kernel, ple=ple is not None, final=final_norm is not None),
        grid=(tokens // tm,),
        in_specs=specs,
        out_specs=hspec,
        out_shape=jax.ShapeDtypeStruct((tokens, d), F32),
        compiler_params=_params("parallel"),
        name="ffn",
    )(*args)


_EV_COLS = (MLA_Q_RANK, MLA_KV_RANK, HEAD_PAD, GLA_HEADS * GLA_DK, GLA_HEADS * GLA_DK,
            GLA_HEADS * GLA_DV, GLA_HEADS * GLA_DV)
ROT_SHIFT = HEAD_PAD - MLA_ROPE


def _even_proj_kernel(h_ref, gm_ref, win_ref, gq_ref, wq_ref, gkv_ref, wk_ref, wv_ref,
                      wg_ref, bg_ref, cq_ref, sq_ref, ck_ref, sk_ref,
                      q_ref, k_ref, v_ref, lq_ref, lk_ref, lv_ref, lg_ref, lr_ref):
    xn = _rms(h_ref[...], gm_ref[...]).astype(MXU_DTYPE)
    y = _dot(xn, win_ref[...])
    offs = [0]
    for w in _EV_COLS:
        offs.append(offs[-1] + w)
    c_q, c_kv, kr, lq, lk, lv, lr = (
        y[:, offs[i]:offs[i + 1]] for i in range(len(_EV_COLS)))
    g_low = kr

    cqn = _rms(c_q, gq_ref[...]).astype(MXU_DTYPE)
    qf = _dot(cqn, wq_ref[...])
    ckn = _rms(c_kv, gkv_ref[...]).astype(MXU_DTYPE)
    kn = _dot(ckn, wk_ref[...])
    v_ref[...] = _dot(ckn, wv_ref[...]).astype(v_ref.dtype)
    k_rope = kr * ck_ref[...] + pltpu.roll(kr, ROT_SHIFT, axis=1) * sk_ref[...]
    cq, sq = cq_ref[...], sq_ref[...]
    for hd in range(MLA_HEADS):
        sl = slice(hd * HEAD_PAD, (hd + 1) * HEAD_PAD)
        q_h = qf[:, sl]
        q_ref[:, sl] = (q_h * cq + pltpu.roll(q_h, ROT_SHIFT, axis=1) * sq).astype(q_ref.dtype)
        k_ref[:, sl] = (kn[:, sl] + k_rope).astype(k_ref.dtype)

    gate = _dot(g_low.astype(MXU_DTYPE), wg_ref[...]) + bg_ref[...]
    lg_ref[...] = _log_sigmoid(gate) * (1.0 / GLA_TAU)
    lq_ref[...] = lq
    lk_ref[...] = lk
    lv_ref[...] = lv.astype(lv_ref.dtype)
    lr_ref[...] = lr


def _pad_heads(w, heads, width):
    lead = w.shape[:-1]
    w = w.reshape(lead + (heads, width))
    w = jnp.pad(w, [(0, 0)] * len(lead) + [(0, 0), (0, HEAD_PAD - width)])
    return w.reshape(lead + (heads * HEAD_PAD,))


def _rot_half(w):
    half = MLA_ROPE // 2
    return jnp.concatenate([-w[..., half:], w[..., :half]], axis=-1)


def _head_slot(w_nope, w_rope):
    return jnp.concatenate([w_nope, w_rope, _rot_half(w_rope)], axis=-1)


def _even_proj(h, mix_norm, w_in, q_norm, w_q_up, kv_norm, w_kv_up, w_gate_up, b_gate, *,
               batch):
    tokens, d = h.shape
    seq = tokens // batch
    tm = PROJ_ROWS
    n_s = seq // tm
    splits = [MLA_Q_RANK, MLA_KV_RANK, MLA_ROPE, GLA_HEADS * GLA_DK, GLA_HEADS * GLA_DK,
              GLA_HEADS * GLA_DV, GLA_GATE_RANK, GLA_HEADS * GLA_DV]
    offs = [0]
    for s in splits:
        offs.append(offs[-1] + s)
    wc_q, wc_kv, w_kr, w_lq, w_lk, w_lv, w_gl, w_lr = (
        w_in[:, offs[i]:offs[i + 1]] for i in range(len(splits)))
    gl_lanes = jnp.pad(w_gl, [(0, 0), (0, MLA_NOPE - GLA_GATE_RANK)])
    win = jnp.concatenate([
        wc_q, wc_kv, _head_slot(gl_lanes, w_kr), w_lq, w_lk, w_lv, w_lr], axis=1)

    wq = w_q_up.reshape(MLA_Q_RANK, MLA_HEADS, MLA_NOPE + MLA_ROPE)
    wq_full = _head_slot(wq[..., :MLA_NOPE], wq[..., MLA_NOPE:]).reshape(MLA_Q_RANK, -1)
    wkv = w_kv_up.reshape(MLA_KV_RANK, MLA_HEADS, MLA_NOPE + MLA_V)
    wk = _pad_heads(wkv[..., :MLA_NOPE].reshape(MLA_KV_RANK, -1), MLA_HEADS, MLA_NOPE)
    wv = wkv[..., MLA_NOPE:].reshape(MLA_KV_RANK, MLA_HEADS * MLA_V)
    wg = jnp.pad(w_gate_up, [(0, HEAD_PAD - GLA_GATE_RANK), (0, 0)])
    bg = b_gate.reshape(1, -1)

    half = MLA_ROPE // 2
    inv = ROPE_BASE ** (-jnp.arange(half, dtype=F32) / half)
    ang = jnp.arange(seq).astype(F32)[:, None] * inv[None, :]
    cos2 = jnp.tile(jnp.cos(ang), (1, 2))
    sin2 = jnp.tile(jnp.sin(ang), (1, 2))
    scale = (MLA_NOPE + MLA_ROPE) ** -0.5 * math.log2(math.e)
    ones = jnp.ones((seq, MLA_NOPE), F32)
    tail = jnp.zeros((seq, HEAD_PAD - MLA_NOPE - MLA_ROPE), F32)
    cq = jnp.concatenate([ones, cos2, tail], axis=1) * scale
    sq = jnp.concatenate([0 * ones, sin2, tail], axis=1) * scale
    ck = jnp.concatenate([0 * ones, cos2, tail], axis=1)
    sk = jnp.concatenate([0 * ones, sin2, tail], axis=1)

    cast = lambda w: w.astype(MXU_DTYPE)
    row = lambda w: pl.BlockSpec((tm, w), lambda r: (r, 0))
    tab = pl.BlockSpec((tm, HEAD_PAD), lambda r: (r % n_s, 0))
    gl_w = GLA_HEADS * GLA_DK
    gv_w = GLA_HEADS * GLA_DV
    qk_w = MLA_HEADS * HEAD_PAD
    outs = [(qk_w, MXU_DTYPE), (qk_w, MXU_DTYPE), (MLA_HEADS * MLA_V, MXU_DTYPE),
            (gl_w, F32), (gl_w, F32), (gv_w, MXU_DTYPE), (gl_w, F32), (gv_w, F32)]
    weights = [mix_norm.reshape(1, d), cast(win), q_norm.reshape(1, -1), cast(wq_full),
               kv_norm.reshape(1, -1), cast(wk), cast(wv), cast(wg), bg]
    return pl.pallas_call(
        _even_proj_kernel,
        grid=(tokens // tm,),
        in_specs=[row(d)] + [_resident(w.shape) for w in weights] + [tab] * 4,
        out_specs=[row(w) for w, _ in outs],
        out_shape=[jax.ShapeDtypeStruct((tokens, w), dt) for w, dt in outs],
        compiler_params=_params("parallel"),
        name="even_proj",
    )(h, *weights, cq, sq, ck, sk)


def _attn_kernel(q_ref, k_ref, v_ref, o_ref, s_ref, m_ref, l_ref, acc_ref, *, tq):
    n_q = q_ref.shape[0] // tq
    row = lax.broadcasted_iota(jnp.int32, (tq, tq), 0)
    col = lax.broadcasted_iota(jnp.int32, (tq, tq), 1)
    causal = col <= row
    sls = [slice(hd * HEAD_PAD, (hd + 1) * HEAD_PAD) for hd in range(2)]
    lane = lax.broadcasted_iota(jnp.int32, (tq, 2 * MLA_V), 1)

    def tile(i):
        return pl.ds(pl.multiple_of(i * tq, tq), tq)

    def scores(hd, qi, j):
        return _dot_nt(q_ref[tile(qi), sls[hd]], k_ref[tile(j), sls[hd]])

    def softmax_pv(hd, s, j, masked):
        if masked:
            s = jnp.where(causal, s, -jnp.inf)
        m = m_ref[hd]
        m_new = jnp.maximum(m, jnp.max(s, axis=1, keepdims=True))
        alpha = jnp.exp2(m - m_new)
        p = jnp.exp2(s - jnp.concatenate([m_new] * (tq // LANES), axis=1))
        l_ref[hd] = alpha * l_ref[hd] + jnp.sum(p, axis=1, keepdims=True)
        m_ref[hd] = m_new
        acc_ref[hd] = alpha * acc_ref[hd] + _dot(p.astype(MXU_DTYPE), v_ref[tile(j), :])

    def step(qi, j, masked, next_qi, next_j):
        s_b = scores(1, qi, j)
        softmax_pv(0, s_ref[...], j, masked)
        s_ref[...] = scores(0, next_qi, next_j)
        softmax_pv(1, s_b, j, masked)

    s_ref[...] = scores(0, 0, 0)

    def q_tile(qi, _):
        m_ref[...] = jnp.full(m_ref.shape, -jnp.inf, F32)
        l_ref[...] = jnp.zeros(l_ref.shape, F32)
        acc_ref[...] = jnp.zeros(acc_ref.shape, F32)

        def body(j, _):
            step(qi, j, False, qi, j + 1)
            return 0

        lax.fori_loop(0, qi, body, 0)
        step(qi, qi, True, jnp.minimum(qi + 1, n_q - 1), 0)
        o_ref[tile(qi), :] = jnp.where(lane < MLA_V, acc_ref[0] / l_ref[0],
                                       acc_ref[1] / l_ref[1]).astype(o_ref.dtype)
        return 0

    lax.fori_loop(0, n_q, q_tile, 0)


def _attention(q, k, v, *, batch):
    tokens = q.shape[0]
    seq = tokens // batch
    tq = ATTN_Q
    q3 = q.reshape(batch, seq, -1)
    k3 = k.reshape(batch, seq, -1)
    v3 = v.reshape(batch, seq, -1)
    pair = 2 * HEAD_PAD
    qk_spec = pl.BlockSpec((None, seq, pair), lambda b, hp: (b, 0, hp))
    v_spec = pl.BlockSpec((None, seq, 2 * MLA_V), lambda b, hp: (b, 0, hp))
    out = pl.pallas_call(
        functools.partial(_attn_kernel, tq=tq),
        grid=(batch, MLA_HEADS // 2),
        in_specs=[qk_spec, qk_spec, v_spec],
        out_specs=v_spec,
        out_shape=jax.ShapeDtypeStruct(v3.shape, MXU_DTYPE),
        scratch_shapes=[pltpu.VMEM((tq, tq), F32), pltpu.VMEM((2, tq, LANES), F32),
                        pltpu.VMEM((2, tq, LANES), F32), pltpu.VMEM((2, tq, 2 * MLA_V), F32)],
        compiler_params=_params("parallel", "parallel"),
        name="attention",
    )(q3, k3, v3)
    return out.reshape(tokens, -1)


def _gla_out_kernel(lq_ref, lk_ref, lv_ref, lg_ref, lr_ref, oa_ref, h_ref, gn_ref, wo_ref,
                    o_ref, state_ref, y_ref):
    @pl.when(pl.program_id(1) == 0)
    def _():
        state_ref[...] = jnp.zeros_like(state_ref)

    tt, width = lg_ref.shape
    chunk = GLA_CHUNK
    pos = lax.broadcasted_iota(jnp.int32, (tt, width), 0) & (chunk - 1)
    b = lg_ref[...]
    d = 1
    while d < chunk:
        b = b + jnp.where(pos >= d, pltpu.roll(b, d, axis=0), 0.0)
        d *= 2
    lk = lk_ref[...]
    parity = lambda rows: (lax.broadcasted_iota(jnp.int32, (rows, width), 1) // GLA_DK) % 2
    q_e = lq_ref[...] * (GLA_DK ** -0.5) * jnp.exp(b)
    q_par = [jnp.where(parity(tt) == par, q_e, 0.0).astype(MXU_DTYPE) for par in range(2)]
    k_e = (lk * jnp.exp(-b)).astype(MXU_DTYPE)
    tri = (lax.broadcasted_iota(jnp.int32, (chunk, chunk), 1)
           <= lax.broadcasted_iota(jnp.int32, (chunk, chunk), 0))
    gn = gn_ref[...]
    for c in range(tt // chunk):
        rows = slice(c * chunk, (c + 1) * chunk)
        b_c = b[rows]
        b_last = b_c[chunk - 1:chunk]
        k_end = lk[rows] * jnp.exp(b_last - b_c)
        k_end_par = [jnp.where(parity(chunk) == par, k_end, 0.0).astype(MXU_DTYPE)
                     for par in range(2)]
        decay = jnp.exp(b_last)
        for hd in range(GLA_HEADS):
            sl = slice((hd // 2) * LANES, (hd // 2 + 1) * LANES)
            vs = slice(hd * GLA_DV, (hd + 1) * GLA_DV)
            v_c = lv_ref[rows, vs]
            q_h = q_par[hd % 2][rows, sl]
            att = jnp.where(tri, _dot_nt(q_h, k_e[rows, sl]), 0.0)
            st = state_ref[hd]
            o = _dot(att.astype(MXU_DTYPE), v_c) + _dot_nt(q_h, st.astype(MXU_DTYPE))
            state_ref[hd] = st * decay[:, sl] + _dot_tn(v_c, k_end_par[hd % 2][:, sl])
            o = _rms(o, gn)
            r = lr_ref[rows, vs]
            y_ref[rows, vs] = (o * (r * _sigmoid(r))).astype(y_ref.dtype)
    n_a = oa_ref.shape[1]
    o_ref[...] = (h_ref[...] + _dot(oa_ref[...], wo_ref[:n_a, :])
                  + _dot(y_ref[...], wo_ref[n_a:, :]))


def _gla_out(lq, lk, lv, lg, lr, o_attn, h, out_norm, w_out, *, batch):
    tokens, d = h.shape
    seq = tokens // batch
    tt = GLA_ROWS
    n_s = seq // tt
    row = lambda w: pl.BlockSpec((tt, w), lambda b, i: (b * n_s + i, 0))
    ins = [lq, lk, lv, lg, lr, o_attn, h]
    return pl.pallas_call(
        _gla_out_kernel,
        grid=(batch, n_s),
        in_specs=[row(a.shape[1]) for a in ins] + [_resident((1, GLA_DV)),
                                                   _resident(w_out.shape)],
        out_specs=row(d),
        out_shape=jax.ShapeDtypeStruct((tokens, d), F32),
        scratch_shapes=[pltpu.VMEM((GLA_HEADS, GLA_DV, HEAD_PAD), F32),
                        pltpu.VMEM((tt, GLA_HEADS * GLA_DV), MXU_DTYPE)],
        compiler_params=_params("parallel", "arbitrary"),
        name="gla_out",
    )(*ins, out_norm.reshape(1, -1), w_out.astype(MXU_DTYPE))


def _interleave(src_ref, slab_ref):
    batch, steps, d = src_ref.shape
    for s in range(d // LANES):
        for b in range(batch):
            slab_ref[s, pl.ds(b, steps, stride=batch), :] = src_ref[b, :, s * LANES:(s + 1) * LANES]
    return jnp.concatenate([slab_ref[s] for s in range(d // LANES)], axis=1)


def _deinterleave(x, slab_ref, dst_ref):
    batch, steps, d = dst_ref.shape
    for s in range(d // LANES):
        slab_ref[s] = x[:, s * LANES:(s + 1) * LANES]
    for s in range(d // LANES):
        for b in range(batch):
            dst_ref[b, :, s * LANES:(s + 1) * LANES] = slab_ref[s, pl.ds(b, steps, stride=batch), :]

def _prev_step(cur, prev_rolled, sub):
    cur_rolled = pltpu.roll(cur, SUBLANES // 2, axis=0)
    return jnp.where(sub < SUBLANES // 2, prev_rolled, cur_rolled), cur_rolled


def _odd_kernel(h_ref, gm_ref, win_ref, cw_ref, cb_ref, wgate_ref, bgate_ref, lam_ref,
                sa_ref, bw_ref, cwt_ref, sd_ref, wglu_ref, bglu_ref, wo_ref,
                o_ref,
                halo_ref, rg_a_ref, rg_b_ref, rg_x_ref, rg_carry_ref,
                uh_ref, cx_ref, s5_carry_ref, slab_ref):
    first = pl.program_id(0) == 0

    @pl.when(first)
    def _():
        halo_ref[...] = jnp.zeros_like(halo_ref)
        rg_carry_ref[...] = jnp.zeros_like(rg_carry_ref)
        s5_carry_ref[...] = jnp.zeros_like(s5_carry_ref)
        uh_ref[...] = jnp.zeros_like(uh_ref)

    tm = h_ref.shape[0] * h_ref.shape[1]
    groups = tm // SUBLANES
    h = _interleave(h_ref, slab_ref)
    xn = _rms(h, gm_ref[...]).astype(MXU_DTYPE)
    y = _dot(xn, win_ref[...])
    x_gate, x_rg, u = (y[:, i * RG_WIDTH:(i + 1) * RG_WIDTH] for i in range(3))

    step = SUBLANES // 2

    halo_rows = halo_ref.shape[0]
    ext = jnp.concatenate([halo_ref[...], x_rg], axis=0)
    halo_ref[...] = x_rg[tm - halo_rows:, :]
    cw = cw_ref[...]
    xc = cb_ref[...] + x_rg * cw[RG_CONV - 1:RG_CONV]
    for j in range(RG_CONV - 1):
        back = (RG_CONV - 1 - j) * step
        xc = xc + ext[halo_rows - back:halo_rows - back + tm, :] * cw[j:j + 1]

    gate_pre = _dot(xc.astype(MXU_DTYPE), wgate_ref[...]) + bgate_ref[...]

    ub = u.astype(MXU_DTYPE)
    half_u = S5_WIDTH // 2
    half_x = S5_LANES // 2
    u_halo = uh_ref.shape[0]
    u_ext = jnp.concatenate([uh_ref[...], u], axis=0)
    uh_ref[...] = u[tm - u_halo:, :]
    ub_prev = u_ext[u_halo - step:u_halo - step + tm, :].astype(MXU_DTYPE)
    for hf in range(2):
        cols = slice(hf * half_u, (hf + 1) * half_u)
        cx_ref[:, 2 * hf * half_x:2 * (hf + 1) * half_x] = _dot(
            jnp.concatenate([ub[:, cols], ub_prev[:, cols]], axis=1), bw_ref[hf])

    gates = _sigmoid(gate_pre)
    r_gate, i_gate = gates[:, :RG_WIDTH], gates[:, RG_WIDTH:]
    lam = lam_ref[...]
    softplus_neg = jnp.maximum(-lam, 0.0) + jnp.log1p(jnp.exp(-jnp.abs(lam)))
    log_a = (-RG_C * softplus_neg) * r_gate
    a = jnp.exp(log_a)
    rg_a_ref[...] = a
    var = -jnp.tanh(log_a) * (a * a + 1.0)
    rg_b_ref[...] = jnp.where(var > 0.0, var * lax.rsqrt(var), 0.0) * (i_gate * xc)

    sub = lax.broadcasted_iota(jnp.int32, (SUBLANES, RG_WIDTH), 0)

    x, ra_prev, rb_prev = rg_carry_ref[0], rg_carry_ref[1], rg_carry_ref[2]
    for g in range(groups):
        rows = slice(g * SUBLANES, (g + 1) * SUBLANES)
        a = rg_a_ref[rows, :]
        bx = rg_b_ref[rows, :]
        a_prev, ra_prev = _prev_step(a, ra_prev, sub)
        b_prev, rb_prev = _prev_step(bx, rb_prev, sub)
        x = (a * a_prev) * x + (a * b_prev + bx)
        rg_x_ref[rows, :] = x
    for i, v in enumerate((x, ra_prev, rb_prev)):
        rg_carry_ref[i] = v
    y_c = rg_x_ref[...] * _gelu_tanh(x_gate)

    for lc in range(S5_LANES // SCAN_LANES):
        hf, within = divmod(lc * SCAN_LANES, half_x)
        coef = slice(lc * SCAN_LANES, (lc + 1) * SCAN_LANES)
        re = slice(2 * hf * half_x + within, 2 * hf * half_x + within + SCAN_LANES)
        im = slice(re.start + half_x, re.stop + half_x)
        a2_re, a2_im = sa_ref[0, :, coef], sa_ref[1, :, coef]

        x_re, x_im = s5_carry_ref[0, :, coef], s5_carry_ref[1, :, coef]
        for g in range(groups):
            rows = slice(g * SUBLANES, (g + 1) * SUBLANES)
            x_re, x_im = (a2_re * x_re - a2_im * x_im + cx_ref[rows, re],
                          a2_re * x_im + a2_im * x_re + cx_ref[rows, im])
            cx_ref[rows, re] = x_re
            cx_ref[rows, im] = x_im
        s5_carry_ref[0, :, coef] = x_re
        s5_carry_ref[1, :, coef] = x_im

    ys = jnp.concatenate(
        [_dot(cx_ref[:, 2 * hf * half_x:2 * (hf + 1) * half_x].astype(MXU_DTYPE), cwt_ref[hf])
         for hf in range(2)], axis=1) + sd_ref[...] * u
    ys = _gelu_tanh(ys)
    y_d = ys * _sigmoid(_dot(ys.astype(MXU_DTYPE), wglu_ref[...]) + bglu_ref[...])
    out = (h + _dot(y_c.astype(MXU_DTYPE), wo_ref[:RG_WIDTH, :])
           + _dot(y_d.astype(MXU_DTYPE), wo_ref[RG_WIDTH:, :]))
    _deinterleave(out, slab_ref, o_ref)


def _block_diag(blocks):
    n, r, c = blocks.shape
    eye = jnp.eye(n, dtype=blocks.dtype)
    return (eye[:, None, :, None] * blocks[:, :, None, :]).reshape(n * r, n * c)


def _odd_mixer(h, mix_norm, w_in, conv_w, conv_b, w_a, b_a, w_i, b_i, lam,
               a_re, a_im, log_dt, b_re, b_im, c_re, c_im, d_skip, w_glu, b_glu, w_out, *,
               batch):
    assert batch == SUBLANES // 2
    tokens, d = h.shape
    seq = tokens // batch
    tm = ODD_ROWS
    steps = tm // batch

    dt = jnp.exp(log_dt.astype(F32))[:, None]
    lr, li = a_re.astype(F32), a_im.astype(F32)
    mag = jnp.exp(lr * dt)
    ab_re = mag * jnp.cos(li * dt)
    ab_im = mag * jnp.sin(li * dt)
    den = lr * lr + li * li
    nr, ni = ab_re - 1.0, ab_im
    coef_re = (nr * lr + ni * li) / den
    coef_im = (ni * lr - nr * li) / den
    bb_re = coef_re[..., None] * b_re - coef_im[..., None] * b_im
    bb_im = coef_re[..., None] * b_im + coef_im[..., None] * b_re
    abb_re = ab_re[..., None] * bb_re - ab_im[..., None] * bb_im
    abb_im = ab_re[..., None] * bb_im + ab_im[..., None] * bb_re
    halves = [slice(0, S5_GROUPS // 2), slice(S5_GROUPS // 2, S5_GROUPS)]

    def in_map(m_re, m_im, g):
        return jnp.concatenate([_block_diag(m_re[g].transpose(0, 2, 1)),
                                _block_diag(m_im[g].transpose(0, 2, 1))], axis=1)

    bw = jnp.stack([jnp.concatenate([in_map(bb_re, bb_im, g), in_map(abb_re, abb_im, g)],
                                    axis=0) for g in halves])
    cwt = jnp.stack([jnp.concatenate([_block_diag(c_re[g].transpose(0, 2, 1)),
                                      -_block_diag(c_im[g].transpose(0, 2, 1))], axis=0)
                     for g in halves])
    a2_re = ab_re * ab_re - ab_im * ab_im
    a2_im = 2.0 * ab_re * ab_im
    sa = jnp.stack([jnp.broadcast_to(v.reshape(1, -1), (SUBLANES, S5_LANES))
                    for v in (a2_re, a2_im)])
    wgate = jnp.concatenate([_block_diag(w_a), _block_diag(w_i)], axis=1)
    bgate = jnp.concatenate([b_a, b_i]).reshape(1, -1)

    cast = lambda w: w.astype(MXU_DTYPE)
    weights = [mix_norm.reshape(1, d), cast(w_in), conv_w, conv_b.reshape(1, -1), cast(wgate),
               bgate, lam.reshape(1, -1), sa, cast(bw), cast(cwt), d_skip.reshape(1, -1),
               cast(w_glu), b_glu.reshape(1, -1), cast(w_out)]
    row = pl.BlockSpec((batch, steps, d), lambda r: (0, r, 0))
    out = pl.pallas_call(
        _odd_kernel,
        grid=(seq // steps,),
        in_specs=[row] + [_resident(w.shape) for w in weights],
        out_specs=row,
        out_shape=jax.ShapeDtypeStruct((batch, seq, d), F32),
        scratch_shapes=[
            pltpu.VMEM((2 * SUBLANES, RG_WIDTH), F32),
            pltpu.VMEM((tm, RG_WIDTH), F32),
            pltpu.VMEM((tm, RG_WIDTH), F32),
            pltpu.VMEM((tm, RG_WIDTH), F32),
            pltpu.VMEM((3, SUBLANES, RG_WIDTH), F32),
            pltpu.VMEM((SUBLANES, S5_WIDTH), F32),
            pltpu.VMEM((tm, 2 * S5_LANES), F32),
            pltpu.VMEM((2, SUBLANES, S5_LANES), F32),
            pltpu.VMEM((d // LANES, tm, LANES), F32),
        ],
        compiler_params=_params("arbitrary"),
        name="odd_mixer",
    )(h.reshape(batch, seq, d), *weights)
    return out.reshape(tokens, d)


def kernel(x, p, ffn_a_norm, ffn_a_w1, ffn_a_w3, ffn_a_w2, mix_norm, ffn_b_norm, ffn_b_w1, ffn_b_w3, ffn_b_w2, ple_norm, ple_w_gate, ple_w_up, ev_w_in, mla_q_norm, mla_w_q_up, mla_kv_norm, mla_w_kv_up, gla_w_gate_up, gla_b_gate, gla_out_norm, ev_w_out, od_w_in, rg_conv_w, rg_conv_b, rg_w_a, rg_b_a, rg_w_i, rg_b_i, rg_lambda, s5_a_re, s5_a_im, s5_log_dt, s5_b_re, s5_b_im, s5_c_re, s5_c_im, s5_d, s5_w_glu, s5_b_glu, od_w_out, final_norm):
    batch, seq, d = x.shape
    depth = p.shape[0]
    h = x.reshape(batch * seq, d)
    cast = lambda w: w.astype(MXU_DTYPE)
    ffn_a = (ffn_a_norm, cast(ffn_a_w1), cast(ffn_a_w3), cast(ffn_a_w2))
    ffn_b = (ffn_b_norm, cast(ffn_b_w1), cast(ffn_b_w3), cast(ffn_b_w2))
    ple = (p, ple_norm, cast(ple_w_gate), cast(ple_w_up))
    for i in range(depth):
        j = i // 2
        h = _ffn(h, i, *ffn_a)
        if i % 2 == 0:
            q, k, v, lq, lk, lv, lg, lr = _even_proj(
                h, mix_norm[i], ev_w_in[j], mla_q_norm[j], mla_w_q_up[j], mla_kv_norm[j],
                mla_w_kv_up[j], gla_w_gate_up[j], gla_b_gate[j], batch=batch)
            o_attn = _attention(q, k, v, batch=batch)
            h = _gla_out(lq, lk, lv, lg, lr, o_attn, h, gla_out_norm[j], ev_w_out[j],
                         batch=batch)
        else:
            h = _odd_mixer(h, mix_norm[i], od_w_in[j], rg_conv_w[j], rg_conv_b[j], rg_w_a[j],
                           rg_b_a[j], rg_w_i[j], rg_b_i[j], rg_lambda[j], s5_a_re[j],
                           s5_a_im[j], s5_log_dt[j], s5_b_re[j], s5_b_im[j], s5_c_re[j],
                           s5_c_im[j], s5_d[j], s5_w_glu[j], s5_b_glu[j], od_w_out[j],
                           batch=batch)
        last = i == depth - 1
        h = _ffn(h, i, *ffn_b, ple=ple, final_norm=final_norm if last else None)
    return h.reshape(batch, seq, d)
```

```python
import functools
import math

import jax
import jax.numpy as jnp
from jax import lax
from jax.experimental import pallas as pl
from jax.experimental.pallas import tpu as pltpu

MXU_DTYPE = jnp.bfloat16
F32 = jnp.float32

EPS = 1e-6
PLE_DIM = 256
MLA_HEADS = 8
MLA_NOPE = 64
MLA_ROPE = 32
MLA_V = 64
MLA_Q_RANK = 384
MLA_KV_RANK = 256
ROPE_BASE = 10000.0
GLA_HEADS = 4
GLA_DK = 64
GLA_DV = 128
GLA_GATE_RANK = 16
GLA_TAU = 16.0
GLA_CHUNK = 64
RG_WIDTH = 512
RG_BLOCKS = 8
RG_CONV = 4
RG_C = 8.0
S5_GROUP = 16
S5_GROUPS = 32
S5_STATE = 64
S5_WIDTH = S5_GROUP * S5_GROUPS
S5_LANES = S5_GROUPS * S5_STATE

LANES = 128
SUBLANES = 8
MXU_TILE = 256
HEAD_PAD = LANES
VMEM_LIMIT = 56 * 1024 * 1024

FFN_ROWS = 512
FFN_CHUNKS = 2
PROJ_ROWS = 512
ATTN_Q = 512
GLA_ROWS = 512
ODD_ROWS = 512
SCAN_LANES = 512


def _rms(x, g):
    return x * lax.rsqrt(jnp.mean(x * x, axis=-1, keepdims=True) + EPS) * g


def _dot(a, b):
    return jnp.dot(a, b, preferred_element_type=F32)


def _dot_nt(a, b):
    return lax.dot_general(a, b, (((1,), (1,)), ((), ())), preferred_element_type=F32)


def _dot_tn(a, b):
    return lax.dot_general(a, b, (((0,), (0,)), ((), ())), preferred_element_type=F32)


def _sigmoid(x):
    return 1.0 / (1.0 + jnp.exp(-x))


def _gelu_tanh(x):
    return 0.5 * x * (1.0 + jnp.tanh(math.sqrt(2.0 / math.pi) * (x + 0.044715 * (x * x * x))))


def _log_sigmoid(x):
    return jnp.minimum(x, 0.0) - jnp.log1p(jnp.exp(-jnp.abs(x)))


def _resident(shape):
    return pl.BlockSpec(shape, lambda *_: (0,) * len(shape), pipeline_mode=pl.Buffered(1))


def _params(*semantics):
    return pltpu.CompilerParams(dimension_semantics=semantics, vmem_limit_bytes=VMEM_LIMIT)


def _ffn_kernel(*refs, ple, final):
    h_ref, g_ref, w1_ref, w3_ref, w2_ref = refs[:5]
    rest = list(refs[5:])
    o_ref = rest.pop()
    h = h_ref[...]
    xn = _rms(h, g_ref[...]).astype(MXU_DTYPE)
    tiles = w1_ref.shape[1] // MXU_TILE
    bounds = [MXU_TILE * ((tiles * c + FFN_CHUNKS - 1) // FFN_CHUNKS) for c in range(FFN_CHUNKS + 1)]
    y = None
    for lo, hi in zip(bounds[:-1], bounds[1:]):
        a = _dot(xn, w1_ref[:, lo:hi])
        b = _dot(xn, w3_ref[:, lo:hi])
        t = (a * _sigmoid(a) * b).astype(MXU_DTYPE)
        yc = _dot(t, w2_ref[lo:hi, :])
        y = yc if y is None else y + yc
    h = h + 0.5 * y
    if ple:
        p_ref, gp_ref, wg_ref, wu_ref = rest[:4]
        rest = rest[4:]
        gate = _sigmoid(_dot(_rms(h, gp_ref[...]).astype(MXU_DTYPE), wg_ref[...]))
        up = _dot(p_ref[...].astype(MXU_DTYPE), wu_ref[...])
        h = h + up * gate
    if final:
        h = _rms(h, rest[0][...])
    o_ref[...] = h


def _ffn(h, layer, norm, w1, w3, w2, *, ple=None, final_norm=None):
    tokens, d = h.shape
    tm = FFN_ROWS
    hspec = pl.BlockSpec((tm, d), lambda r: (r, 0))

    def of_layer(a):
        rest = a.shape[1:]
        return pl.BlockSpec((None,) + rest, lambda r: (layer,) + (0,) * len(rest),
                            pipeline_mode=pl.Buffered(1))

    gains = lambda g: g.reshape(g.shape[0], 1, d)
    args = [h, gains(norm), w1, w3, w2]
    specs = [hspec] + [of_layer(a) for a in args[1:]]
    if ple is not None:
        p, gp, wg, wu = ple
        p = p.reshape(p.shape[0], tokens, p.shape[-1])
        args += [p, gains(gp), wg, wu]
        specs += [pl.BlockSpec((None, tm, p.shape[-1]), lambda r: (layer, r, 0)),
                  of_layer(args[-3]), of_layer(wg), of_layer(wu)]
    if final_norm is not None:
        args.append(final_norm.reshape(1, d))
        specs.append(_resident((1, d)))
    return pl.pallas_call(
        functools.partial(_ffn_kernel, ple=ple is not None, final=final_norm is not None),
        grid=(tokens // tm,),
        in_specs=specs,
        out_specs=hspec,
        out_shape=jax.ShapeDtypeStruct((tokens, d), F32),
        compiler_params=_params("parallel"),
        name="ffn",
    )(*args)


_EV_COLS = (MLA_Q_RANK, MLA_KV_RANK, HEAD_PAD, GLA_HEADS * GLA_DK, GLA_HEADS * GLA_DK,
            GLA_HEADS * GLA_DV, GLA_HEADS * GLA_DV)
ROT_SHIFT = HEAD_PAD - MLA_ROPE


def _even_proj_kernel(h_ref, gm_ref, win_ref, gq_ref, wq_ref, gkv_ref, wk_ref, wv_ref,
                      wg_ref, bg_ref, cq_ref, sq_ref, ck_ref, sk_ref,
                      q_ref, k_ref, v_ref, lq_ref, lk_ref, lv_ref, lg_ref, lr_ref):
    xn = _rms(h_ref[...], gm_ref[...]).astype(MXU_DTYPE)
    y = _dot(xn, win_ref[...])
    offs = [0]
    for w in _EV_COLS:
        offs.append(offs[-1] + w)
    c_q, c_kv, kr, lq, lk, lv, lr = (
        y[:, offs[i]:offs[i + 1]] for i in range(len(_EV_COLS)))
    g_low = kr

    cqn = _rms(c_q, gq_ref[...]).astype(MXU_DTYPE)
    qf = _dot(cqn, wq_ref[...])
    ckn = _rms(c_kv, gkv_ref[...]).astype(MXU_DTYPE)
    kn = _dot(ckn, wk_ref[...])
    v_lane = lax.broadcasted_iota(jnp.int32, (1, MLA_HEADS * HEAD_PAD), 1) % HEAD_PAD
    v_ref[...] = (_dot(ckn, wv_ref[...])
                  + jnp.where(v_lane == MLA_V, 1.0, 0.0)).astype(v_ref.dtype)
    k_rope = kr * ck_ref[...] + pltpu.roll(kr, ROT_SHIFT, axis=1) * sk_ref[...]
    cq, sq = cq_ref[...], sq_ref[...]
    for hd in range(MLA_HEADS):
        sl = slice(hd * HEAD_PAD, (hd + 1) * HEAD_PAD)
        q_h = qf[:, sl]
        q_ref[:, sl] = (q_h * cq + pltpu.roll(q_h, ROT_SHIFT, axis=1) * sq).astype(q_ref.dtype)
        k_ref[:, sl] = (kn[:, sl] + k_rope).astype(k_ref.dtype)

    gate = _dot(g_low.astype(MXU_DTYPE), wg_ref[...]) + bg_ref[...]
    lg_ref[...] = _log_sigmoid(gate) * (1.0 / GLA_TAU)
    lq_ref[...] = lq
    lk_ref[...] = lk
    lv_ref[...] = lv.astype(lv_ref.dtype)
    lr_ref[...] = lr


def _pad_heads(w, heads, width):
    lead = w.shape[:-1]
    w = w.reshape(lead + (heads, width))
    w = jnp.pad(w, [(0, 0)] * len(lead) + [(0, 0), (0, HEAD_PAD - width)])
    return w.reshape(lead + (heads * HEAD_PAD,))


def _rot_half(w):
    half = MLA_ROPE // 2
    return jnp.concatenate([-w[..., half:], w[..., :half]], axis=-1)


def _head_slot(w_nope, w_rope):
    return jnp.concatenate([w_nope, w_rope, _rot_half(w_rope)], axis=-1)


def _even_proj(h, mix_norm, w_in, q_norm, w_q_up, kv_norm, w_kv_up, w_gate_up, b_gate, *,
               batch):
    tokens, d = h.shape
    seq = tokens // batch
    tm = PROJ_ROWS
    n_s = seq // tm
    splits = [MLA_Q_RANK, MLA_KV_RANK, MLA_ROPE, GLA_HEADS * GLA_DK, GLA_HEADS * GLA_DK,
              GLA_HEADS * GLA_DV, GLA_GATE_RANK, GLA_HEADS * GLA_DV]
    offs = [0]
    for s in splits:
        offs.append(offs[-1] + s)
    wc_q, wc_kv, w_kr, w_lq, w_lk, w_lv, w_gl, w_lr = (
        w_in[:, offs[i]:offs[i + 1]] for i in range(len(splits)))
    gl_lanes = jnp.pad(w_gl, [(0, 0), (0, MLA_NOPE - GLA_GATE_RANK)])
    win = jnp.concatenate([
        wc_q, wc_kv, _head_slot(gl_lanes, w_kr), w_lq, w_lk, w_lv, w_lr], axis=1)

    wq = w_q_up.reshape(MLA_Q_RANK, MLA_HEADS, MLA_NOPE + MLA_ROPE)
    wq_full = _head_slot(wq[..., :MLA_NOPE], wq[..., MLA_NOPE:]).reshape(MLA_Q_RANK, -1)
    wkv = w_kv_up.reshape(MLA_KV_RANK, MLA_HEADS, MLA_NOPE + MLA_V)
    wk = _pad_heads(wkv[..., :MLA_NOPE].reshape(MLA_KV_RANK, -1), MLA_HEADS, MLA_NOPE)
    wv = _pad_heads(wkv[..., MLA_NOPE:].reshape(MLA_KV_RANK, -1), MLA_HEADS, MLA_V)
    wg = jnp.pad(w_gate_up, [(0, HEAD_PAD - GLA_GATE_RANK), (0, 0)])
    bg = b_gate.reshape(1, -1)

    half = MLA_ROPE // 2
    inv = ROPE_BASE ** (-jnp.arange(half, dtype=F32) / half)
    ang = jnp.arange(seq).astype(F32)[:, None] * inv[None, :]
    cos2 = jnp.tile(jnp.cos(ang), (1, 2))
    sin2 = jnp.tile(jnp.sin(ang), (1, 2))
    scale = (MLA_NOPE + MLA_ROPE) ** -0.5 * math.log2(math.e)
    ones = jnp.ones((seq, MLA_NOPE), F32)
    tail = jnp.zeros((seq, HEAD_PAD - MLA_NOPE - MLA_ROPE), F32)
    cq = jnp.concatenate([ones, cos2, tail], axis=1) * scale
    sq = jnp.concatenate([0 * ones, sin2, tail], axis=1) * scale
    ck = jnp.concatenate([0 * ones, cos2, tail], axis=1)
    sk = jnp.concatenate([0 * ones, sin2, tail], axis=1)

    cast = lambda w: w.astype(MXU_DTYPE)
    row = lambda w: pl.BlockSpec((tm, w), lambda r: (r, 0))
    tab = pl.BlockSpec((tm, HEAD_PAD), lambda r: (r % n_s, 0))
    gl_w = GLA_HEADS * GLA_DK
    gv_w = GLA_HEADS * GLA_DV
    qk_w = MLA_HEADS * HEAD_PAD
    outs = [(qk_w, MXU_DTYPE), (qk_w, MXU_DTYPE), (qk_w, MXU_DTYPE),
            (gl_w, F32), (gl_w, F32), (gv_w, MXU_DTYPE), (gl_w, F32), (gv_w, F32)]
    weights = [mix_norm.reshape(1, d), cast(win), q_norm.reshape(1, -1), cast(wq_full),
               kv_norm.reshape(1, -1), cast(wk), cast(wv), cast(wg), bg]
    return pl.pallas_call(
        _even_proj_kernel,
        grid=(tokens // tm,),
        in_specs=[row(d)] + [_resident(w.shape) for w in weights] + [tab] * 4,
        out_specs=[row(w) for w, _ in outs],
        out_shape=[jax.ShapeDtypeStruct((tokens, w), dt) for w, dt in outs],
        compiler_params=_params("parallel"),
        name="even_proj",
    )(h, *weights, cq, sq, ck, sk)


def _attn_kernel(q_ref, k_ref, v_ref, o_ref, s_ref, m_ref, acc_ref, *, tq):
    n_q = q_ref.shape[0] // tq
    row = lax.broadcasted_iota(jnp.int32, (tq, tq), 0)
    col = lax.broadcasted_iota(jnp.int32, (tq, tq), 1)
    causal = col <= row
    sls = [slice(hd * HEAD_PAD, (hd + 1) * HEAD_PAD) for hd in range(2)]
    lane = lax.broadcasted_iota(jnp.int32, (tq, HEAD_PAD), 1)

    def tile(i):
        return pl.ds(pl.multiple_of(i * tq, tq), tq)

    def scores(hd, qi, j):
        return _dot_nt(q_ref[tile(qi), sls[hd]], k_ref[tile(j), sls[hd]])

    def softmax_pv(hd, s, j, masked):
        if masked:
            s = jnp.where(causal, s, -jnp.inf)
        m = m_ref[hd]
        m_new = jnp.maximum(m, jnp.max(s, axis=1, keepdims=True)).astype(MXU_DTYPE)
        alpha = jnp.exp2(m - m_new.astype(F32))
        p = jnp.exp2(s.astype(MXU_DTYPE) - jnp.concatenate([m_new] * (tq // LANES), axis=1))
        m_ref[hd] = m_new.astype(F32)
        acc_ref[hd] = alpha * acc_ref[hd] + _dot(p, v_ref[tile(j), sls[hd]])

    def step(qi, j, masked, next_qi, next_j):
        s_b = scores(1, qi, j)
        softmax_pv(0, s_ref[...], j, masked)
        s_ref[...] = scores(0, next_qi, next_j)
        softmax_pv(1, s_b, j, masked)

    s_ref[...] = scores(0, 0, 0)

    def q_tile(qi, _):
        m_ref[...] = jnp.full(m_ref.shape, -jnp.inf, F32)
        acc_ref[...] = jnp.zeros(acc_ref.shape, F32)

        def body(j, _):
            step(qi, j, False, qi, j + 1)
            return 0

        lax.fori_loop(0, qi, body, 0)
        step(qi, qi, True, jnp.minimum(qi + 1, n_q - 1), 0)
        o_a, o_b = (acc_ref[hd] / acc_ref[hd][:, MLA_V:MLA_V + 1] for hd in range(2))
        o_ref[tile(qi), :] = jnp.where(lane < MLA_V, o_a,
                                       pltpu.roll(o_b, MLA_V, axis=1)).astype(o_ref.dtype)
        return 0

    lax.fori_loop(0, n_q, q_tile, 0)


def _attention(q, k, v, *, batch):
    tokens = q.shape[0]
    seq = tokens // batch
    tq = ATTN_Q
    q3 = q.reshape(batch, seq, -1)
    k3 = k.reshape(batch, seq, -1)
    v3 = v.reshape(batch, seq, -1)
    pair = 2 * HEAD_PAD
    in_spec = pl.BlockSpec((None, seq, pair), lambda b, hp: (b, 0, hp))
    out = pl.pallas_call(
        functools.partial(_attn_kernel, tq=tq),
        grid=(batch, MLA_HEADS // 2),
        in_specs=[in_spec, in_spec, in_spec],
        out_specs=pl.BlockSpec((None, seq, 2 * MLA_V), lambda b, hp: (b, 0, hp)),
        out_shape=jax.ShapeDtypeStruct((batch, seq, MLA_HEADS * MLA_V), MXU_DTYPE),
        scratch_shapes=[pltpu.VMEM((tq, tq), F32), pltpu.VMEM((2, tq, LANES), F32),
                        pltpu.VMEM((2, tq, HEAD_PAD), F32)],
        compiler_params=_params("parallel", "parallel"),
        name="attention",
    )(q3, k3, v3)
    return out.reshape(tokens, -1)


def _gla_out_kernel(lq_ref, lk_ref, lv_ref, lg_ref, lr_ref, oa_ref, h_ref, gn_ref, wo_ref,
                    o_ref, state_ref, y_ref):
    @pl.when(pl.program_id(1) == 0)
    def _():
        state_ref[...] = jnp.zeros_like(state_ref)

    tt, width = lg_ref.shape
    chunk = GLA_CHUNK
    pos = lax.broadcasted_iota(jnp.int32, (tt, width), 0) & (chunk - 1)
    b = lg_ref[...]
    d = 1
    while d < chunk:
        b = b + jnp.where(pos >= d, pltpu.roll(b, d, axis=0), 0.0)
        d *= 2
    lk = lk_ref[...]
    parity = lambda rows: (lax.broadcasted_iota(jnp.int32, (rows, width), 1) // GLA_DK) % 2
    q_e = lq_ref[...] * (GLA_DK ** -0.5) * jnp.exp(b)
    q_par = [jnp.where(parity(tt) == par, q_e, 0.0).astype(MXU_DTYPE) for par in range(2)]
    k_e = (lk * jnp.exp(-b)).astype(MXU_DTYPE)
    tri = (lax.broadcasted_iota(jnp.int32, (chunk, chunk), 1)
           <= lax.broadcasted_iota(jnp.int32, (chunk, chunk), 0))
    gn = gn_ref[...]
    for c in range(tt // chunk):
        rows = slice(c * chunk, (c + 1) * chunk)
        b_c = b[rows]
        b_last = b_c[chunk - 1:chunk]
        k_end = lk[rows] * jnp.exp(b_last - b_c)
        k_end_par = [jnp.where(parity(chunk) == par, k_end, 0.0).astype(MXU_DTYPE)
                     for par in range(2)]
        decay = jnp.exp(b_last)
        for hd in range(GLA_HEADS):
            sl = slice((hd // 2) * LANES, (hd // 2 + 1) * LANES)
            vs = slice(hd * GLA_DV, (hd + 1) * GLA_DV)
            v_c = lv_ref[rows, vs]
            q_h = q_par[hd % 2][rows, sl]
            att = jnp.where(tri, _dot_nt(q_h, k_e[rows, sl]), 0.0)
            st = state_ref[hd]
            o = _dot(att.astype(MXU_DTYPE), v_c) + _dot_nt(q_h, st.astype(MXU_DTYPE))
            state_ref[hd] = st * decay[:, sl] + _dot_tn(v_c, k_end_par[hd % 2][:, sl])
            o = _rms(o, gn)
            r = lr_ref[rows, vs]
            y_ref[rows, vs] = (o * (r * _sigmoid(r))).astype(y_ref.dtype)
    n_a = oa_ref.shape[1]
    o_ref[...] = (h_ref[...] + _dot(oa_ref[...], wo_ref[:n_a, :])
                  + _dot(y_ref[...], wo_ref[n_a:, :]))


def _gla_out(lq, lk, lv, lg, lr, o_attn, h, out_norm, w_out, *, batch):
    tokens, d = h.shape
    seq = tokens // batch
    tt = GLA_ROWS
    n_s = seq // tt
    row = lambda w: pl.BlockSpec((tt, w), lambda b, i: (b * n_s + i, 0))
    ins = [lq, lk, lv, lg, lr, o_attn, h]
    return pl.pallas_call(
        _gla_out_kernel,
        grid=(batch, n_s),
        in_specs=[row(a.shape[1]) for a in ins] + [_resident((1, GLA_DV)),
                                                   _resident(w_out.shape)],
        out_specs=row(d),
        out_shape=jax.ShapeDtypeStruct((tokens, d), F32),
        scratch_shapes=[pltpu.VMEM((GLA_HEADS, GLA_DV, HEAD_PAD), F32),
                        pltpu.VMEM((tt, GLA_HEADS * GLA_DV), MXU_DTYPE)],
        compiler_params=_params("parallel", "arbitrary"),
        name="gla_out",
    )(*ins, out_norm.reshape(1, -1), w_out.astype(MXU_DTYPE))


def _interleave(src_ref, slab_ref):
    batch, steps, d = src_ref.shape
    for s in range(d // LANES):
        for b in range(batch):
            slab_ref[s, pl.ds(b, steps, stride=batch), :] = src_ref[b, :, s * LANES:(s + 1) * LANES]
    return jnp.concatenate([slab_ref[s] for s in range(d // LANES)], axis=1)


def _deinterleave(x, slab_ref, dst_ref):
    batch, steps, d = dst_ref.shape
    for s in range(d // LANES):
        slab_ref[s] = x[:, s * LANES:(s + 1) * LANES]
    for s in range(d // LANES):
        for b in range(batch):
            dst_ref[b, :, s * LANES:(s + 1) * LANES] = slab_ref[s, pl.ds(b, steps, stride=batch), :]

def _prev_step(cur, prev_rolled, sub):
    cur_rolled = pltpu.roll(cur, SUBLANES // 2, axis=0)
    return jnp.where(sub < SUBLANES // 2, prev_rolled, cur_rolled), cur_rolled


def _odd_kernel(h_ref, gm_ref, win_ref, cw_ref, cb_ref, wgate_ref, bgate_ref, lam_ref,
                sa_ref, bw_ref, cwt_ref, sd_ref, wglu_ref, bglu_ref, wo_ref,
                o_ref,
                halo_ref, rg_a_ref, rg_b_ref, rg_x_ref, rg_carry_ref,
                uh_ref, cx_ref, s5_carry_ref, slab_ref):
    first = pl.program_id(0) == 0

    @pl.when(first)
    def _():
        halo_ref[...] = jnp.zeros_like(halo_ref)
        rg_carry_ref[...] = jnp.zeros_like(rg_carry_ref)
        s5_carry_ref[...] = jnp.zeros_like(s5_carry_ref)
        uh_ref[...] = jnp.zeros_like(uh_ref)

    tm = h_ref.shape[0] * h_ref.shape[1]
    groups = tm // SUBLANES
    h = _interleave(h_ref, slab_ref)
    xn = _rms(h, gm_ref[...]).astype(MXU_DTYPE)
    y = _dot(xn, win_ref[...])
    x_gate, x_rg, u = (y[:, i * RG_WIDTH:(i + 1) * RG_WIDTH] for i in range(3))

    step = SUBLANES // 2

    halo_rows = halo_ref.shape[0]
    ext = jnp.concatenate([halo_ref[...], x_rg], axis=0)
    halo_ref[...] = x_rg[tm - halo_rows:, :]
    cw = cw_ref[...]
    xc = cb_ref[...] + x_rg * cw[RG_CONV - 1:RG_CONV]
    for j in range(RG_CONV - 1):
        back = (RG_CONV - 1 - j) * step
        xc = xc + ext[halo_rows - back:halo_rows - back + tm, :] * cw[j:j + 1]

    gate_pre = _dot(xc.astype(MXU_DTYPE), wgate_ref[...]) + bgate_ref[...]

    ub = u.astype(MXU_DTYPE)
    half_u = S5_WIDTH // 2
    half_x = S5_LANES // 2
    u_halo = uh_ref.shape[0]
    u_ext = jnp.concatenate([uh_ref[...], u], axis=0)
    uh_ref[...] = u[tm - u_halo:, :]
    ub_prev = u_ext[u_halo - step:u_halo - step + tm, :].astype(MXU_DTYPE)
    for hf in range(2):
        cols = slice(hf * half_u, (hf + 1) * half_u)
        cx_ref[:, 2 * hf * half_x:2 * (hf + 1) * half_x] = _dot(
            jnp.concatenate([ub[:, cols], ub_prev[:, cols]], axis=1), bw_ref[hf])

    gates = _sigmoid(gate_pre)
    r_gate, i_gate = gates[:, :RG_WIDTH], gates[:, RG_WIDTH:]
    lam = lam_ref[...]
    softplus_neg = jnp.maximum(-lam, 0.0) + jnp.log1p(jnp.exp(-jnp.abs(lam)))
    log_a = (-RG_C * softplus_neg) * r_gate
    a = jnp.exp(log_a)
    rg_a_ref[...] = a
    var = -jnp.tanh(log_a) * (a * a + 1.0)
    rg_b_ref[...] = jnp.where(var > 0.0, var * lax.rsqrt(var), 0.0) * (i_gate * xc)

    sub = lax.broadcasted_iota(jnp.int32, (SUBLANES, RG_WIDTH), 0)

    x, ra_prev, rb_prev = rg_carry_ref[0], rg_carry_ref[1], rg_carry_ref[2]
    for g in range(groups):
        rows = slice(g * SUBLANES, (g + 1) * SUBLANES)
        a = rg_a_ref[rows, :]
        bx = rg_b_ref[rows, :]
        a_prev, ra_prev = _prev_step(a, ra_prev, sub)
        b_prev, rb_prev = _prev_step(bx, rb_prev, sub)
        x = (a * a_prev) * x + (a * b_prev + bx)
        rg_x_ref[rows, :] = x
    for i, v in enumerate((x, ra_prev, rb_prev)):
        rg_carry_ref[i] = v
    y_c = rg_x_ref[...] * _gelu_tanh(x_gate)

    for lc in range(S5_LANES // SCAN_LANES):
        hf, within = divmod(lc * SCAN_LANES, half_x)
        coef = slice(lc * SCAN_LANES, (lc + 1) * SCAN_LANES)
        re = slice(2 * hf * half_x + within, 2 * hf * half_x + within + SCAN_LANES)
        im = slice(re.start + half_x, re.stop + half_x)
        a2_re, a2_im = sa_ref[0, :, coef], sa_ref[1, :, coef]

        x_re, x_im = s5_carry_ref[0, :, coef], s5_carry_ref[1, :, coef]
        for g in range(groups):
            rows = slice(g * SUBLANES, (g + 1) * SUBLANES)
            x_re, x_im = (a2_re * x_re - a2_im * x_im + cx_ref[rows, re],
                          a2_re * x_im + a2_im * x_re + cx_ref[rows, im])
            cx_ref[rows, re] = x_re
            cx_ref[rows, im] = x_im
        s5_carry_ref[0, :, coef] = x_re
        s5_carry_ref[1, :, coef] = x_im

    ys = jnp.concatenate(
        [_dot(cx_ref[:, 2 * hf * half_x:2 * (hf + 1) * half_x].astype(MXU_DTYPE), cwt_ref[hf])
         for hf in range(2)], axis=1) + sd_ref[...] * u
    ys = _gelu_tanh(ys)
    y_d = ys * _sigmoid(_dot(ys.astype(MXU_DTYPE), wglu_ref[...]) + bglu_ref[...])
    out = (h + _dot(y_c.astype(MXU_DTYPE), wo_ref[:RG_WIDTH, :])
           + _dot(y_d.astype(MXU_DTYPE), wo_ref[RG_WIDTH:, :]))
    _deinterleave(out, slab_ref, o_ref)


def _block_diag(blocks):
    n, r, c = blocks.shape
    eye = jnp.eye(n, dtype=blocks.dtype)
    return (eye[:, None, :, None] * blocks[:, :, None, :]).reshape(n * r, n * c)


def _odd_mixer(h, mix_norm, w_in, conv_w, conv_b, w_a, b_a, w_i, b_i, lam,
               a_re, a_im, log_dt, b_re, b_im, c_re, c_im, d_skip, w_glu, b_glu, w_out, *,
               batch):
    assert batch == SUBLANES // 2
    tokens, d = h.shape
    seq = tokens // batch
    tm = ODD_ROWS
    steps = tm // batch

    dt = jnp.exp(log_dt.astype(F32))[:, None]
    lr, li = a_re.astype(F32), a_im.astype(F32)
    mag = jnp.exp(lr * dt)
    ab_re = mag * jnp.cos(li * dt)
    ab_im = mag * jnp.sin(li * dt)
    den = lr * lr + li * li
    nr, ni = ab_re - 1.0, ab_im
    coef_re = (nr * lr + ni * li) / den
    coef_im = (ni * lr - nr * li) / den
    bb_re = coef_re[..., None] * b_re - coef_im[..., None] * b_im
    bb_im = coef_re[..., None] * b_im + coef_im[..., None] * b_re
    abb_re = ab_re[..., None] * bb_re - ab_im[..., None] * bb_im
    abb_im = ab_re[..., None] * bb_im + ab_im[..., None] * bb_re
    halves = [slice(0, S5_GROUPS // 2), slice(S5_GROUPS // 2, S5_GROUPS)]

    def in_map(m_re, m_im, g):
        return jnp.concatenate([_block_diag(m_re[g].transpose(0, 2, 1)),
                                _block_diag(m_im[g].transpose(0, 2, 1))], axis=1)

    bw = jnp.stack([jnp.concatenate([in_map(bb_re, bb_im, g), in_map(abb_re, abb_im, g)],
                                    axis=0) for g in halves])
    cwt = jnp.stack([jnp.concatenate([_block_diag(c_re[g].transpose(0, 2, 1)),
                                      -_block_diag(c_im[g].transpose(0, 2, 1))], axis=0)
                     for g in halves])
    a2_re = ab_re * ab_re - ab_im * ab_im
    a2_im = 2.0 * ab_re * ab_im
    sa = jnp.stack([jnp.broadcast_to(v.reshape(1, -1), (SUBLANES, S5_LANES))
                    for v in (a2_re, a2_im)])
    wgate = jnp.concatenate([_block_diag(w_a), _block_diag(w_i)], axis=1)
    bgate = jnp.concatenate([b_a, b_i]).reshape(1, -1)

    cast = lambda w: w.astype(MXU_DTYPE)
    weights = [mix_norm.reshape(1, d), cast(w_in), conv_w, conv_b.reshape(1, -1), cast(wgate),
               bgate, lam.reshape(1, -1), sa, cast(bw), cast(cwt), d_skip.reshape(1, -1),
               cast(w_glu), b_glu.reshape(1, -1), cast(w_out)]
    row = pl.BlockSpec((batch, steps, d), lambda r: (0, r, 0))
    out = pl.pallas_call(
        _odd_kernel,
        grid=(seq // steps,),
        in_specs=[row] + [_resident(w.shape) for w in weights],
        out_specs=row,
        out_shape=jax.ShapeDtypeStruct((batch, seq, d), F32),
        scratch_shapes=[
            pltpu.VMEM((2 * SUBLANES, RG_WIDTH), F32),
            pltpu.VMEM((tm, RG_WIDTH), F32),
            pltpu.VMEM((tm, RG_WIDTH), F32),
            pltpu.VMEM((tm, RG_WIDTH), F32),
            pltpu.VMEM((3, SUBLANES, RG_WIDTH), F32),
            pltpu.VMEM((SUBLANES, S5_WIDTH), F32),
            pltpu.VMEM((tm, 2 * S5_LANES), F32),
            pltpu.VMEM((2, SUBLANES, S5_LANES), F32),
            pltpu.VMEM((d // LANES, tm, LANES), F32),
        ],
        compiler_params=_params("arbitrary"),
        name="odd_mixer",
    )(h.reshape(batch, seq, d), *weights)
    return out.reshape(tokens, d)


def kernel(x, p, ffn_a_norm, ffn_a_w1, ffn_a_w3, ffn_a_w2, mix_norm, ffn_b_norm, ffn_b_w1, ffn_b_w3, ffn_b_w2, ple_norm, ple_w_gate, ple_w_up, ev_w_in, mla_q_norm, mla_w_q_up, mla_kv_norm, mla_w_kv_up, gla_w_gate_up, gla_b_gate, gla_out_norm, ev_w_out, od_w_in, rg_conv_w, rg_conv_b, rg_w_a, rg_b_a, rg_w_i, rg_b_i, rg_lambda, s5_a_re, s5_a_im, s5_log_dt, s5_b_re, s5_b_im, s5_c_re, s5_c_im, s5_d, s5_w_glu, s5_b_glu, od_w_out, final_norm):
    batch, seq, d = x.shape
    depth = p.shape[0]
    h = x.reshape(batch * seq, d)
    cast = lambda w: w.astype(MXU_DTYPE)
    ffn_a = (ffn_a_norm, cast(ffn_a_w1), cast(ffn_a_w3), cast(ffn_a_w2))
    ffn_b = (ffn_b_norm, cast(ffn_b_w1), cast(ffn_b_w3), cast(ffn_b_w2))
    ple = (p, ple_norm, cast(ple_w_gate), cast(ple_w_up))
    for i in range(depth):
        j = i // 2
        h = _ffn(h, i, *ffn_a)
        if i % 2 == 0:
            q, k, v, lq, lk, lv, lg, lr = _even_proj(
                h, mix_norm[i], ev_w_in[j], mla_q_norm[j], mla_w_q_up[j], mla_kv_norm[j],
                mla_w_kv_up[j], gla_w_gate_up[j], gla_b_gate[j], batch=batch)
            o_attn = _attention(q, k, v, batch=batch)
            h = _gla_out(lq, lk, lv, lg, lr, o_attn, h, gla_out_norm[j], ev_w_out[j],
                         batch=batch)
        else:
            h = _odd_mixer(h, mix_norm[i], od_w_in[j], rg_conv_w[j], rg_conv_b[j], rg_w_a[j],
                           rg_b_a[j], rg_w_i[j], rg_b_i[j], rg_lambda[j], s5_a_re[j],
                           s5_a_im[j], s5_log_dt[j], s5_b_re[j], s5_b_im[j], s5_c_re[j],
                           s5_c_im[j], s5_d[j], s5_w_glu[j], s5_b_glu[j], od_w_out[j],
                           batch=batch)
        last = i == depth - 1
        h = _ffn(h, i, *ffn_b, ple=ple, final_norm=final_norm if last else None)
    return h.reshape(batch, seq, d)
```

```python
import functools
import math

import jax
import jax.numpy as jnp
from jax import lax
from jax.experimental import pallas as pl
from jax.experimental.pallas import tpu as pltpu

MXU_DTYPE = jnp.bfloat16
F32 = jnp.float32

EPS = 1e-6
PLE_DIM = 256
MLA_HEADS = 8
MLA_NOPE = 64
MLA_ROPE = 32
MLA_V = 64
MLA_Q_RANK = 384
MLA_KV_RANK = 256
ROPE_BASE = 10000.0
GLA_HEADS = 4
GLA_DK = 64
GLA_DV = 128
GLA_GATE_RANK = 16
GLA_TAU = 16.0
GLA_CHUNK = 64
RG_WIDTH = 512
RG_BLOCKS = 8
RG_CONV = 4
RG_C = 8.0
S5_GROUP = 16
S5_GROUPS = 32
S5_STATE = 64
S5_WIDTH = S5_GROUP * S5_GROUPS
S5_LANES = S5_GROUPS * S5_STATE

LANES = 128
SUBLANES = 8
MXU_TILE = 256
HEAD_PAD = LANES
VMEM_LIMIT = 56 * 1024 * 1024

FFN_ROWS = 512
FFN_CHUNKS = 2
PROJ_ROWS = 512
ATTN_Q = 512
ATTN_HEADS = 4
GLA_ROWS = 512
ODD_ROWS = 512
SCAN_LANES = 512


def _rms(x, g):
    return x * lax.rsqrt(jnp.mean(x * x, axis=-1, keepdims=True) + EPS) * g


def _dot(a, b):
    return jnp.dot(a, b, preferred_element_type=F32)


def _dot_nt(a, b):
    return lax.dot_general(a, b, (((1,), (1,)), ((), ())), preferred_element_type=F32)


def _dot_tn(a, b):
    return lax.dot_general(a, b, (((0,), (0,)), ((), ())), preferred_element_type=F32)


def _sigmoid(x):
    return 1.0 / (1.0 + jnp.exp(-x))


def _gelu_tanh(x):
    return 0.5 * x * (1.0 + jnp.tanh(math.sqrt(2.0 / math.pi) * (x + 0.044715 * (x * x * x))))


def _log_sigmoid(x):
    return jnp.minimum(x, 0.0) - jnp.log1p(jnp.exp(-jnp.abs(x)))


def _resident(shape):
    return pl.BlockSpec(shape, lambda *_: (0,) * len(shape), pipeline_mode=pl.Buffered(1))


def _params(*semantics):
    return pltpu.CompilerParams(dimension_semantics=semantics, vmem_limit_bytes=VMEM_LIMIT)


def _ffn_kernel(*refs, ple, final):
    h_ref, g_ref, w1_ref, w3_ref, w2_ref = refs[:5]
    rest = list(refs[5:])
    o_ref = rest.pop()
    h = h_ref[...]
    xn = _rms(h, g_ref[...]).astype(MXU_DTYPE)
    tiles = w1_ref.shape[1] // MXU_TILE
    bounds = [MXU_TILE * ((tiles * c + FFN_CHUNKS - 1) // FFN_CHUNKS) for c in range(FFN_CHUNKS + 1)]
    y = None
    for lo, hi in zip(bounds[:-1], bounds[1:]):
        a = _dot(xn, w1_ref[:, lo:hi])
        b = _dot(xn, w3_ref[:, lo:hi])
        t = (a * _sigmoid(a) * b).astype(MXU_DTYPE)
        yc = _dot(t, w2_ref[lo:hi, :])
        y = yc if y is None else y + yc
    h = h + 0.5 * y
    if ple:
        p_ref, gp_ref, wg_ref, wu_ref = rest[:4]
        rest = rest[4:]
        gate = _sigmoid(_dot(_rms(h, gp_ref[...]).astype(MXU_DTYPE), wg_ref[...]))
        up = _dot(p_ref[...].astype(MXU_DTYPE), wu_ref[...])
        h = h + up * gate
    if final:
        h = _rms(h, rest[0][...])
    o_ref[...] = h


def _ffn(h, layer, norm, w1, w3, w2, *, ple=None, final_norm=None):
    tokens, d = h.shape
    tm = FFN_ROWS
    hspec = pl.BlockSpec((tm, d), lambda r: (r, 0))

    def of_layer(a):
        rest = a.shape[1:]
        return pl.BlockSpec((None,) + rest, lambda r: (layer,) + (0,) * len(rest),
                            pipeline_mode=pl.Buffered(1))

    gains = lambda g: g.reshape(g.shape[0], 1, d)
    args = [h, gains(norm), w1, w3, w2]
    specs = [hspec] + [of_layer(a) for a in args[1:]]
    if ple is not None:
        p, gp, wg, wu = ple
        p = p.reshape(p.shape[0], tokens, p.shape[-1])
        args += [p, gains(gp), wg, wu]
        specs += [pl.BlockSpec((None, tm, p.shape[-1]), lambda r: (layer, r, 0)),
                  of_layer(args[-3]), of_layer(wg), of_layer(wu)]
    if final_norm is not None:
        args.append(final_norm.reshape(1, d))
        specs.append(_resident((1, d)))
    return pl.pallas_call(
        functools.partial(_ffn_kernel, ple=ple is not None, final=final_norm is not None),
        grid=(tokens // tm,),
        in_specs=specs,
        out_specs=hspec,
        out_shape=jax.ShapeDtypeStruct((tokens, d), F32),
        compiler_params=_params("parallel"),
        name="ffn",
    )(*args)


_EV_COLS = (MLA_Q_RANK, MLA_KV_RANK, HEAD_PAD, GLA_HEADS * GLA_DK, GLA_HEADS * GLA_DK,
            GLA_HEADS * GLA_DV, GLA_HEADS * GLA_DV)
ROT_SHIFT = HEAD_PAD - MLA_ROPE


def _even_proj_kernel(h_ref, gm_ref, win_ref, gq_ref, wq_ref, gkv_ref, wk_ref, wv_ref,
                      wg_ref, bg_ref, cq_ref, sq_ref, ck_ref, sk_ref,
                      q_ref, k_ref, v_ref, lq_ref, lk_ref, lv_ref, lg_ref, lr_ref):
    xn = _rms(h_ref[...], gm_ref[...]).astype(MXU_DTYPE)
    y = _dot(xn, win_ref[...])
    offs = [0]
    for w in _EV_COLS:
        offs.append(offs[-1] + w)
    c_q, c_kv, kr, lq, lk, lv, lr = (
        y[:, offs[i]:offs[i + 1]] for i in range(len(_EV_COLS)))
    g_low = kr

    cqn = _rms(c_q, gq_ref[...]).astype(MXU_DTYPE)
    qf = _dot(cqn, wq_ref[...])
    ckn = _rms(c_kv, gkv_ref[...]).astype(MXU_DTYPE)
    kn = _dot(ckn, wk_ref[...])
    v_lane = lax.broadcasted_iota(jnp.int32, (1, MLA_HEADS * HEAD_PAD), 1) % HEAD_PAD
    v_ref[...] = (_dot(ckn, wv_ref[...])
                  + jnp.where(v_lane == MLA_V, 1.0, 0.0)).astype(v_ref.dtype)
    k_rope = kr * ck_ref[...] + pltpu.roll(kr, ROT_SHIFT, axis=1) * sk_ref[...]
    cq, sq = cq_ref[...], sq_ref[...]
    for hd in range(MLA_HEADS):
        sl = slice(hd * HEAD_PAD, (hd + 1) * HEAD_PAD)
        q_h = qf[:, sl]
        q_ref[:, sl] = (q_h * cq + pltpu.roll(q_h, ROT_SHIFT, axis=1) * sq).astype(q_ref.dtype)
        k_ref[:, sl] = (kn[:, sl] + k_rope).astype(k_ref.dtype)

    gate = _dot(g_low.astype(MXU_DTYPE), wg_ref[...]) + bg_ref[...]
    lg_ref[...] = _log_sigmoid(gate) * (1.0 / GLA_TAU)
    lq_ref[...] = lq
    lk_ref[...] = lk
    lv_ref[...] = lv.astype(lv_ref.dtype)
    lr_ref[...] = lr


def _pad_heads(w, heads, width):
    lead = w.shape[:-1]
    w = w.reshape(lead + (heads, width))
    w = jnp.pad(w, [(0, 0)] * len(lead) + [(0, 0), (0, HEAD_PAD - width)])
    return w.reshape(lead + (heads * HEAD_PAD,))


def _rot_half(w):
    half = MLA_ROPE // 2
    return jnp.concatenate([-w[..., half:], w[..., :half]], axis=-1)


def _head_slot(w_nope, w_rope):
    return jnp.concatenate([w_nope, w_rope, _rot_half(w_rope)], axis=-1)


def _even_proj(h, mix_norm, w_in, q_norm, w_q_up, kv_norm, w_kv_up, w_gate_up, b_gate, *,
               batch):
    tokens, d = h.shape
    seq = tokens // batch
    tm = PROJ_ROWS
    n_s = seq // tm
    splits = [MLA_Q_RANK, MLA_KV_RANK, MLA_ROPE, GLA_HEADS * GLA_DK, GLA_HEADS * GLA_DK,
              GLA_HEADS * GLA_DV, GLA_GATE_RANK, GLA_HEADS * GLA_DV]
    offs = [0]
    for s in splits:
        offs.append(offs[-1] + s)
    wc_q, wc_kv, w_kr, w_lq, w_lk, w_lv, w_gl, w_lr = (
        w_in[:, offs[i]:offs[i + 1]] for i in range(len(splits)))
    gl_lanes = jnp.pad(w_gl, [(0, 0), (0, MLA_NOPE - GLA_GATE_RANK)])
    win = jnp.concatenate([
        wc_q, wc_kv, _head_slot(gl_lanes, w_kr), w_lq, w_lk, w_lv, w_lr], axis=1)

    wq = w_q_up.reshape(MLA_Q_RANK, MLA_HEADS, MLA_NOPE + MLA_ROPE)
    wq_full = _head_slot(wq[..., :MLA_NOPE], wq[..., MLA_NOPE:]).reshape(MLA_Q_RANK, -1)
    wkv = w_kv_up.reshape(MLA_KV_RANK, MLA_HEADS, MLA_NOPE + MLA_V)
    wk = _pad_heads(wkv[..., :MLA_NOPE].reshape(MLA_KV_RANK, -1), MLA_HEADS, MLA_NOPE)
    wv = _pad_heads(wkv[..., MLA_NOPE:].reshape(MLA_KV_RANK, -1), MLA_HEADS, MLA_V)
    wg = jnp.pad(w_gate_up, [(0, HEAD_PAD - GLA_GATE_RANK), (0, 0)])
    bg = b_gate.reshape(1, -1)

    half = MLA_ROPE // 2
    inv = ROPE_BASE ** (-jnp.arange(half, dtype=F32) / half)
    ang = jnp.arange(seq).astype(F32)[:, None] * inv[None, :]
    cos2 = jnp.tile(jnp.cos(ang), (1, 2))
    sin2 = jnp.tile(jnp.sin(ang), (1, 2))
    scale = (MLA_NOPE + MLA_ROPE) ** -0.5 * math.log2(math.e)
    ones = jnp.ones((seq, MLA_NOPE), F32)
    tail = jnp.zeros((seq, HEAD_PAD - MLA_NOPE - MLA_ROPE), F32)
    cq = jnp.concatenate([ones, cos2, tail], axis=1) * scale
    sq = jnp.concatenate([0 * ones, sin2, tail], axis=1) * scale
    ck = jnp.concatenate([0 * ones, cos2, tail], axis=1)
    sk = jnp.concatenate([0 * ones, sin2, tail], axis=1)

    cast = lambda w: w.astype(MXU_DTYPE)
    row = lambda w: pl.BlockSpec((tm, w), lambda r: (r, 0))
    tab = pl.BlockSpec((tm, HEAD_PAD), lambda r: (r % n_s, 0))
    gl_w = GLA_HEADS * GLA_DK
    gv_w = GLA_HEADS * GLA_DV
    qk_w = MLA_HEADS * HEAD_PAD
    outs = [(qk_w, MXU_DTYPE), (qk_w, MXU_DTYPE), (qk_w, MXU_DTYPE),
            (gl_w, F32), (gl_w, F32), (gv_w, MXU_DTYPE), (gl_w, F32), (gv_w, F32)]
    weights = [mix_norm.reshape(1, d), cast(win), q_norm.reshape(1, -1), cast(wq_full),
               kv_norm.reshape(1, -1), cast(wk), cast(wv), cast(wg), bg]
    return pl.pallas_call(
        _even_proj_kernel,
        grid=(tokens // tm,),
        in_specs=[row(d)] + [_resident(w.shape) for w in weights] + [tab] * 4,
        out_specs=[row(w) for w, _ in outs],
        out_shape=[jax.ShapeDtypeStruct((tokens, w), dt) for w, dt in outs],
        compiler_params=_params("parallel"),
        name="even_proj",
    )(h, *weights, cq, sq, ck, sk)


def _attn_kernel(q_ref, k_ref, v_ref, o_ref, s_ref, m_ref, acc_ref, *, tq):
    n_q = q_ref.shape[0] // tq
    heads = m_ref.shape[0]
    row = lax.broadcasted_iota(jnp.int32, (tq, tq), 0)
    col = lax.broadcasted_iota(jnp.int32, (tq, tq), 1)
    causal = col <= row
    sls = [slice(hd * HEAD_PAD, (hd + 1) * HEAD_PAD) for hd in range(heads)]
    lane = lax.broadcasted_iota(jnp.int32, (tq, HEAD_PAD), 1)

    def tile(i):
        return pl.ds(pl.multiple_of(i * tq, tq), tq)

    def scores(hd, qi, j):
        return _dot_nt(q_ref[tile(qi), sls[hd]], k_ref[tile(j), sls[hd]])

    def softmax_pv(hd, s, j, masked):
        if masked:
            s = jnp.where(causal, s, -jnp.inf)
        m = m_ref[hd]
        m_new = jnp.maximum(m, jnp.max(s, axis=1, keepdims=True)).astype(MXU_DTYPE)
        alpha = jnp.exp2(m - m_new.astype(F32))
        p = jnp.exp2(s.astype(MXU_DTYPE) - jnp.concatenate([m_new] * (tq // LANES), axis=1))
        m_ref[hd] = m_new.astype(F32)
        acc_ref[hd] = alpha * acc_ref[hd] + _dot(p, v_ref[tile(j), sls[hd]])

    def step(qi, j, masked, next_qi, next_j):
        s_cur = s_ref[...]
        for hd in range(heads):
            s_next = scores(hd + 1, qi, j) if hd + 1 < heads else scores(0, next_qi, next_j)
            softmax_pv(hd, s_cur, j, masked)
            s_cur = s_next
        s_ref[...] = s_cur

    s_ref[...] = scores(0, 0, 0)

    def q_tile(qi, _):
        m_ref[...] = jnp.full(m_ref.shape, -jnp.inf, F32)
        acc_ref[...] = jnp.zeros(acc_ref.shape, F32)

        def body(j, _):
            step(qi, j, False, qi, j + 1)
            return 0

        lax.fori_loop(0, qi, body, 0)
        step(qi, qi, True, jnp.minimum(qi + 1, n_q - 1), 0)
        for pair in range(heads // 2):
            o_a, o_b = (acc_ref[hd] / acc_ref[hd][:, MLA_V:MLA_V + 1]
                        for hd in (2 * pair, 2 * pair + 1))
            o_ref[tile(qi), pair * LANES:(pair + 1) * LANES] = jnp.where(
                lane < MLA_V, o_a, pltpu.roll(o_b, MLA_V, axis=1)).astype(o_ref.dtype)
        return 0

    lax.fori_loop(0, n_q, q_tile, 0)


def _attention(q, k, v, *, batch):
    tokens = q.shape[0]
    seq = tokens // batch
    tq = ATTN_Q
    q3 = q.reshape(batch, seq, -1)
    k3 = k.reshape(batch, seq, -1)
    v3 = v.reshape(batch, seq, -1)
    nh = ATTN_HEADS
    group = lambda w, **kw: pl.BlockSpec((None, seq, nh * w), lambda b, g: (b, 0, g), **kw)
    once = dict(pipeline_mode=pl.Buffered(1))
    out = pl.pallas_call(
        functools.partial(_attn_kernel, tq=tq),
        grid=(batch, MLA_HEADS // nh),
        in_specs=[group(HEAD_PAD, **once), group(HEAD_PAD), group(HEAD_PAD)],
        out_specs=group(MLA_V, **once),
        out_shape=jax.ShapeDtypeStruct((batch, seq, MLA_HEADS * MLA_V), MXU_DTYPE),
        scratch_shapes=[pltpu.VMEM((tq, tq), F32), pltpu.VMEM((nh, tq, LANES), F32),
                        pltpu.VMEM((nh, tq, HEAD_PAD), F32)],
        compiler_params=_params("parallel", "parallel"),
        name="attention",
    )(q3, k3, v3)
    return out.reshape(tokens, -1)


def _gla_out_kernel(lq_ref, lk_ref, lv_ref, lg_ref, lr_ref, oa_ref, h_ref, gn_ref, wo_ref,
                    o_ref, state_ref, y_ref):
    @pl.when(pl.program_id(1) == 0)
    def _():
        state_ref[...] = jnp.zeros_like(state_ref)

    tt, width = lg_ref.shape
    chunk = GLA_CHUNK
    pos = lax.broadcasted_iota(jnp.int32, (tt, width), 0) & (chunk - 1)
    b = lg_ref[...]
    d = 1
    while d < chunk:
        b = b + jnp.where(pos >= d, pltpu.roll(b, d, axis=0), 0.0)
        d *= 2
    lk = lk_ref[...]
    parity = lambda rows: (lax.broadcasted_iota(jnp.int32, (rows, width), 1) // GLA_DK) % 2
    q_e = lq_ref[...] * (GLA_DK ** -0.5) * jnp.exp(b)
    q_par = [jnp.where(parity(tt) == par, q_e, 0.0).astype(MXU_DTYPE) for par in range(2)]
    k_e = (lk * jnp.exp(-b)).astype(MXU_DTYPE)
    tri = (lax.broadcasted_iota(jnp.int32, (chunk, chunk), 1)
           <= lax.broadcasted_iota(jnp.int32, (chunk, chunk), 0))
    gn = gn_ref[...]
    for c in range(tt // chunk):
        rows = slice(c * chunk, (c + 1) * chunk)
        b_c = b[rows]
        b_last = b_c[chunk - 1:chunk]
        k_end = lk[rows] * jnp.exp(b_last - b_c)
        k_end_par = [jnp.where(parity(chunk) == par, k_end, 0.0).astype(MXU_DTYPE)
                     for par in range(2)]
        decay = jnp.exp(b_last)
        for hd in range(GLA_HEADS):
            sl = slice((hd // 2) * LANES, (hd // 2 + 1) * LANES)
            vs = slice(hd * GLA_DV, (hd + 1) * GLA_DV)
            v_c = lv_ref[rows, vs]
            q_h = q_par[hd % 2][rows, sl]
            att = jnp.where(tri, _dot_nt(q_h, k_e[rows, sl]), 0.0)
            st = state_ref[hd]
            o = _dot(att.astype(MXU_DTYPE), v_c) + _dot_nt(q_h, st.astype(MXU_DTYPE))
            state_ref[hd] = st * decay[:, sl] + _dot_tn(v_c, k_end_par[hd % 2][:, sl])
            o = _rms(o, gn)
            r = lr_ref[rows, vs]
            y_ref[rows, vs] = (o * (r * _sigmoid(r))).astype(y_ref.dtype)
    n_a = oa_ref.shape[1]
    o_ref[...] = (h_ref[...] + _dot(oa_ref[...], wo_ref[:n_a, :])
                  + _dot(y_ref[...], wo_ref[n_a:, :]))


def _gla_out(lq, lk, lv, lg, lr, o_attn, h, out_norm, w_out, *, batch):
    tokens, d = h.shape
    seq = tokens // batch
    tt = GLA_ROWS
    n_s = seq // tt
    row = lambda w: pl.BlockSpec((tt, w), lambda b, i: (b * n_s + i, 0))
    ins = [lq, lk, lv, lg, lr, o_attn, h]
    return pl.pallas_call(
        _gla_out_kernel,
        grid=(batch, n_s),
        in_specs=[row(a.shape[1]) for a in ins] + [_resident((1, GLA_DV)),
                                                   _resident(w_out.shape)],
        out_specs=row(d),
        out_shape=jax.ShapeDtypeStruct((tokens, d), F32),
        scratch_shapes=[pltpu.VMEM((GLA_HEADS, GLA_DV, HEAD_PAD), F32),
                        pltpu.VMEM((tt, GLA_HEADS * GLA_DV), MXU_DTYPE)],
        compiler_params=_params("parallel", "arbitrary"),
        name="gla_out",
    )(*ins, out_norm.reshape(1, -1), w_out.astype(MXU_DTYPE))


def _interleave(src_ref, slab_ref):
    batch, steps, d = src_ref.shape
    for s in range(d // LANES):
        for b in range(batch):
            slab_ref[s, pl.ds(b, steps, stride=batch), :] = src_ref[b, :, s * LANES:(s + 1) * LANES]
    return jnp.concatenate([slab_ref[s] for s in range(d // LANES)], axis=1)


def _deinterleave(x, slab_ref, dst_ref):
    batch, steps, d = dst_ref.shape
    for s in range(d // LANES):
        slab_ref[s] = x[:, s * LANES:(s + 1) * LANES]
    for s in range(d // LANES):
        for b in range(batch):
            dst_ref[b, :, s * LANES:(s + 1) * LANES] = slab_ref[s, pl.ds(b, steps, stride=batch), :]

def _prev_step(cur, prev_rolled, sub):
    cur_rolled = pltpu.roll(cur, SUBLANES // 2, axis=0)
    return jnp.where(sub < SUBLANES // 2, prev_rolled, cur_rolled), cur_rolled


def _odd_kernel(h_ref, gm_ref, win_ref, cw_ref, cb_ref, wgate_ref, bgate_ref, lam_ref,
                sa_ref, bw_ref, cwt_ref, sd_ref, wglu_ref, bglu_ref, wo_ref,
                o_ref,
                halo_ref, rg_a_ref, rg_b_ref, rg_x_ref, rg_carry_ref,
                uh_ref, cx_ref, s5_carry_ref, slab_ref):
    first = pl.program_id(0) == 0

    @pl.when(first)
    def _():
        halo_ref[...] = jnp.zeros_like(halo_ref)
        rg_carry_ref[...] = jnp.zeros_like(rg_carry_ref)
        s5_carry_ref[...] = jnp.zeros_like(s5_carry_ref)
        uh_ref[...] = jnp.zeros_like(uh_ref)

    tm = h_ref.shape[0] * h_ref.shape[1]
    groups = tm // SUBLANES
    h = _interleave(h_ref, slab_ref)
    xn = _rms(h, gm_ref[...]).astype(MXU_DTYPE)
    y = _dot(xn, win_ref[...])
    x_gate, x_rg, u = (y[:, i * RG_WIDTH:(i + 1) * RG_WIDTH] for i in range(3))

    step = SUBLANES // 2

    halo_rows = halo_ref.shape[0]
    ext = jnp.concatenate([halo_ref[...], x_rg], axis=0)
    halo_ref[...] = x_rg[tm - halo_rows:, :]
    cw = cw_ref[...]
    xc = cb_ref[...] + x_rg * cw[RG_CONV - 1:RG_CONV]
    for j in range(RG_CONV - 1):
        back = (RG_CONV - 1 - j) * step
        xc = xc + ext[halo_rows - back:halo_rows - back + tm, :] * cw[j:j + 1]

    gate_pre = _dot(xc.astype(MXU_DTYPE), wgate_ref[...]) + bgate_ref[...]

    ub = u.astype(MXU_DTYPE)
    half_u = S5_WIDTH // 2
    half_x = S5_LANES // 2
    u_halo = uh_ref.shape[0]
    u_ext = jnp.concatenate([uh_ref[...], u], axis=0)
    uh_ref[...] = u[tm - u_halo:, :]
    ub_prev = u_ext[u_halo - step:u_halo - step + tm, :].astype(MXU_DTYPE)
    for hf in range(2):
        cols = slice(hf * half_u, (hf + 1) * half_u)
        cx_ref[:, 2 * hf * half_x:2 * (hf + 1) * half_x] = _dot(
            jnp.concatenate([ub[:, cols], ub_prev[:, cols]], axis=1), bw_ref[hf])

    gates = _sigmoid(gate_pre)
    r_gate, i_gate = gates[:, :RG_WIDTH], gates[:, RG_WIDTH:]
    lam = lam_ref[...]
    softplus_neg = jnp.maximum(-lam, 0.0) + jnp.log1p(jnp.exp(-jnp.abs(lam)))
    log_a = (-RG_C * softplus_neg) * r_gate
    a = jnp.exp(log_a)
    rg_a_ref[...] = a
    var = -jnp.tanh(log_a) * (a * a + 1.0)
    rg_b_ref[...] = jnp.where(var > 0.0, var * lax.rsqrt(var), 0.0) * (i_gate * xc)

    sub = lax.broadcasted_iota(jnp.int32, (SUBLANES, RG_WIDTH), 0)

    x, ra_prev, rb_prev = rg_carry_ref[0], rg_carry_ref[1], rg_carry_ref[2]
    for g in range(groups):
        rows = slice(g * SUBLANES, (g + 1) * SUBLANES)
        a = rg_a_ref[rows, :]
        bx = rg_b_ref[rows, :]
        a_prev, ra_prev = _prev_step(a, ra_prev, sub)
        b_prev, rb_prev = _prev_step(bx, rb_prev, sub)
        x = (a * a_prev) * x + (a * b_prev + bx)
        rg_x_ref[rows, :] = x
    for i, v in enumerate((x, ra_prev, rb_prev)):
        rg_carry_ref[i] = v
    y_c = rg_x_ref[...] * _gelu_tanh(x_gate)

    for lc in range(S5_LANES // SCAN_LANES):
        hf, within = divmod(lc * SCAN_LANES, half_x)
        coef = slice(lc * SCAN_LANES, (lc + 1) * SCAN_LANES)
        re = slice(2 * hf * half_x + within, 2 * hf * half_x + within + SCAN_LANES)
        im = slice(re.start + half_x, re.stop + half_x)
        a2_re, a2_im = sa_ref[0, :, coef], sa_ref[1, :, coef]

        x_re, x_im = s5_carry_ref[0, :, coef], s5_carry_ref[1, :, coef]
        for g in range(groups):
            rows = slice(g * SUBLANES, (g + 1) * SUBLANES)
            x_re, x_im = (a2_re * x_re - a2_im * x_im + cx_ref[rows, re],
                          a2_re * x_im + a2_im * x_re + cx_ref[rows, im])
            cx_ref[rows, re] = x_re
            cx_ref[rows, im] = x_im
        s5_carry_ref[0, :, coef] = x_re
        s5_carry_ref[1, :, coef] = x_im

    ys = jnp.concatenate(
        [_dot(cx_ref[:, 2 * hf * half_x:2 * (hf + 1) * half_x].astype(MXU_DTYPE), cwt_ref[hf])
         for hf in range(2)], axis=1) + sd_ref[...] * u
    ys = _gelu_tanh(ys)
    y_d = ys * _sigmoid(_dot(ys.astype(MXU_DTYPE), wglu_ref[...]) + bglu_ref[...])
    out = (h + _dot(y_c.astype(MXU_DTYPE), wo_ref[:RG_WIDTH, :])
           + _dot(y_d.astype(MXU_DTYPE), wo_ref[RG_WIDTH:, :]))
    _deinterleave(out, slab_ref, o_ref)


def _block_diag(blocks):
    n, r, c = blocks.shape
    eye = jnp.eye(n, dtype=blocks.dtype)
    return (eye[:, None, :, None] * blocks[:, :, None, :]).reshape(n * r, n * c)


def _odd_mixer(h, mix_norm, w_in, conv_w, conv_b, w_a, b_a, w_i, b_i, lam,
               a_re, a_im, log_dt, b_re, b_im, c_re, c_im, d_skip, w_glu, b_glu, w_out, *,
               batch):
    assert batch == SUBLANES // 2
    tokens, d = h.shape
    seq = tokens // batch
    tm = ODD_ROWS
    steps = tm // batch

    dt = jnp.exp(log_dt.astype(F32))[:, None]
    lr, li = a_re.astype(F32), a_im.astype(F32)
    mag = jnp.exp(lr * dt)
    ab_re = mag * jnp.cos(li * dt)
    ab_im = mag * jnp.sin(li * dt)
    den = lr * lr + li * li
    nr, ni = ab_re - 1.0, ab_im
    coef_re = (nr * lr + ni * li) / den
    coef_im = (ni * lr - nr * li) / den
    bb_re = coef_re[..., None] * b_re - coef_im[..., None] * b_im
    bb_im = coef_re[..., None] * b_im + coef_im[..., None] * b_re
    abb_re = ab_re[..., None] * bb_re - ab_im[..., None] * bb_im
    abb_im = ab_re[..., None] * bb_im + ab_im[..., None] * bb_re
    halves = [slice(0, S5_GROUPS // 2), slice(S5_GROUPS // 2, S5_GROUPS)]

    def in_map(m_re, m_im, g):
        return jnp.concatenate([_block_diag(m_re[g].transpose(0, 2, 1)),
                                _block_diag(m_im[g].transpose(0, 2, 1))], axis=1)

    bw = jnp.stack([jnp.concatenate([in_map(bb_re, bb_im, g), in_map(abb_re, abb_im, g)],
                                    axis=0) for g in halves])
    cwt = jnp.stack([jnp.concatenate([_block_diag(c_re[g].transpose(0, 2, 1)),
                                      -_block_diag(c_im[g].transpose(0, 2, 1))], axis=0)
                     for g in halves])
    a2_re = ab_re * ab_re - ab_im * ab_im
    a2_im = 2.0 * ab_re * ab_im
    sa = jnp.stack([jnp.broadcast_to(v.reshape(1, -1), (SUBLANES, S5_LANES))
                    for v in (a2_re, a2_im)])
    wgate = jnp.concatenate([_block_diag(w_a), _block_diag(w_i)], axis=1)
    bgate = jnp.concatenate([b_a, b_i]).reshape(1, -1)

    cast = lambda w: w.astype(MXU_DTYPE)
    weights = [mix_norm.reshape(1, d), cast(w_in), conv_w, conv_b.reshape(1, -1), cast(wgate),
               bgate, lam.reshape(1, -1), sa, cast(bw), cast(cwt), d_skip.reshape(1, -1),
               cast(w_glu), b_glu.reshape(1, -1), cast(w_out)]
    row = pl.BlockSpec((batch, steps, d), lambda r: (0, r, 0))
    out = pl.pallas_call(
        _odd_kernel,
        grid=(seq // steps,),
        in_specs=[row] + [_resident(w.shape) for w in weights],
        out_specs=row,
        out_shape=jax.ShapeDtypeStruct((batch, seq, d), F32),
        scratch_shapes=[
            pltpu.VMEM((2 * SUBLANES, RG_WIDTH), F32),
            pltpu.VMEM((tm, RG_WIDTH), F32),
            pltpu.VMEM((tm, RG_WIDTH), F32),
            pltpu.VMEM((tm, RG_WIDTH), F32),
            pltpu.VMEM((3, SUBLANES, RG_WIDTH), F32),
            pltpu.VMEM((SUBLANES, S5_WIDTH), F32),
            pltpu.VMEM((tm, 2 * S5_LANES), F32),
            pltpu.VMEM((2, SUBLANES, S5_LANES), F32),
            pltpu.VMEM((d // LANES, tm, LANES), F32),
        ],
        compiler_params=_params("arbitrary"),
        name="odd_mixer",
    )(h.reshape(batch, seq, d), *weights)
    return out.reshape(tokens, d)


def kernel(x, p, ffn_a_norm, ffn_a_w1, ffn_a_w3, ffn_a_w2, mix_norm, ffn_b_norm, ffn_b_w1, ffn_b_w3, ffn_b_w2, ple_norm, ple_w_gate, ple_w_up, ev_w_in, mla_q_norm, mla_w_q_up, mla_kv_norm, mla_w_kv_up, gla_w_gate_up, gla_b_gate, gla_out_norm, ev_w_out, od_w_in, rg_conv_w, rg_conv_b, rg_w_a, rg_b_a, rg_w_i, rg_b_i, rg_lambda, s5_a_re, s5_a_im, s5_log_dt, s5_b_re, s5_b_im, s5_c_re, s5_c_im, s5_d, s5_w_glu, s5_b_glu, od_w_out, final_norm):
    batch, seq, d = x.shape
    depth = p.shape[0]
    h = x.reshape(batch * seq, d)
    cast = lambda w: w.astype(MXU_DTYPE)
    ffn_a = (ffn_a_norm, cast(ffn_a_w1), cast(ffn_a_w3), cast(ffn_a_w2))
    ffn_b = (ffn_b_norm, cast(ffn_b_w1), cast(ffn_b_w3), cast(ffn_b_w2))
    ple = (p, ple_norm, cast(ple_w_gate), cast(ple_w_up))
    for i in range(depth):
        j = i // 2
        h = _ffn(h, i, *ffn_a)
        if i % 2 == 0:
            q, k, v, lq, lk, lv, lg, lr = _even_proj(
                h, mix_norm[i], ev_w_in[j], mla_q_norm[j], mla_w_q_up[j], mla_kv_norm[j],
                mla_w_kv_up[j], gla_w_gate_up[j], gla_b_gate[j], batch=batch)
            o_attn = _attention(q, k, v, batch=batch)
            h = _gla_out(lq, lk, lv, lg, lr, o_attn, h, gla_out_norm[j], ev_w_out[j],
                         batch=batch)
        else:
            h = _odd_mixer(h, mix_norm[i], od_w_in[j], rg_conv_w[j], rg_conv_b[j], rg_w_a[j],
                           rg_b_a[j], rg_w_i[j], rg_b_i[j], rg_lambda[j], s5_a_re[j],
                           s5_a_im[j], s5_log_dt[j], s5_b_re[j], s5_b_im[j], s5_c_re[j],
                           s5_c_im[j], s5_d[j], s5_w_glu[j], s5_b_glu[j], od_w_out[j],
                           batch=batch)
        last = i == depth - 1
        h = _ffn(h, i, *ffn_b, ple=ple, final_norm=final_norm if last else None)
    return h.reshape(batch, seq, d)
```

```python
import functools
import math

import jax
import jax.numpy as jnp
from jax import lax
from jax.experimental import pallas as pl
from jax.experimental.pallas import tpu as pltpu

MXU_DTYPE = jnp.bfloat16
F32 = jnp.float32

EPS = 1e-6
MLA_HEADS = 8
MLA_NOPE = 64
MLA_ROPE = 32
MLA_V = 64
MLA_Q_RANK = 384
MLA_KV_RANK = 256
ROPE_BASE = 10000.0
GLA_HEADS = 4
GLA_DK = 64
GLA_DV = 128
GLA_GATE_RANK = 16
GLA_TAU = 16.0
GLA_CHUNK = 64
RG_WIDTH = 512
RG_BLOCKS = 8
RG_CONV = 4
RG_C = 8.0
S5_GROUP = 16
S5_GROUPS = 32
S5_STATE = 64
S5_WIDTH = S5_GROUP * S5_GROUPS
S5_LANES = S5_GROUPS * S5_STATE

LANES = 128
SUBLANES = 8
MXU_TILE = 256
HEAD_PAD = LANES
VMEM_LIMIT = 56 * 1024 * 1024

FFN_ROWS = 512
FFN_CHUNKS = 2
PROJ_ROWS = 512
ATTN_Q = 512
ATTN_HEADS = 4
GLA_ROWS = 512
ODD_ROWS = 512
SCAN_LANES = 512


def _rms(x, g):
    return x * lax.rsqrt(jnp.mean(x * x, axis=-1, keepdims=True) + EPS) * g


def _dot(a, b):
    return jnp.dot(a, b, preferred_element_type=F32)


def _dot_nt(a, b):
    return lax.dot_general(a, b, (((1,), (1,)), ((), ())), preferred_element_type=F32)


def _dot_tn(a, b):
    return lax.dot_general(a, b, (((0,), (0,)), ((), ())), preferred_element_type=F32)


def _sigmoid(x):
    return 1.0 / (1.0 + jnp.exp(-x))


def _gelu_tanh(x):
    return 0.5 * x * (1.0 + jnp.tanh(math.sqrt(2.0 / math.pi) * (x + 0.044715 * (x * x * x))))


def _log_sigmoid(x):
    return jnp.minimum(x, 0.0) - jnp.log1p(jnp.exp(-jnp.abs(x)))


def _resident(shape):
    return pl.BlockSpec(shape, lambda *_: (0,) * len(shape), pipeline_mode=pl.Buffered(1))


def _params(*semantics):
    return pltpu.CompilerParams(dimension_semantics=semantics, vmem_limit_bytes=VMEM_LIMIT)


def _ffn_kernel(*refs, ple, final):
    h_ref, g_ref, w1_ref, w3_ref, w2_ref = refs[:5]
    rest = list(refs[5:])
    o_ref = rest.pop()
    h = h_ref[...]
    xn = _rms(h, g_ref[...]).astype(MXU_DTYPE)
    tiles = w1_ref.shape[1] // MXU_TILE
    bounds = [MXU_TILE * ((tiles * c + FFN_CHUNKS - 1) // FFN_CHUNKS) for c in range(FFN_CHUNKS + 1)]
    y = None
    for lo, hi in zip(bounds[:-1], bounds[1:]):
        a = _dot(xn, w1_ref[:, lo:hi])
        b = _dot(xn, w3_ref[:, lo:hi])
        t = (a * _sigmoid(a) * b).astype(MXU_DTYPE)
        yc = _dot(t, w2_ref[lo:hi, :])
        y = yc if y is None else y + yc
    h = h + 0.5 * y
    if ple:
        p_ref, gp_ref, wg_ref, wu_ref = rest[:4]
        rest = rest[4:]
        gate = _sigmoid(_dot(_rms(h, gp_ref[...]).astype(MXU_DTYPE), wg_ref[...]))
        up = _dot(p_ref[...].astype(MXU_DTYPE), wu_ref[...])
        h = h + up * gate
    if final:
        h = _rms(h, rest[0][...])
    o_ref[...] = h


def _ffn(h, layer, norm, w1, w3, w2, *, ple=None, final_norm=None):
    tokens, d = h.shape
    tm = FFN_ROWS
    hspec = pl.BlockSpec((tm, d), lambda r: (r, 0))

    def of_layer(a):
        rest = a.shape[1:]
        return pl.BlockSpec((None,) + rest, lambda r: (layer,) + (0,) * len(rest),
                            pipeline_mode=pl.Buffered(1))

    gains = lambda g: g.reshape(g.shape[0], 1, d)
    args = [h, gains(norm), w1, w3, w2]
    specs = [hspec] + [of_layer(a) for a in args[1:]]
    if ple is not None:
        p, gp, wg, wu = ple
        p = p.reshape(p.shape[0], tokens, p.shape[-1])
        args += [p, gains(gp), wg, wu]
        specs += [pl.BlockSpec((None, tm, p.shape[-1]), lambda r: (layer, r, 0)),
                  of_layer(args[-3]), of_layer(wg), of_layer(wu)]
    if final_norm is not None:
        args.append(final_norm.reshape(1, d))
        specs.append(_resident((1, d)))
    return pl.pallas_call(
        functools.partial(_ffn_kernel, ple=ple is not None, final=final_norm is not None),
        grid=(tokens // tm,),
        in_specs=specs,
        out_specs=hspec,
        out_shape=jax.ShapeDtypeStruct((tokens, d), F32),
        compiler_params=_params("parallel"),
        name="ffn",
    )(*args)


_EV_COLS = (MLA_Q_RANK, MLA_KV_RANK, HEAD_PAD, GLA_HEADS * GLA_DK, GLA_HEADS * GLA_DK,
            GLA_HEADS * GLA_DV, GLA_HEADS * GLA_DV)
ROT_SHIFT = HEAD_PAD - MLA_ROPE


def _even_proj_kernel(h_ref, gm_ref, win_ref, gq_ref, wq_ref, gkv_ref, wk_ref, wv_ref,
                      wg_ref, bg_ref, cq_ref, sq_ref, ck_ref, sk_ref,
                      q_ref, k_ref, v_ref, lq_ref, lk_ref, lv_ref, lg_ref, lr_ref):
    xn = _rms(h_ref[...], gm_ref[...]).astype(MXU_DTYPE)
    y = _dot(xn, win_ref[...])
    offs = [0]
    for w in _EV_COLS:
        offs.append(offs[-1] + w)
    c_q, c_kv, kr, lq, lk, lv, lr = (
        y[:, offs[i]:offs[i + 1]] for i in range(len(_EV_COLS)))
    g_low = kr

    cqn = _rms(c_q, gq_ref[...]).astype(MXU_DTYPE)
    qf = _dot(cqn, wq_ref[...])
    ckn = _rms(c_kv, gkv_ref[...]).astype(MXU_DTYPE)
    kn = _dot(ckn, wk_ref[...])
    v_lane = lax.broadcasted_iota(jnp.int32, (1, MLA_HEADS * HEAD_PAD), 1) % HEAD_PAD
    v_ref[...] = (_dot(ckn, wv_ref[...])
                  + jnp.where(v_lane == MLA_V, 1.0, 0.0)).astype(v_ref.dtype)
    k_rope = kr * ck_ref[...] + pltpu.roll(kr, ROT_SHIFT, axis=1) * sk_ref[...]
    cq, sq = cq_ref[...], sq_ref[...]
    for hd in range(MLA_HEADS):
        sl = slice(hd * HEAD_PAD, (hd + 1) * HEAD_PAD)
        q_h = qf[:, sl]
        q_ref[:, sl] = (q_h * cq + pltpu.roll(q_h, ROT_SHIFT, axis=1) * sq).astype(q_ref.dtype)
        k_ref[:, sl] = (kn[:, sl] + k_rope).astype(k_ref.dtype)

    gate = _dot(g_low.astype(MXU_DTYPE), wg_ref[...]) + bg_ref[...]
    lg_ref[...] = _log_sigmoid(gate) * (1.0 / GLA_TAU)
    lq_ref[...] = lq
    lk_ref[...] = lk
    lv_ref[...] = lv.astype(lv_ref.dtype)
    lr_ref[...] = lr


def _pad_heads(w, heads, width):
    lead = w.shape[:-1]
    w = w.reshape(lead + (heads, width))
    w = jnp.pad(w, [(0, 0)] * len(lead) + [(0, 0), (0, HEAD_PAD - width)])
    return w.reshape(lead + (heads * HEAD_PAD,))


def _rot_half(w):
    half = MLA_ROPE // 2
    return jnp.concatenate([-w[..., half:], w[..., :half]], axis=-1)


def _head_slot(w_nope, w_rope):
    return jnp.concatenate([w_nope, w_rope, _rot_half(w_rope)], axis=-1)


def _even_proj(h, mix_norm, w_in, q_norm, w_q_up, kv_norm, w_kv_up, w_gate_up, b_gate, *,
               batch):
    tokens, d = h.shape
    seq = tokens // batch
    tm = PROJ_ROWS
    n_s = seq // tm
    splits = [MLA_Q_RANK, MLA_KV_RANK, MLA_ROPE, GLA_HEADS * GLA_DK, GLA_HEADS * GLA_DK,
              GLA_HEADS * GLA_DV, GLA_GATE_RANK, GLA_HEADS * GLA_DV]
    offs = [0]
    for s in splits:
        offs.append(offs[-1] + s)
    wc_q, wc_kv, w_kr, w_lq, w_lk, w_lv, w_gl, w_lr = (
        w_in[:, offs[i]:offs[i + 1]] for i in range(len(splits)))
    gl_lanes = jnp.pad(w_gl, [(0, 0), (0, MLA_NOPE - GLA_GATE_RANK)])
    win = jnp.concatenate([
        wc_q, wc_kv, _head_slot(gl_lanes, w_kr), w_lq, w_lk, w_lv, w_lr], axis=1)

    wq = w_q_up.reshape(MLA_Q_RANK, MLA_HEADS, MLA_NOPE + MLA_ROPE)
    wq_full = _head_slot(wq[..., :MLA_NOPE], wq[..., MLA_NOPE:]).reshape(MLA_Q_RANK, -1)
    wkv = w_kv_up.reshape(MLA_KV_RANK, MLA_HEADS, MLA_NOPE + MLA_V)
    wk = _pad_heads(wkv[..., :MLA_NOPE].reshape(MLA_KV_RANK, -1), MLA_HEADS, MLA_NOPE)
    wv = _pad_heads(wkv[..., MLA_NOPE:].reshape(MLA_KV_RANK, -1), MLA_HEADS, MLA_V)
    wg = jnp.pad(w_gate_up, [(0, HEAD_PAD - GLA_GATE_RANK), (0, 0)])
    bg = b_gate.reshape(1, -1)

    half = MLA_ROPE // 2
    inv = ROPE_BASE ** (-jnp.arange(half, dtype=F32) / half)
    ang = jnp.arange(seq).astype(F32)[:, None] * inv[None, :]
    cos2 = jnp.tile(jnp.cos(ang), (1, 2))
    sin2 = jnp.tile(jnp.sin(ang), (1, 2))
    scale = (MLA_NOPE + MLA_ROPE) ** -0.5 * math.log2(math.e)
    ones = jnp.ones((seq, MLA_NOPE), F32)
    tail = jnp.zeros((seq, HEAD_PAD - MLA_NOPE - MLA_ROPE), F32)
    cq = jnp.concatenate([ones, cos2, tail], axis=1) * scale
    sq = jnp.concatenate([0 * ones, sin2, tail], axis=1) * scale
    ck = jnp.concatenate([0 * ones, cos2, tail], axis=1)
    sk = jnp.concatenate([0 * ones, sin2, tail], axis=1)

    cast = lambda w: w.astype(MXU_DTYPE)
    row = lambda w: pl.BlockSpec((tm, w), lambda r: (r, 0))
    tab = pl.BlockSpec((tm, HEAD_PAD), lambda r: (r % n_s, 0))
    gl_w = GLA_HEADS * GLA_DK
    gv_w = GLA_HEADS * GLA_DV
    qk_w = MLA_HEADS * HEAD_PAD
    outs = [(qk_w, MXU_DTYPE), (qk_w, MXU_DTYPE), (qk_w, MXU_DTYPE),
            (gl_w, F32), (gl_w, F32), (gv_w, MXU_DTYPE), (gl_w, F32), (gv_w, F32)]
    weights = [mix_norm.reshape(1, d), cast(win), q_norm.reshape(1, -1), cast(wq_full),
               kv_norm.reshape(1, -1), cast(wk), cast(wv), cast(wg), bg]
    return pl.pallas_call(
        _even_proj_kernel,
        grid=(tokens // tm,),
        in_specs=[row(d)] + [_resident(w.shape) for w in weights] + [tab] * 4,
        out_specs=[row(w) for w, _ in outs],
        out_shape=[jax.ShapeDtypeStruct((tokens, w), dt) for w, dt in outs],
        compiler_params=_params("parallel"),
        name="even_proj",
    )(h, *weights, cq, sq, ck, sk)


def _attn_kernel(q_ref, k_ref, v_ref, o_ref, s_ref, m_ref, acc_ref, *, tq):
    n_q = q_ref.shape[0] // tq
    heads = m_ref.shape[0]
    row = lax.broadcasted_iota(jnp.int32, (tq, tq), 0)
    col = lax.broadcasted_iota(jnp.int32, (tq, tq), 1)
    causal = col <= row
    sls = [slice(hd * HEAD_PAD, (hd + 1) * HEAD_PAD) for hd in range(heads)]
    lane = lax.broadcasted_iota(jnp.int32, (tq, HEAD_PAD), 1)

    def tile(i):
        return pl.ds(pl.multiple_of(i * tq, tq), tq)

    def scores(hd, qi, j):
        return _dot_nt(q_ref[tile(qi), sls[hd]], k_ref[tile(j), sls[hd]])

    def softmax_pv(hd, s, j, masked):
        if masked:
            s = jnp.where(causal, s, -jnp.inf)
        m = m_ref[hd]
        m_new = jnp.maximum(m, jnp.max(s, axis=1, keepdims=True)).astype(MXU_DTYPE)
        alpha = jnp.exp2(m - m_new.astype(F32))
        p = jnp.exp2(s.astype(MXU_DTYPE) - jnp.concatenate([m_new] * (tq // LANES), axis=1))
        m_ref[hd] = m_new.astype(F32)
        acc_ref[hd] = alpha * acc_ref[hd] + _dot(p, v_ref[tile(j), sls[hd]])

    def step(qi, j, masked, next_qi, next_j):
        s_cur = s_ref[...]
        for hd in range(heads):
            s_next = scores(hd + 1, qi, j) if hd + 1 < heads else scores(0, next_qi, next_j)
            softmax_pv(hd, s_cur, j, masked)
            s_cur = s_next
        s_ref[...] = s_cur

    s_ref[...] = scores(0, 0, 0)

    def q_tile(qi, _):
        m_ref[...] = jnp.full(m_ref.shape, -jnp.inf, F32)
        acc_ref[...] = jnp.zeros(acc_ref.shape, F32)

        def body(j, _):
            step(qi, j, False, qi, j + 1)
            return 0

        lax.fori_loop(0, qi, body, 0)
        step(qi, qi, True, jnp.minimum(qi + 1, n_q - 1), 0)
        for pair in range(heads // 2):
            o_a, o_b = (acc_ref[hd] / acc_ref[hd][:, MLA_V:MLA_V + 1]
                        for hd in (2 * pair, 2 * pair + 1))
            o_ref[tile(qi), pair * LANES:(pair + 1) * LANES] = jnp.where(
                lane < MLA_V, o_a, pltpu.roll(o_b, MLA_V, axis=1)).astype(o_ref.dtype)
        return 0

    lax.fori_loop(0, n_q, q_tile, 0)


def _attention(q, k, v, *, batch):
    tokens = q.shape[0]
    seq = tokens // batch
    tq = ATTN_Q
    q3 = q.reshape(batch, seq, -1)
    k3 = k.reshape(batch, seq, -1)
    v3 = v.reshape(batch, seq, -1)
    nh = ATTN_HEADS
    group = lambda w, **kw: pl.BlockSpec((None, seq, nh * w), lambda b, g: (b, 0, g), **kw)
    once = dict(pipeline_mode=pl.Buffered(1))
    out = pl.pallas_call(
        functools.partial(_attn_kernel, tq=tq),
        grid=(batch, MLA_HEADS // nh),
        in_specs=[group(HEAD_PAD, **once), group(HEAD_PAD), group(HEAD_PAD)],
        out_specs=group(MLA_V, **once),
        out_shape=jax.ShapeDtypeStruct((batch, seq, MLA_HEADS * MLA_V), MXU_DTYPE),
        scratch_shapes=[pltpu.VMEM((tq, tq), F32), pltpu.VMEM((nh, tq, LANES), F32),
                        pltpu.VMEM((nh, tq, HEAD_PAD), F32)],
        compiler_params=_params("parallel", "parallel"),
        name="attention",
    )(q3, k3, v3)
    return out.reshape(tokens, -1)


def _gla_out_kernel(lq_ref, lk_ref, lv_ref, lg_ref, lr_ref, oa_ref, h_ref, gn_ref, wo_ref,
                    o_ref, state_ref, y_ref):
    @pl.when(pl.program_id(1) == 0)
    def _():
        state_ref[...] = jnp.zeros_like(state_ref)

    tt, width = lg_ref.shape
    chunk = GLA_CHUNK
    pos = lax.broadcasted_iota(jnp.int32, (tt, width), 0) & (chunk - 1)
    b = lg_ref[...]
    d = 1
    while d < chunk:
        b = b + jnp.where(pos >= d, pltpu.roll(b, d, axis=0), 0.0)
        d *= 2
    lk = lk_ref[...]
    parity = lambda rows: (lax.broadcasted_iota(jnp.int32, (rows, width), 1) // GLA_DK) % 2
    q_e = lq_ref[...] * (GLA_DK ** -0.5) * jnp.exp(b)
    q_par = [jnp.where(parity(tt) == par, q_e, 0.0).astype(MXU_DTYPE) for par in range(2)]
    k_e = (lk * jnp.exp(-b)).astype(MXU_DTYPE)
    tri = (lax.broadcasted_iota(jnp.int32, (chunk, chunk), 1)
           <= lax.broadcasted_iota(jnp.int32, (chunk, chunk), 0))
    gn = gn_ref[...]
    for c in range(tt // chunk):
        rows = slice(c * chunk, (c + 1) * chunk)
        b_c = b[rows]
        b_last = b_c[chunk - 1:chunk]
        k_end = lk[rows] * jnp.exp(b_last - b_c)
        k_end_par = [jnp.where(parity(chunk) == par, k_end, 0.0).astype(MXU_DTYPE)
                     for par in range(2)]
        decay = jnp.exp(b_last)
        for hd in range(GLA_HEADS):
            sl = slice((hd // 2) * LANES, (hd // 2 + 1) * LANES)
            vs = slice(hd * GLA_DV, (hd + 1) * GLA_DV)
            v_c = lv_ref[rows, vs]
            q_h = q_par[hd % 2][rows, sl]
            att = jnp.where(tri, _dot_nt(q_h, k_e[rows, sl]), 0.0)
            st = state_ref[hd]
            o = _dot(att.astype(MXU_DTYPE), v_c) + _dot_nt(q_h, st.astype(MXU_DTYPE))
            state_ref[hd] = st * decay[:, sl] + _dot_tn(v_c, k_end_par[hd % 2][:, sl])
            o = _rms(o, gn)
            r = lr_ref[rows, vs]
            y_ref[rows, vs] = (o * (r * _sigmoid(r))).astype(y_ref.dtype)
    n_a = oa_ref.shape[1]
    o_ref[...] = (h_ref[...] + _dot(oa_ref[...], wo_ref[:n_a, :])
                  + _dot(y_ref[...], wo_ref[n_a:, :]))


def _gla_out(lq, lk, lv, lg, lr, o_attn, h, out_norm, w_out, *, batch):
    tokens, d = h.shape
    seq = tokens // batch
    tt = GLA_ROWS
    n_s = seq // tt
    row = lambda w: pl.BlockSpec((tt, w), lambda b, i: (b * n_s + i, 0))
    ins = [lq, lk, lv, lg, lr, o_attn, h]
    return pl.pallas_call(
        _gla_out_kernel,
        grid=(batch, n_s),
        in_specs=[row(a.shape[1]) for a in ins] + [_resident((1, GLA_DV)),
                                                   _resident(w_out.shape)],
        out_specs=row(d),
        out_shape=jax.ShapeDtypeStruct((tokens, d), F32),
        scratch_shapes=[pltpu.VMEM((GLA_HEADS, GLA_DV, HEAD_PAD), F32),
                        pltpu.VMEM((tt, GLA_HEADS * GLA_DV), MXU_DTYPE)],
        compiler_params=_params("parallel", "arbitrary"),
        name="gla_out",
    )(*ins, out_norm.reshape(1, -1), w_out.astype(MXU_DTYPE))


def _interleave(src_ref, slab_ref):
    batch, steps, d = src_ref.shape
    for s in range(d // LANES):
        for b in range(batch):
            slab_ref[s, pl.ds(b, steps, stride=batch), :] = src_ref[b, :, s * LANES:(s + 1) * LANES]
    return jnp.concatenate([slab_ref[s] for s in range(d // LANES)], axis=1)


def _deinterleave(x, slab_ref, dst_ref):
    batch, steps, d = dst_ref.shape
    for s in range(d // LANES):
        slab_ref[s] = x[:, s * LANES:(s + 1) * LANES]
    for s in range(d // LANES):
        for b in range(batch):
            dst_ref[b, :, s * LANES:(s + 1) * LANES] = slab_ref[s, pl.ds(b, steps, stride=batch), :]

def _prev_step(cur, prev_rolled, sub):
    cur_rolled = pltpu.roll(cur, SUBLANES // 2, axis=0)
    return jnp.where(sub < SUBLANES // 2, prev_rolled, cur_rolled), cur_rolled


def _odd_kernel(h_ref, gm_ref, win_ref, cw_ref, cb_ref, wgate_ref, bgate_ref, lam_ref,
                sa_ref, bw_ref, cwt_ref, sd_ref, wglu_ref, bglu_ref, wo_ref,
                o_ref,
                halo_ref, rg_a_ref, rg_b_ref, rg_x_ref, rg_carry_ref,
                uh_ref, cx_ref, s5_carry_ref, slab_ref):
    first = pl.program_id(0) == 0

    @pl.when(first)
    def _():
        halo_ref[...] = jnp.zeros_like(halo_ref)
        rg_carry_ref[...] = jnp.zeros_like(rg_carry_ref)
        s5_carry_ref[...] = jnp.zeros_like(s5_carry_ref)
        uh_ref[...] = jnp.zeros_like(uh_ref)

    tm = h_ref.shape[0] * h_ref.shape[1]
    groups = tm // SUBLANES
    h = _interleave(h_ref, slab_ref)
    xn = _rms(h, gm_ref[...]).astype(MXU_DTYPE)
    y = _dot(xn, win_ref[...])
    x_gate, x_rg, u = (y[:, i * RG_WIDTH:(i + 1) * RG_WIDTH] for i in range(3))

    step = SUBLANES // 2

    halo_rows = halo_ref.shape[0]
    ext = jnp.concatenate([halo_ref[...], x_rg], axis=0)
    halo_ref[...] = x_rg[tm - halo_rows:, :]
    cw = cw_ref[...]
    xc = cb_ref[...] + x_rg * cw[RG_CONV - 1:RG_CONV]
    for j in range(RG_CONV - 1):
        back = (RG_CONV - 1 - j) * step
        xc = xc + ext[halo_rows - back:halo_rows - back + tm, :] * cw[j:j + 1]

    xcb = xc.astype(MXU_DTYPE)
    half_rg = RG_WIDTH // 2
    g_lo = _dot(xcb[:, :half_rg], wgate_ref[0])
    g_hi = _dot(xcb[:, half_rg:], wgate_ref[1])
    gate_pre = jnp.concatenate([g_lo[:, :half_rg], g_hi[:, :half_rg],
                                g_lo[:, half_rg:], g_hi[:, half_rg:]], axis=1) + bgate_ref[...]

    ub = u.astype(MXU_DTYPE)
    half_u = S5_WIDTH // 2
    half_x = S5_LANES // 2
    u_halo = uh_ref.shape[0]
    u_ext = jnp.concatenate([uh_ref[...], u], axis=0)
    uh_ref[...] = u[tm - u_halo:, :]
    ub_prev = u_ext[u_halo - step:u_halo - step + tm, :].astype(MXU_DTYPE)
    for hf in range(2):
        cols = slice(hf * half_u, (hf + 1) * half_u)
        cx_ref[:, 2 * hf * half_x:2 * (hf + 1) * half_x] = _dot(
            jnp.concatenate([ub[:, cols], ub_prev[:, cols]], axis=1), bw_ref[hf])

    gates = _sigmoid(gate_pre)
    r_gate, i_gate = gates[:, :RG_WIDTH], gates[:, RG_WIDTH:]
    lam = lam_ref[...]
    softplus_neg = jnp.maximum(-lam, 0.0) + jnp.log1p(jnp.exp(-jnp.abs(lam)))
    log_a = (-RG_C * softplus_neg) * r_gate
    a = jnp.exp(log_a)
    rg_a_ref[...] = a
    var = -jnp.tanh(log_a) * (a * a + 1.0)
    rg_b_ref[...] = jnp.where(var > 0.0, var * lax.rsqrt(var), 0.0) * (i_gate * xc)

    sub = lax.broadcasted_iota(jnp.int32, (SUBLANES, RG_WIDTH), 0)

    x, ra_prev, rb_prev = rg_carry_ref[0], rg_carry_ref[1], rg_carry_ref[2]
    for g in range(groups):
        rows = slice(g * SUBLANES, (g + 1) * SUBLANES)
        a = rg_a_ref[rows, :]
        bx = rg_b_ref[rows, :]
        a_prev, ra_prev = _prev_step(a, ra_prev, sub)
        b_prev, rb_prev = _prev_step(bx, rb_prev, sub)
        x = (a * a_prev) * x + (a * b_prev + bx)
        rg_x_ref[rows, :] = x
    for i, v in enumerate((x, ra_prev, rb_prev)):
        rg_carry_ref[i] = v
    y_c = rg_x_ref[...] * _gelu_tanh(x_gate)

    for lc in range(S5_LANES // SCAN_LANES):
        hf, within = divmod(lc * SCAN_LANES, half_x)
        coef = slice(lc * SCAN_LANES, (lc + 1) * SCAN_LANES)
        re = slice(2 * hf * half_x + within, 2 * hf * half_x + within + SCAN_LANES)
        im = slice(re.start + half_x, re.stop + half_x)
        a2_re, a2_im = sa_ref[0, :, coef], sa_ref[1, :, coef]

        x_re, x_im = s5_carry_ref[0, :, coef], s5_carry_ref[1, :, coef]
        for g in range(groups):
            rows = slice(g * SUBLANES, (g + 1) * SUBLANES)
            x_re, x_im = (a2_re * x_re - a2_im * x_im + cx_ref[rows, re],
                          a2_re * x_im + a2_im * x_re + cx_ref[rows, im])
            cx_ref[rows, re] = x_re
            cx_ref[rows, im] = x_im
        s5_carry_ref[0, :, coef] = x_re
        s5_carry_ref[1, :, coef] = x_im

    ys = jnp.concatenate(
        [_dot(cx_ref[:, 2 * hf * half_x:2 * (hf + 1) * half_x].astype(MXU_DTYPE), cwt_ref[hf])
         for hf in range(2)], axis=1) + sd_ref[...] * u
    ys = _gelu_tanh(ys)
    y_d = ys * _sigmoid(_dot(ys.astype(MXU_DTYPE), wglu_ref[...]) + bglu_ref[...])
    out = (h + _dot(y_c.astype(MXU_DTYPE), wo_ref[:RG_WIDTH, :])
           + _dot(y_d.astype(MXU_DTYPE), wo_ref[RG_WIDTH:, :]))
    _deinterleave(out, slab_ref, o_ref)


def _block_diag(blocks):
    n, r, c = blocks.shape
    eye = jnp.eye(n, dtype=blocks.dtype)
    return (eye[:, None, :, None] * blocks[:, :, None, :]).reshape(n * r, n * c)


def _odd_mixer(h, mix_norm, w_in, conv_w, conv_b, w_a, b_a, w_i, b_i, lam,
               a_re, a_im, log_dt, b_re, b_im, c_re, c_im, d_skip, w_glu, b_glu, w_out, *,
               batch):
    assert batch == SUBLANES // 2
    tokens, d = h.shape
    seq = tokens // batch
    tm = ODD_ROWS
    steps = tm // batch

    dt = jnp.exp(log_dt.astype(F32))[:, None]
    lr, li = a_re.astype(F32), a_im.astype(F32)
    mag = jnp.exp(lr * dt)
    ab_re = mag * jnp.cos(li * dt)
    ab_im = mag * jnp.sin(li * dt)
    den = lr * lr + li * li
    nr, ni = ab_re - 1.0, ab_im
    coef_re = (nr * lr + ni * li) / den
    coef_im = (ni * lr - nr * li) / den
    bb_re = coef_re[..., None] * b_re - coef_im[..., None] * b_im
    bb_im = coef_re[..., None] * b_im + coef_im[..., None] * b_re
    abb_re = ab_re[..., None] * bb_re - ab_im[..., None] * bb_im
    abb_im = ab_re[..., None] * bb_im + ab_im[..., None] * bb_re
    halves = [slice(0, S5_GROUPS // 2), slice(S5_GROUPS // 2, S5_GROUPS)]

    def in_map(m_re, m_im, g):
        return jnp.concatenate([_block_diag(m_re[g].transpose(0, 2, 1)),
                                _block_diag(m_im[g].transpose(0, 2, 1))], axis=1)

    bw = jnp.stack([jnp.concatenate([in_map(bb_re, bb_im, g), in_map(abb_re, abb_im, g)],
                                    axis=0) for g in halves])
    cwt = jnp.stack([jnp.concatenate([_block_diag(c_re[g].transpose(0, 2, 1)),
                                      -_block_diag(c_im[g].transpose(0, 2, 1))], axis=0)
                     for g in halves])
    a2_re = ab_re * ab_re - ab_im * ab_im
    a2_im = 2.0 * ab_re * ab_im
    sa = jnp.stack([jnp.broadcast_to(v.reshape(1, -1), (SUBLANES, S5_LANES))
                    for v in (a2_re, a2_im)])
    hb = RG_BLOCKS // 2
    wgate = jnp.stack([jnp.concatenate([_block_diag(w_a[g]), _block_diag(w_i[g])], axis=1)
                       for g in (slice(0, hb), slice(hb, RG_BLOCKS))])
    bgate = jnp.concatenate([b_a, b_i]).reshape(1, -1)

    cast = lambda w: w.astype(MXU_DTYPE)
    weights = [mix_norm.reshape(1, d), cast(w_in), conv_w, conv_b.reshape(1, -1), cast(wgate),
               bgate, lam.reshape(1, -1), sa, cast(bw), cast(cwt), d_skip.reshape(1, -1),
               cast(w_glu), b_glu.reshape(1, -1), cast(w_out)]
    row = pl.BlockSpec((batch, steps, d), lambda r: (0, r, 0))
    out = pl.pallas_call(
        _odd_kernel,
        grid=(seq // steps,),
        in_specs=[row] + [_resident(w.shape) for w in weights],
        out_specs=row,
        out_shape=jax.ShapeDtypeStruct((batch, seq, d), F32),
        scratch_shapes=[
            pltpu.VMEM((2 * SUBLANES, RG_WIDTH), F32),
            pltpu.VMEM((tm, RG_WIDTH), F32),
            pltpu.VMEM((tm, RG_WIDTH), F32),
            pltpu.VMEM((tm, RG_WIDTH), F32),
            pltpu.VMEM((3, SUBLANES, RG_WIDTH), F32),
            pltpu.VMEM((SUBLANES, S5_WIDTH), F32),
            pltpu.VMEM((tm, 2 * S5_LANES), F32),
            pltpu.VMEM((2, SUBLANES, S5_LANES), F32),
            pltpu.VMEM((d // LANES, tm, LANES), F32),
        ],
        compiler_params=_params("arbitrary"),
        name="odd_mixer",
    )(h.reshape(batch, seq, d), *weights)
    return out.reshape(tokens, d)


def kernel(x, p, ffn_a_norm, ffn_a_w1, ffn_a_w3, ffn_a_w2, mix_norm, ffn_b_norm, ffn_b_w1, ffn_b_w3, ffn_b_w2, ple_norm, ple_w_gate, ple_w_up, ev_w_in, mla_q_norm, mla_w_q_up, mla_kv_norm, mla_w_kv_up, gla_w_gate_up, gla_b_gate, gla_out_norm, ev_w_out, od_w_in, rg_conv_w, rg_conv_b, rg_w_a, rg_b_a, rg_w_i, rg_b_i, rg_lambda, s5_a_re, s5_a_im, s5_log_dt, s5_b_re, s5_b_im, s5_c_re, s5_c_im, s5_d, s5_w_glu, s5_b_glu, od_w_out, final_norm):
    batch, seq, d = x.shape
    depth = p.shape[0]
    h = x.reshape(batch * seq, d)
    cast = lambda w: w.astype(MXU_DTYPE)
    ffn_a = (ffn_a_norm, cast(ffn_a_w1), cast(ffn_a_w3), cast(ffn_a_w2))
    ffn_b = (ffn_b_norm, cast(ffn_b_w1), cast(ffn_b_w3), cast(ffn_b_w2))
    ple = (p, ple_norm, cast(ple_w_gate), cast(ple_w_up))
    for i in range(depth):
        j = i // 2
        h = _ffn(h, i, *ffn_a)
        if i % 2 == 0:
            q, k, v, lq, lk, lv, lg, lr = _even_proj(
                h, mix_norm[i], ev_w_in[j], mla_q_norm[j], mla_w_q_up[j], mla_kv_norm[j],
                mla_w_kv_up[j], gla_w_gate_up[j], gla_b_gate[j], batch=batch)
            o_attn = _attention(q, k, v, batch=batch)
            h = _gla_out(lq, lk, lv, lg, lr, o_attn, h, gla_out_norm[j], ev_w_out[j],
                         batch=batch)
        else:
            h = _odd_mixer(h, mix_norm[i], od_w_in[j], rg_conv_w[j], rg_conv_b[j], rg_w_a[j],
                           rg_b_a[j], rg_w_i[j], rg_b_i[j], rg_lambda[j], s5_a_re[j],
                           s5_a_im[j], s5_log_dt[j], s5_b_re[j], s5_b_im[j], s5_c_re[j],
                           s5_c_im[j], s5_d[j], s5_w_glu[j], s5_b_glu[j], od_w_out[j],
                           batch=batch)
        last = i == depth - 1
        h = _ffn(h, i, *ffn_b, ple=ple, final_norm=final_norm if last else None)
    return h.reshape(batch, seq, d)
```

```python
import functools
import math

import jax
import jax.numpy as jnp
from jax import lax
from jax.experimental import pallas as pl
from jax.experimental.pallas import tpu as pltpu

MXU_DTYPE = jnp.bfloat16
F32 = jnp.float32

EPS = 1e-6
MLA_HEADS = 8
MLA_NOPE = 64
MLA_ROPE = 32
MLA_V = 64
MLA_Q_RANK = 384
MLA_KV_RANK = 256
ROPE_BASE = 10000.0
GLA_HEADS = 4
GLA_DK = 64
GLA_DV = 128
GLA_GATE_RANK = 16
GLA_TAU = 16.0
GLA_CHUNK = 64
RG_WIDTH = 512
RG_BLOCKS = 8
RG_CONV = 4
RG_C = 8.0
S5_GROUP = 16
S5_GROUPS = 32
S5_STATE = 64
S5_WIDTH = S5_GROUP * S5_GROUPS
S5_LANES = S5_GROUPS * S5_STATE

LANES = 128
SUBLANES = 8
MXU_TILE = 256
HEAD_PAD = LANES
VMEM_LIMIT = 56 * 1024 * 1024

FFN_ROWS = 512
FFN_CHUNKS = 2
PROJ_ROWS = 512
ATTN_Q = 512
ATTN_HEADS = 4
GLA_ROWS = 512
ODD_ROWS = 512
SCAN_LANES = 512


def _rms(x, g):
    return x * lax.rsqrt(jnp.mean(x * x, axis=-1, keepdims=True) + EPS) * g


def _dot(a, b):
    return jnp.dot(a, b, preferred_element_type=F32)


def _dot_nt(a, b):
    return lax.dot_general(a, b, (((1,), (1,)), ((), ())), preferred_element_type=F32)


def _dot_tn(a, b):
    return lax.dot_general(a, b, (((0,), (0,)), ((), ())), preferred_element_type=F32)


def _sigmoid(x):
    return 1.0 / (1.0 + jnp.exp(-x))


def _gelu_tanh(x):
    return 0.5 * x * (1.0 + jnp.tanh(math.sqrt(2.0 / math.pi) * (x + 0.044715 * (x * x * x))))


def _log_sigmoid(x):
    return jnp.minimum(x, 0.0) - jnp.log1p(jnp.exp(-jnp.abs(x)))


def _resident(shape):
    return pl.BlockSpec(shape, lambda *_: (0,) * len(shape), pipeline_mode=pl.Buffered(1))


def _params(*semantics):
    return pltpu.CompilerParams(dimension_semantics=semantics, vmem_limit_bytes=VMEM_LIMIT)


def _ffn_kernel(*refs, ple, final):
    h_ref, g_ref, w1_ref, w3_ref, w2_ref = refs[:5]
    rest = list(refs[5:])
    o_ref = rest.pop()
    h = h_ref[...]
    xn = _rms(h, g_ref[...]).astype(MXU_DTYPE)
    tiles = w1_ref.shape[1] // MXU_TILE
    bounds = [MXU_TILE * ((tiles * c + FFN_CHUNKS - 1) // FFN_CHUNKS) for c in range(FFN_CHUNKS + 1)]
    y = None
    for lo, hi in zip(bounds[:-1], bounds[1:]):
        a = _dot(xn, w1_ref[:, lo:hi])
        b = _dot(xn, w3_ref[:, lo:hi])
        t = (a * _sigmoid(a) * b).astype(MXU_DTYPE)
        yc = _dot(t, w2_ref[lo:hi, :])
        y = yc if y is None else y + yc
    h = h + 0.5 * y
    if ple:
        p_ref, gp_ref, wg_ref, wu_ref = rest[:4]
        rest = rest[4:]
        gate = _sigmoid(_dot(_rms(h, gp_ref[...]).astype(MXU_DTYPE), wg_ref[...]))
        up = _dot(p_ref[...].astype(MXU_DTYPE), wu_ref[...])
        h = h + up * gate
    if final:
        h = _rms(h, rest[0][...])
    o_ref[...] = h


def _ffn(h, layer, norm, w1, w3, w2, *, ple=None, final_norm=None):
    tokens, d = h.shape
    tm = FFN_ROWS
    hspec = pl.BlockSpec((tm, d), lambda r: (r, 0))

    def of_layer(a):
        rest = a.shape[1:]
        return pl.BlockSpec((None,) + rest, lambda r: (layer,) + (0,) * len(rest),
                            pipeline_mode=pl.Buffered(1))

    gains = lambda g: g.reshape(g.shape[0], 1, d)
    args = [h, gains(norm), w1, w3, w2]
    specs = [hspec] + [of_layer(a) for a in args[1:]]
    if ple is not None:
        p, gp, wg, wu = ple
        p = p.reshape(p.shape[0], tokens, p.shape[-1])
        args += [p, gains(gp), wg, wu]
        specs += [pl.BlockSpec((None, tm, p.shape[-1]), lambda r: (layer, r, 0)),
                  of_layer(args[-3]), of_layer(wg), of_layer(wu)]
    if final_norm is not None:
        args.append(final_norm.reshape(1, d))
        specs.append(_resident((1, d)))
    return pl.pallas_call(
        functools.partial(_ffn_kernel, ple=ple is not None, final=final_norm is not None),
        grid=(tokens // tm,),
        in_specs=specs,
        out_specs=hspec,
        out_shape=jax.ShapeDtypeStruct((tokens, d), F32),
        compiler_params=_params("parallel"),
        name="ffn",
    )(*args)


_EV_COLS = (MLA_Q_RANK, MLA_KV_RANK, HEAD_PAD, GLA_HEADS * GLA_DK, GLA_HEADS * GLA_DK,
            GLA_HEADS * GLA_DV, GLA_HEADS * GLA_DV)
ROT_SHIFT = HEAD_PAD - MLA_ROPE


def _even_proj_kernel(h_ref, gm_ref, win_ref, gq_ref, wq_ref, gkv_ref, wk_ref, wv_ref,
                      wg_ref, bg_ref, cq_ref, sq_ref, ck_ref, sk_ref,
                      q_ref, k_ref, v_ref, lq_ref, lk_ref, lv_ref, lg_ref, lr_ref):
    xn = _rms(h_ref[...], gm_ref[...]).astype(MXU_DTYPE)
    y = _dot(xn, win_ref[...])
    offs = [0]
    for w in _EV_COLS:
        offs.append(offs[-1] + w)
    c_q, c_kv, kr, lq, lk, lv, lr = (
        y[:, offs[i]:offs[i + 1]] for i in range(len(_EV_COLS)))
    g_low = kr

    cqn = _rms(c_q, gq_ref[...]).astype(MXU_DTYPE)
    qf = _dot(cqn, wq_ref[...])
    ckn = _rms(c_kv, gkv_ref[...]).astype(MXU_DTYPE)
    kn = _dot(ckn, wk_ref[...])
    v_lane = lax.broadcasted_iota(jnp.int32, (1, MLA_HEADS * HEAD_PAD), 1) % HEAD_PAD
    v_ref[...] = (_dot(ckn, wv_ref[...])
                  + jnp.where(v_lane == MLA_V, 1.0, 0.0)).astype(v_ref.dtype)
    k_rope = kr * ck_ref[...] + pltpu.roll(kr, ROT_SHIFT, axis=1) * sk_ref[...]
    cq, sq = cq_ref[...], sq_ref[...]
    for hd in range(MLA_HEADS):
        sl = slice(hd * HEAD_PAD, (hd + 1) * HEAD_PAD)
        q_h = qf[:, sl]
        q_ref[:, sl] = (q_h * cq + pltpu.roll(q_h, ROT_SHIFT, axis=1) * sq).astype(q_ref.dtype)
        k_ref[:, sl] = (kn[:, sl] + k_rope).astype(k_ref.dtype)

    gate = _dot(g_low.astype(MXU_DTYPE), wg_ref[...]) + bg_ref[...]
    lg_ref[...] = _log_sigmoid(gate) * (1.0 / GLA_TAU)
    lq_ref[...] = lq
    lk_ref[...] = lk
    lv_ref[...] = lv.astype(lv_ref.dtype)
    lr_ref[...] = lr


def _pad_heads(w, heads, width):
    lead = w.shape[:-1]
    w = w.reshape(lead + (heads, width))
    w = jnp.pad(w, [(0, 0)] * len(lead) + [(0, 0), (0, HEAD_PAD - width)])
    return w.reshape(lead + (heads * HEAD_PAD,))


def _rot_half(w):
    half = MLA_ROPE // 2
    return jnp.concatenate([-w[..., half:], w[..., :half]], axis=-1)


def _head_slot(w_nope, w_rope):
    return jnp.concatenate([w_nope, w_rope, _rot_half(w_rope)], axis=-1)


def _even_proj(h, mix_norm, w_in, q_norm, w_q_up, kv_norm, w_kv_up, w_gate_up, b_gate, *,
               batch):
    tokens, d = h.shape
    seq = tokens // batch
    tm = PROJ_ROWS
    n_s = seq // tm
    splits = [MLA_Q_RANK, MLA_KV_RANK, MLA_ROPE, GLA_HEADS * GLA_DK, GLA_HEADS * GLA_DK,
              GLA_HEADS * GLA_DV, GLA_GATE_RANK, GLA_HEADS * GLA_DV]
    offs = [0]
    for s in splits:
        offs.append(offs[-1] + s)
    wc_q, wc_kv, w_kr, w_lq, w_lk, w_lv, w_gl, w_lr = (
        w_in[:, offs[i]:offs[i + 1]] for i in range(len(splits)))
    gl_lanes = jnp.pad(w_gl, [(0, 0), (0, MLA_NOPE - GLA_GATE_RANK)])
    win = jnp.concatenate([
        wc_q, wc_kv, _head_slot(gl_lanes, w_kr), w_lq, w_lk, w_lv, w_lr], axis=1)

    wq = w_q_up.reshape(MLA_Q_RANK, MLA_HEADS, MLA_NOPE + MLA_ROPE)
    wq_full = _head_slot(wq[..., :MLA_NOPE], wq[..., MLA_NOPE:]).reshape(MLA_Q_RANK, -1)
    wkv = w_kv_up.reshape(MLA_KV_RANK, MLA_HEADS, MLA_NOPE + MLA_V)
    wk = _pad_heads(wkv[..., :MLA_NOPE].reshape(MLA_KV_RANK, -1), MLA_HEADS, MLA_NOPE)
    wv = _pad_heads(wkv[..., MLA_NOPE:].reshape(MLA_KV_RANK, -1), MLA_HEADS, MLA_V)
    wg = jnp.pad(w_gate_up, [(0, HEAD_PAD - GLA_GATE_RANK), (0, 0)])
    bg = b_gate.reshape(1, -1)

    half = MLA_ROPE // 2
    inv = ROPE_BASE ** (-jnp.arange(half, dtype=F32) / half)
    ang = jnp.arange(seq).astype(F32)[:, None] * inv[None, :]
    cos2 = jnp.tile(jnp.cos(ang), (1, 2))
    sin2 = jnp.tile(jnp.sin(ang), (1, 2))
    scale = (MLA_NOPE + MLA_ROPE) ** -0.5 * math.log2(math.e)
    ones = jnp.ones((seq, MLA_NOPE), F32)
    tail = jnp.zeros((seq, HEAD_PAD - MLA_NOPE - MLA_ROPE), F32)
    cq = jnp.concatenate([ones, cos2, tail], axis=1) * scale
    sq = jnp.concatenate([0 * ones, sin2, tail], axis=1) * scale
    ck = jnp.concatenate([0 * ones, cos2, tail], axis=1)
    sk = jnp.concatenate([0 * ones, sin2, tail], axis=1)

    cast = lambda w: w.astype(MXU_DTYPE)
    row = lambda w: pl.BlockSpec((tm, w), lambda r: (r, 0))
    tab = pl.BlockSpec((tm, HEAD_PAD), lambda r: (r % n_s, 0))
    gl_w = GLA_HEADS * GLA_DK
    gv_w = GLA_HEADS * GLA_DV
    qk_w = MLA_HEADS * HEAD_PAD
    outs = [(qk_w, MXU_DTYPE), (qk_w, MXU_DTYPE), (qk_w, MXU_DTYPE),
            (gl_w, F32), (gl_w, F32), (gv_w, MXU_DTYPE), (gl_w, F32), (gv_w, F32)]
    weights = [mix_norm.reshape(1, d), cast(win), q_norm.reshape(1, -1), cast(wq_full),
               kv_norm.reshape(1, -1), cast(wk), cast(wv), cast(wg), bg]
    return pl.pallas_call(
        _even_proj_kernel,
        grid=(tokens // tm,),
        in_specs=[row(d)] + [_resident(w.shape) for w in weights] + [tab] * 4,
        out_specs=[row(w) for w, _ in outs],
        out_shape=[jax.ShapeDtypeStruct((tokens, w), dt) for w, dt in outs],
        compiler_params=_params("parallel"),
        name="even_proj",
    )(h, *weights, cq, sq, ck, sk)


def _attn_kernel(q_ref, k_ref, v_ref, o_ref, s_ref, m_ref, acc_ref, *, tq):
    n_q = q_ref.shape[0] // tq
    heads = m_ref.shape[0]
    row = lax.broadcasted_iota(jnp.int32, (tq, tq), 0)
    col = lax.broadcasted_iota(jnp.int32, (tq, tq), 1)
    causal = col <= row
    sls = [slice(hd * HEAD_PAD, (hd + 1) * HEAD_PAD) for hd in range(heads)]
    lane = lax.broadcasted_iota(jnp.int32, (tq, HEAD_PAD), 1)

    def tile(i):
        return pl.ds(pl.multiple_of(i * tq, tq), tq)

    def scores(hd, qi, j):
        return _dot_nt(q_ref[tile(qi), sls[hd]], k_ref[tile(j), sls[hd]])

    def softmax_pv(hd, s, j, masked):
        if masked:
            s = jnp.where(causal, s, -jnp.inf)
        m = m_ref[hd]
        m_new = jnp.maximum(m, jnp.max(s, axis=1, keepdims=True)).astype(MXU_DTYPE)
        alpha = jnp.exp2(m - m_new.astype(F32))
        p = jnp.exp2(s.astype(MXU_DTYPE) - jnp.concatenate([m_new] * (tq // LANES), axis=1))
        m_ref[hd] = m_new.astype(F32)
        acc_ref[hd] = alpha * acc_ref[hd] + _dot(p, v_ref[tile(j), sls[hd]])

    def step(qi, j, masked, next_qi, next_j):
        s_cur = s_ref[...]
        for hd in range(heads):
            s_next = scores(hd + 1, qi, j) if hd + 1 < heads else scores(0, next_qi, next_j)
            softmax_pv(hd, s_cur, j, masked)
            s_cur = s_next
        s_ref[...] = s_cur

    s_ref[...] = scores(0, 0, 0)

    def q_tile(qi, _):
        m_ref[...] = jnp.full(m_ref.shape, -jnp.inf, F32)
        acc_ref[...] = jnp.zeros(acc_ref.shape, F32)

        def body(j, _):
            step(qi, j, False, qi, j + 1)
            return 0

        lax.fori_loop(0, qi, body, 0)
        step(qi, qi, True, jnp.minimum(qi + 1, n_q - 1), 0)
        for pair in range(heads // 2):
            o_a, o_b = (acc_ref[hd] / acc_ref[hd][:, MLA_V:MLA_V + 1]
                        for hd in (2 * pair, 2 * pair + 1))
            o_ref[tile(qi), pair * LANES:(pair + 1) * LANES] = jnp.where(
                lane < MLA_V, o_a, pltpu.roll(o_b, MLA_V, axis=1)).astype(o_ref.dtype)
        return 0

    lax.fori_loop(0, n_q, q_tile, 0)


def _attention(q, k, v, *, batch):
    tokens = q.shape[0]
    seq = tokens // batch
    tq = ATTN_Q
    q3 = q.reshape(batch, seq, -1)
    k3 = k.reshape(batch, seq, -1)
    v3 = v.reshape(batch, seq, -1)
    nh = ATTN_HEADS
    group = lambda w, **kw: pl.BlockSpec((None, seq, nh * w), lambda b, g: (b, 0, g), **kw)
    once = dict(pipeline_mode=pl.Buffered(1))
    out = pl.pallas_call(
        functools.partial(_attn_kernel, tq=tq),
        grid=(batch, MLA_HEADS // nh),
        in_specs=[group(HEAD_PAD, **once), group(HEAD_PAD), group(HEAD_PAD)],
        out_specs=group(MLA_V, **once),
        out_shape=jax.ShapeDtypeStruct((batch, seq, MLA_HEADS * MLA_V), MXU_DTYPE),
        scratch_shapes=[pltpu.VMEM((tq, tq), F32), pltpu.VMEM((nh, tq, LANES), F32),
                        pltpu.VMEM((nh, tq, HEAD_PAD), F32)],
        compiler_params=_params("parallel", "parallel"),
        name="attention",
    )(q3, k3, v3)
    return out.reshape(tokens, -1)


def _gla_out_kernel(lq_ref, lk_ref, lv_ref, lg_ref, lr_ref, oa_ref, h_ref, gn_ref, wo_ref,
                    o_ref, state_ref, y_ref):
    @pl.when(pl.program_id(1) == 0)
    def _():
        state_ref[...] = jnp.zeros_like(state_ref)

    tt, width = lg_ref.shape
    chunk = GLA_CHUNK
    pos = lax.broadcasted_iota(jnp.int32, (tt, width), 0) & (chunk - 1)
    b = lg_ref[...]
    d = 1
    while d < chunk:
        b = b + jnp.where(pos >= d, pltpu.roll(b, d, axis=0), 0.0)
        d *= 2
    lk = lk_ref[...]
    parity = lambda rows: (lax.broadcasted_iota(jnp.int32, (rows, width), 1) // GLA_DK) % 2
    q_e = lq_ref[...] * (GLA_DK ** -0.5) * jnp.exp(b)
    q_par = [jnp.where(parity(tt) == par, q_e, 0.0).astype(MXU_DTYPE) for par in range(2)]
    k_e = (lk * jnp.exp(-b)).astype(MXU_DTYPE)
    tri = (lax.broadcasted_iota(jnp.int32, (chunk, chunk), 1)
           <= lax.broadcasted_iota(jnp.int32, (chunk, chunk), 0))
    gn = gn_ref[...]
    for c in range(tt // chunk):
        rows = slice(c * chunk, (c + 1) * chunk)
        b_c = b[rows]
        b_last = b_c[chunk - 1:chunk]
        k_end = lk[rows] * jnp.exp(b_last - b_c)
        k_end_par = [jnp.where(parity(chunk) == par, k_end, 0.0).astype(MXU_DTYPE)
                     for par in range(2)]
        decay = jnp.exp(b_last)
        sls = [slice((hd // 2) * LANES, (hd // 2 + 1) * LANES) for hd in range(GLA_HEADS)]
        vss = [slice(hd * GLA_DV, (hd + 1) * GLA_DV) for hd in range(GLA_HEADS)]
        v_cs = [lv_ref[rows, vs] for vs in vss]
        q_hs = [q_par[hd % 2][rows, sls[hd]] for hd in range(GLA_HEADS)]
        atts = [_dot_nt(q_hs[hd], k_e[rows, sls[hd]]) for hd in range(GLA_HEADS)]
        sts = [state_ref[hd] for hd in range(GLA_HEADS)]
        inter = [_dot_nt(q_hs[hd], sts[hd].astype(MXU_DTYPE)) for hd in range(GLA_HEADS)]
        kvs = [_dot_tn(v_cs[hd], k_end_par[hd % 2][:, sls[hd]]) for hd in range(GLA_HEADS)]
        for hd in range(GLA_HEADS):
            state_ref[hd] = sts[hd] * decay[:, sls[hd]] + kvs[hd]
        for hd in range(GLA_HEADS):
            att = jnp.where(tri, atts[hd], 0.0)
            o = _rms(_dot(att.astype(MXU_DTYPE), v_cs[hd]) + inter[hd], gn)
            r = lr_ref[rows, vss[hd]]
            y_ref[rows, vss[hd]] = (o * (r * _sigmoid(r))).astype(y_ref.dtype)
    n_a = oa_ref.shape[1]
    o_ref[...] = (h_ref[...] + _dot(oa_ref[...], wo_ref[:n_a, :])
                  + _dot(y_ref[...], wo_ref[n_a:, :]))


def _gla_out(lq, lk, lv, lg, lr, o_attn, h, out_norm, w_out, *, batch):
    tokens, d = h.shape
    seq = tokens // batch
    tt = GLA_ROWS
    n_s = seq // tt
    row = lambda w: pl.BlockSpec((tt, w), lambda b, i: (b * n_s + i, 0))
    ins = [lq, lk, lv, lg, lr, o_attn, h]
    return pl.pallas_call(
        _gla_out_kernel,
        grid=(batch, n_s),
        in_specs=[row(a.shape[1]) for a in ins] + [_resident((1, GLA_DV)),
                                                   _resident(w_out.shape)],
        out_specs=row(d),
        out_shape=jax.ShapeDtypeStruct((tokens, d), F32),
        scratch_shapes=[pltpu.VMEM((GLA_HEADS, GLA_DV, HEAD_PAD), F32),
                        pltpu.VMEM((tt, GLA_HEADS * GLA_DV), MXU_DTYPE)],
        compiler_params=_params("parallel", "arbitrary"),
        name="gla_out",
    )(*ins, out_norm.reshape(1, -1), w_out.astype(MXU_DTYPE))


def _interleave(src_ref, slab_ref):
    batch, steps, d = src_ref.shape
    for s in range(d // LANES):
        for b in range(batch):
            slab_ref[s, pl.ds(b, steps, stride=batch), :] = src_ref[b, :, s * LANES:(s + 1) * LANES]
    return jnp.concatenate([slab_ref[s] for s in range(d // LANES)], axis=1)


def _deinterleave(x, slab_ref, dst_ref):
    batch, steps, d = dst_ref.shape
    for s in range(d // LANES):
        slab_ref[s] = x[:, s * LANES:(s + 1) * LANES]
    for s in range(d // LANES):
        for b in range(batch):
            dst_ref[b, :, s * LANES:(s + 1) * LANES] = slab_ref[s, pl.ds(b, steps, stride=batch), :]

def _prev_step(cur, prev_rolled, sub):
    cur_rolled = pltpu.roll(cur, SUBLANES // 2, axis=0)
    return jnp.where(sub < SUBLANES // 2, prev_rolled, cur_rolled), cur_rolled


def _odd_kernel(h_ref, gm_ref, win_ref, cw_ref, cb_ref, wgate_ref, bgate_ref, lam_ref,
                sa_ref, bw_ref, cwt_ref, sd_ref, wglu_ref, bglu_ref, wo_ref,
                o_ref,
                halo_ref, rg_a_ref, rg_b_ref, rg_x_ref, rg_carry_ref,
                uh_ref, cx_ref, s5_carry_ref, slab_ref):
    first = pl.program_id(0) == 0

    @pl.when(first)
    def _():
        halo_ref[...] = jnp.zeros_like(halo_ref)
        rg_carry_ref[...] = jnp.zeros_like(rg_carry_ref)
        s5_carry_ref[...] = jnp.zeros_like(s5_carry_ref)
        uh_ref[...] = jnp.zeros_like(uh_ref)

    tm = h_ref.shape[0] * h_ref.shape[1]
    groups = tm // SUBLANES
    h = _interleave(h_ref, slab_ref)
    xn = _rms(h, gm_ref[...]).astype(MXU_DTYPE)
    y = _dot(xn, win_ref[...])
    x_gate, x_rg, u = (y[:, i * RG_WIDTH:(i + 1) * RG_WIDTH] for i in range(3))

    step = SUBLANES // 2

    halo_rows = halo_ref.shape[0]
    ext = jnp.concatenate([halo_ref[...], x_rg], axis=0)
    halo_ref[...] = x_rg[tm - halo_rows:, :]
    cw = cw_ref[...]
    xc = cb_ref[...] + x_rg * cw[RG_CONV - 1:RG_CONV]
    for j in range(RG_CONV - 1):
        back = (RG_CONV - 1 - j) * step
        xc = xc + ext[halo_rows - back:halo_rows - back + tm, :] * cw[j:j + 1]

    xcb = xc.astype(MXU_DTYPE)
    half_rg = RG_WIDTH // 2
    g_lo = _dot(xcb[:, :half_rg], wgate_ref[0])
    g_hi = _dot(xcb[:, half_rg:], wgate_ref[1])
    gate_pre = jnp.concatenate([g_lo[:, :half_rg], g_hi[:, :half_rg],
                                g_lo[:, half_rg:], g_hi[:, half_rg:]], axis=1) + bgate_ref[...]

    ub = u.astype(MXU_DTYPE)
    half_u = S5_WIDTH // 2
    half_x = S5_LANES // 2
    u_halo = uh_ref.shape[0]
    u_ext = jnp.concatenate([uh_ref[...], u], axis=0)
    uh_ref[...] = u[tm - u_halo:, :]
    ub_prev = u_ext[u_halo - step:u_halo - step + tm, :].astype(MXU_DTYPE)
    for hf in range(2):
        cols = slice(hf * half_u, (hf + 1) * half_u)
        cx_ref[:, 2 * hf * half_x:2 * (hf + 1) * half_x] = _dot(
            jnp.concatenate([ub[:, cols], ub_prev[:, cols]], axis=1), bw_ref[hf])

    gates = _sigmoid(gate_pre)
    r_gate, i_gate = gates[:, :RG_WIDTH], gates[:, RG_WIDTH:]
    lam = lam_ref[...]
    softplus_neg = jnp.maximum(-lam, 0.0) + jnp.log1p(jnp.exp(-jnp.abs(lam)))
    log_a = (-RG_C * softplus_neg) * r_gate
    a = jnp.exp(log_a)
    rg_a_ref[...] = a
    var = -jnp.tanh(log_a) * (a * a + 1.0)
    rg_b_ref[...] = jnp.where(var > 0.0, var * lax.rsqrt(var), 0.0) * (i_gate * xc)

    sub = lax.broadcasted_iota(jnp.int32, (SUBLANES, RG_WIDTH), 0)

    x, ra_prev, rb_prev = rg_carry_ref[0], rg_carry_ref[1], rg_carry_ref[2]
    for g in range(groups):
        rows = slice(g * SUBLANES, (g + 1) * SUBLANES)
        a = rg_a_ref[rows, :]
        bx = rg_b_ref[rows, :]
        a_prev, ra_prev = _prev_step(a, ra_prev, sub)
        b_prev, rb_prev = _prev_step(bx, rb_prev, sub)
        x = (a * a_prev) * x + (a * b_prev + bx)
        rg_x_ref[rows, :] = x
    for i, v in enumerate((x, ra_prev, rb_prev)):
        rg_carry_ref[i] = v
    y_c = rg_x_ref[...] * _gelu_tanh(x_gate)

    for lc in range(S5_LANES // SCAN_LANES):
        hf, within = divmod(lc * SCAN_LANES, half_x)
        coef = slice(lc * SCAN_LANES, (lc + 1) * SCAN_LANES)
        re = slice(2 * hf * half_x + within, 2 * hf * half_x + within + SCAN_LANES)
        im = slice(re.start + half_x, re.stop + half_x)
        a2_re, a2_im = sa_ref[0, :, coef], sa_ref[1, :, coef]

        x_re, x_im = s5_carry_ref[0, :, coef], s5_carry_ref[1, :, coef]
        for g in range(groups):
            rows = slice(g * SUBLANES, (g + 1) * SUBLANES)
            x_re, x_im = (a2_re * x_re - a2_im * x_im + cx_ref[rows, re],
                          a2_re * x_im + a2_im * x_re + cx_ref[rows, im])
            cx_ref[rows, re] = x_re
            cx_ref[rows, im] = x_im
        s5_carry_ref[0, :, coef] = x_re
        s5_carry_ref[1, :, coef] = x_im

    ys = jnp.concatenate(
        [_dot(cx_ref[:, 2 * hf * half_x:2 * (hf + 1) * half_x].astype(MXU_DTYPE), cwt_ref[hf])
         for hf in range(2)], axis=1) + sd_ref[...] * u
    ys = _gelu_tanh(ys)
    y_d = ys * _sigmoid(_dot(ys.astype(MXU_DTYPE), wglu_ref[...]) + bglu_ref[...])
    out = (h + _dot(y_c.astype(MXU_DTYPE), wo_ref[:RG_WIDTH, :])
           + _dot(y_d.astype(MXU_DTYPE), wo_ref[RG_WIDTH:, :]))
    _deinterleave(out, slab_ref, o_ref)


def _block_diag(blocks):
    n, r, c = blocks.shape
    eye = jnp.eye(n, dtype=blocks.dtype)
    return (eye[:, None, :, None] * blocks[:, :, None, :]).reshape(n * r, n * c)


def _odd_mixer(h, mix_norm, w_in, conv_w, conv_b, w_a, b_a, w_i, b_i, lam,
               a_re, a_im, log_dt, b_re, b_im, c_re, c_im, d_skip, w_glu, b_glu, w_out, *,
               batch):
    assert batch == SUBLANES // 2
    tokens, d = h.shape
    seq = tokens // batch
    tm = ODD_ROWS
    steps = tm // batch

    dt = jnp.exp(log_dt.astype(F32))[:, None]
    lr, li = a_re.astype(F32), a_im.astype(F32)
    mag = jnp.exp(lr * dt)
    ab_re = mag * jnp.cos(li * dt)
    ab_im = mag * jnp.sin(li * dt)
    den = lr * lr + li * li
    nr, ni = ab_re - 1.0, ab_im
    coef_re = (nr * lr + ni * li) / den
    coef_im = (ni * lr - nr * li) / den
    bb_re = coef_re[..., None] * b_re - coef_im[..., None] * b_im
    bb_im = coef_re[..., None] * b_im + coef_im[..., None] * b_re
    abb_re = ab_re[..., None] * bb_re - ab_im[..., None] * bb_im
    abb_im = ab_re[..., None] * bb_im + ab_im[..., None] * bb_re
    halves = [slice(0, S5_GROUPS // 2), slice(S5_GROUPS // 2, S5_GROUPS)]

    def in_map(m_re, m_im, g):
        return jnp.concatenate([_block_diag(m_re[g].transpose(0, 2, 1)),
                                _block_diag(m_im[g].transpose(0, 2, 1))], axis=1)

    bw = jnp.stack([jnp.concatenate([in_map(bb_re, bb_im, g), in_map(abb_re, abb_im, g)],
                                    axis=0) for g in halves])
    cwt = jnp.stack([jnp.concatenate([_block_diag(c_re[g].transpose(0, 2, 1)),
                                      -_block_diag(c_im[g].transpose(0, 2, 1))], axis=0)
                     for g in halves])
    a2_re = ab_re * ab_re - ab_im * ab_im
    a2_im = 2.0 * ab_re * ab_im
    sa = jnp.stack([jnp.broadcast_to(v.reshape(1, -1), (SUBLANES, S5_LANES))
                    for v in (a2_re, a2_im)])
    hb = RG_BLOCKS // 2
    wgate = jnp.stack([jnp.concatenate([_block_diag(w_a[g]), _block_diag(w_i[g])], axis=1)
                       for g in (slice(0, hb), slice(hb, RG_BLOCKS))])
    bgate = jnp.concatenate([b_a, b_i]).reshape(1, -1)

    cast = lambda w: w.astype(MXU_DTYPE)
    weights = [mix_norm.reshape(1, d), cast(w_in), conv_w, conv_b.reshape(1, -1), cast(wgate),
               bgate, lam.reshape(1, -1), sa, cast(bw), cast(cwt), d_skip.reshape(1, -1),
               cast(w_glu), b_glu.reshape(1, -1), cast(w_out)]
    row = pl.BlockSpec((batch, steps, d), lambda r: (0, r, 0))
    out = pl.pallas_call(
        _odd_kernel,
        grid=(seq // steps,),
        in_specs=[row] + [_resident(w.shape) for w in weights],
        out_specs=row,
        out_shape=jax.ShapeDtypeStruct((batch, seq, d), F32),
        scratch_shapes=[
            pltpu.VMEM((2 * SUBLANES, RG_WIDTH), F32),
            pltpu.VMEM((tm, RG_WIDTH), F32),
            pltpu.VMEM((tm, RG_WIDTH), F32),
            pltpu.VMEM((tm, RG_WIDTH), F32),
            pltpu.VMEM((3, SUBLANES, RG_WIDTH), F32),
            pltpu.VMEM((SUBLANES, S5_WIDTH), F32),
            pltpu.VMEM((tm, 2 * S5_LANES), F32),
            pltpu.VMEM((2, SUBLANES, S5_LANES), F32),
            pltpu.VMEM((d // LANES, tm, LANES), F32),
        ],
        compiler_params=_params("arbitrary"),
        name="odd_mixer",
    )(h.reshape(batch, seq, d), *weights)
    return out.reshape(tokens, d)


def kernel(x, p, ffn_a_norm, ffn_a_w1, ffn_a_w3, ffn_a_w2, mix_norm, ffn_b_norm, ffn_b_w1, ffn_b_w3, ffn_b_w2, ple_norm, ple_w_gate, ple_w_up, ev_w_in, mla_q_norm, mla_w_q_up, mla_kv_norm, mla_w_kv_up, gla_w_gate_up, gla_b_gate, gla_out_norm, ev_w_out, od_w_in, rg_conv_w, rg_conv_b, rg_w_a, rg_b_a, rg_w_i, rg_b_i, rg_lambda, s5_a_re, s5_a_im, s5_log_dt, s5_b_re, s5_b_im, s5_c_re, s5_c_im, s5_d, s5_w_glu, s5_b_glu, od_w_out, final_norm):
    batch, seq, d = x.shape
    depth = p.shape[0]
    h = x.reshape(batch * seq, d)
    cast = lambda w: w.astype(MXU_DTYPE)
    ffn_a = (ffn_a_norm, cast(ffn_a_w1), cast(ffn_a_w3), cast(ffn_a_w2))
    ffn_b = (ffn_b_norm, cast(ffn_b_w1), cast(ffn_b_w3), cast(ffn_b_w2))
    ple = (p, ple_norm, cast(ple_w_gate), cast(ple_w_up))
    for i in range(depth):
        j = i // 2
        h = _ffn(h, i, *ffn_a)
        if i % 2 == 0:
            q, k, v, lq, lk, lv, lg, lr = _even_proj(
                h, mix_norm[i], ev_w_in[j], mla_q_norm[j], mla_w_q_up[j], mla_kv_norm[j],
                mla_w_kv_up[j], gla_w_gate_up[j], gla_b_gate[j], batch=batch)
            o_attn = _attention(q, k, v, batch=batch)
            h = _gla_out(lq, lk, lv, lg, lr, o_attn, h, gla_out_norm[j], ev_w_out[j],
                         batch=batch)
        else:
            h = _odd_mixer(h, mix_norm[i], od_w_in[j], rg_conv_w[j], rg_conv_b[j], rg_w_a[j],
                           rg_b_a[j], rg_w_i[j], rg_b_i[j], rg_lambda[j], s5_a_re[j],
                           s5_a_im[j], s5_log_dt[j], s5_b_re[j], s5_b_im[j], s5_c_re[j],
                           s5_c_im[j], s5_d[j], s5_w_glu[j], s5_b_glu[j], od_w_out[j],
                           batch=batch)
        last = i == depth - 1
        h = _ffn(h, i, *ffn_b, ple=ple, final_norm=final_norm if last else None)
    return h.reshape(batch, seq, d)
```

```python
import functools
import math

import jax
import jax.numpy as jnp
from jax import lax
from jax.experimental import pallas as pl
from jax.experimental.pallas import tpu as pltpu

MXU_DTYPE = jnp.bfloat16
F32 = jnp.float32

EPS = 1e-6
MLA_HEADS = 8
MLA_NOPE = 64
MLA_ROPE = 32
MLA_V = 64
MLA_Q_RANK = 384
MLA_KV_RANK = 256
ROPE_BASE = 10000.0
GLA_HEADS = 4
GLA_DK = 64
GLA_DV = 128
GLA_GATE_RANK = 16
GLA_TAU = 16.0
GLA_CHUNK = 64
RG_WIDTH = 512
RG_BLOCKS = 8
RG_CONV = 4
RG_C = 8.0
S5_GROUP = 16
S5_GROUPS = 32
S5_STATE = 64
S5_WIDTH = S5_GROUP * S5_GROUPS
S5_LANES = S5_GROUPS * S5_STATE

LANES = 128
SUBLANES = 8
MXU_TILE = 256
HEAD_PAD = LANES
VMEM_LIMIT = 56 * 1024 * 1024

FFN_ROWS = 512
FFN_CHUNKS = 2
PROJ_ROWS = 1024
ATTN_Q = 512
ATTN_HEADS = 4
GLA_ROWS = 1024
ODD_ROWS = 512
SCAN_LANES = 512


def _rms(x, g):
    return x * lax.rsqrt(jnp.mean(x * x, axis=-1, keepdims=True) + EPS) * g


def _dot(a, b):
    return jnp.dot(a, b, preferred_element_type=F32)


def _dot_nt(a, b):
    return lax.dot_general(a, b, (((1,), (1,)), ((), ())), preferred_element_type=F32)


def _dot_tn(a, b):
    return lax.dot_general(a, b, (((0,), (0,)), ((), ())), preferred_element_type=F32)


def _sigmoid(x):
    return 1.0 / (1.0 + jnp.exp(-x))


def _gelu_tanh(x):
    return 0.5 * x * (1.0 + jnp.tanh(math.sqrt(2.0 / math.pi) * (x + 0.044715 * (x * x * x))))


def _log_sigmoid(x):
    return jnp.minimum(x, 0.0) - jnp.log1p(jnp.exp(-jnp.abs(x)))


def _resident(shape):
    return pl.BlockSpec(shape, lambda *_: (0,) * len(shape), pipeline_mode=pl.Buffered(1))


def _params(*semantics):
    return pltpu.CompilerParams(dimension_semantics=semantics, vmem_limit_bytes=VMEM_LIMIT)


def _ffn_kernel(*refs, ple, final):
    h_ref, g_ref, w1_ref, w3_ref, w2_ref = refs[:5]
    rest = list(refs[5:])
    o_ref = rest.pop()
    h = h_ref[...]
    xn = _rms(h, g_ref[...]).astype(MXU_DTYPE)
    tiles = w1_ref.shape[1] // MXU_TILE
    bounds = [MXU_TILE * ((tiles * c + FFN_CHUNKS - 1) // FFN_CHUNKS) for c in range(FFN_CHUNKS + 1)]
    y = None
    for lo, hi in zip(bounds[:-1], bounds[1:]):
        a = _dot(xn, w1_ref[:, lo:hi])
        b = _dot(xn, w3_ref[:, lo:hi])
        t = (a * _sigmoid(a) * b).astype(MXU_DTYPE)
        yc = _dot(t, w2_ref[lo:hi, :])
        y = yc if y is None else y + yc
    h = h + 0.5 * y
    if ple:
        p_ref, gp_ref, wg_ref, wu_ref = rest[:4]
        rest = rest[4:]
        gate = _sigmoid(_dot(_rms(h, gp_ref[...]).astype(MXU_DTYPE), wg_ref[...]))
        up = _dot(p_ref[...].astype(MXU_DTYPE), wu_ref[...])
        h = h + up * gate
    if final:
        h = _rms(h, rest[0][...])
    o_ref[...] = h


def _ffn(h, layer, norm, w1, w3, w2, *, ple=None, final_norm=None):
    tokens, d = h.shape
    tm = FFN_ROWS
    hspec = pl.BlockSpec((tm, d), lambda r: (r, 0))

    def of_layer(a):
        rest = a.shape[1:]
        return pl.BlockSpec((None,) + rest, lambda r: (layer,) + (0,) * len(rest),
                            pipeline_mode=pl.Buffered(1))

    gains = lambda g: g.reshape(g.shape[0], 1, d)
    args = [h, gains(norm), w1, w3, w2]
    specs = [hspec] + [of_layer(a) for a in args[1:]]
    if ple is not None:
        p, gp, wg, wu = ple
        p = p.reshape(p.shape[0], tokens, p.shape[-1])
        args += [p, gains(gp), wg, wu]
        specs += [pl.BlockSpec((None, tm, p.shape[-1]), lambda r: (layer, r, 0)),
                  of_layer(args[-3]), of_layer(wg), of_layer(wu)]
    if final_norm is not None:
        args.append(final_norm.reshape(1, d))
        specs.append(_resident((1, d)))
    return pl.pallas_call(
        functools.partial(_ffn_kernel, ple=ple is not None, final=final_norm is not None),
        grid=(tokens // tm,),
        in_specs=specs,
        out_specs=hspec,
        out_shape=jax.ShapeDtypeStruct((tokens, d), F32),
        compiler_params=_params("parallel"),
        name="ffn",
    )(*args)


_EV_COLS = (MLA_Q_RANK, MLA_KV_RANK, HEAD_PAD, GLA_HEADS * GLA_DK, GLA_HEADS * GLA_DK,
            GLA_HEADS * GLA_DV, GLA_HEADS * GLA_DV)
ROT_SHIFT = HEAD_PAD - MLA_ROPE


def _even_proj_kernel(h_ref, gm_ref, win_ref, gq_ref, wq_ref, gkv_ref, wk_ref, wv_ref,
                      wg_ref, bg_ref, cq_ref, sq_ref, ck_ref, sk_ref,
                      q_ref, k_ref, v_ref, lq_ref, lk_ref, lv_ref, lg_ref, lr_ref):
    xn = _rms(h_ref[...], gm_ref[...]).astype(MXU_DTYPE)
    y = _dot(xn, win_ref[...])
    offs = [0]
    for w in _EV_COLS:
        offs.append(offs[-1] + w)
    c_q, c_kv, kr, lq, lk, lv, lr = (
        y[:, offs[i]:offs[i + 1]] for i in range(len(_EV_COLS)))
    g_low = kr

    cqn = _rms(c_q, gq_ref[...]).astype(MXU_DTYPE)
    qf = _dot(cqn, wq_ref[...])
    ckn = _rms(c_kv, gkv_ref[...]).astype(MXU_DTYPE)
    kn = _dot(ckn, wk_ref[...])
    v_lane = lax.broadcasted_iota(jnp.int32, (1, MLA_HEADS * HEAD_PAD), 1) % HEAD_PAD
    v_ref[...] = (_dot(ckn, wv_ref[...])
                  + jnp.where(v_lane == MLA_V, 1.0, 0.0)).astype(v_ref.dtype)
    k_rope = kr * ck_ref[...] + pltpu.roll(kr, ROT_SHIFT, axis=1) * sk_ref[...]
    cq, sq = cq_ref[...], sq_ref[...]
    for hd in range(MLA_HEADS):
        sl = slice(hd * HEAD_PAD, (hd + 1) * HEAD_PAD)
        q_h = qf[:, sl]
        q_ref[:, sl] = (q_h * cq + pltpu.roll(q_h, ROT_SHIFT, axis=1) * sq).astype(q_ref.dtype)
        k_ref[:, sl] = (kn[:, sl] + k_rope).astype(k_ref.dtype)

    gate = _dot(g_low.astype(MXU_DTYPE), wg_ref[...]) + bg_ref[...]
    lg_ref[...] = _log_sigmoid(gate) * (1.0 / GLA_TAU)
    lq_ref[...] = lq
    lk_ref[...] = lk
    lv_ref[...] = lv.astype(lv_ref.dtype)
    lr_ref[...] = lr


def _pad_heads(w, heads, width):
    lead = w.shape[:-1]
    w = w.reshape(lead + (heads, width))
    w = jnp.pad(w, [(0, 0)] * len(lead) + [(0, 0), (0, HEAD_PAD - width)])
    return w.reshape(lead + (heads * HEAD_PAD,))


def _rot_half(w):
    half = MLA_ROPE // 2
    return jnp.concatenate([-w[..., half:], w[..., :half]], axis=-1)


def _head_slot(w_nope, w_rope):
    return jnp.concatenate([w_nope, w_rope, _rot_half(w_rope)], axis=-1)


def _even_proj(h, mix_norm, w_in, q_norm, w_q_up, kv_norm, w_kv_up, w_gate_up, b_gate, *,
               batch):
    tokens, d = h.shape
    seq = tokens // batch
    tm = PROJ_ROWS
    n_s = seq // tm
    splits = [MLA_Q_RANK, MLA_KV_RANK, MLA_ROPE, GLA_HEADS * GLA_DK, GLA_HEADS * GLA_DK,
              GLA_HEADS * GLA_DV, GLA_GATE_RANK, GLA_HEADS * GLA_DV]
    offs = [0]
    for s in splits:
        offs.append(offs[-1] + s)
    wc_q, wc_kv, w_kr, w_lq, w_lk, w_lv, w_gl, w_lr = (
        w_in[:, offs[i]:offs[i + 1]] for i in range(len(splits)))
    gl_lanes = jnp.pad(w_gl, [(0, 0), (0, MLA_NOPE - GLA_GATE_RANK)])
    win = jnp.concatenate([
        wc_q, wc_kv, _head_slot(gl_lanes, w_kr), w_lq, w_lk, w_lv, w_lr], axis=1)

    wq = w_q_up.reshape(MLA_Q_RANK, MLA_HEADS, MLA_NOPE + MLA_ROPE)
    wq_full = _head_slot(wq[..., :MLA_NOPE], wq[..., MLA_NOPE:]).reshape(MLA_Q_RANK, -1)
    wkv = w_kv_up.reshape(MLA_KV_RANK, MLA_HEADS, MLA_NOPE + MLA_V)
    wk = _pad_heads(wkv[..., :MLA_NOPE].reshape(MLA_KV_RANK, -1), MLA_HEADS, MLA_NOPE)
    wv = _pad_heads(wkv[..., MLA_NOPE:].reshape(MLA_KV_RANK, -1), MLA_HEADS, MLA_V)
    wg = jnp.pad(w_gate_up, [(0, HEAD_PAD - GLA_GATE_RANK), (0, 0)])
    bg = b_gate.reshape(1, -1)

    half = MLA_ROPE // 2
    inv = ROPE_BASE ** (-jnp.arange(half, dtype=F32) / half)
    ang = jnp.arange(seq).astype(F32)[:, None] * inv[None, :]
    cos2 = jnp.tile(jnp.cos(ang), (1, 2))
    sin2 = jnp.tile(jnp.sin(ang), (1, 2))
    scale = (MLA_NOPE + MLA_ROPE) ** -0.5 * math.log2(math.e)
    ones = jnp.ones((seq, MLA_NOPE), F32)
    tail = jnp.zeros((seq, HEAD_PAD - MLA_NOPE - MLA_ROPE), F32)
    cq = jnp.concatenate([ones, cos2, tail], axis=1) * scale
    sq = jnp.concatenate([0 * ones, sin2, tail], axis=1) * scale
    ck = jnp.concatenate([0 * ones, cos2, tail], axis=1)
    sk = jnp.concatenate([0 * ones, sin2, tail], axis=1)

    cast = lambda w: w.astype(MXU_DTYPE)
    row = lambda w: pl.BlockSpec((tm, w), lambda r: (r, 0))
    tab = pl.BlockSpec((tm, HEAD_PAD), lambda r: (r % n_s, 0))
    gl_w = GLA_HEADS * GLA_DK
    gv_w = GLA_HEADS * GLA_DV
    qk_w = MLA_HEADS * HEAD_PAD
    outs = [(qk_w, MXU_DTYPE), (qk_w, MXU_DTYPE), (qk_w, MXU_DTYPE),
            (gl_w, F32), (gl_w, F32), (gv_w, MXU_DTYPE), (gl_w, F32), (gv_w, F32)]
    weights = [mix_norm.reshape(1, d), cast(win), q_norm.reshape(1, -1), cast(wq_full),
               kv_norm.reshape(1, -1), cast(wk), cast(wv), cast(wg), bg]
    return pl.pallas_call(
        _even_proj_kernel,
        grid=(tokens // tm,),
        in_specs=[row(d)] + [_resident(w.shape) for w in weights] + [tab] * 4,
        out_specs=[row(w) for w, _ in outs],
        out_shape=[jax.ShapeDtypeStruct((tokens, w), dt) for w, dt in outs],
        compiler_params=_params("parallel"),
        name="even_proj",
    )(h, *weights, cq, sq, ck, sk)


def _attn_kernel(q_ref, k_ref, v_ref, o_ref, s_ref, m_ref, acc_ref, *, tq):
    n_q = q_ref.shape[0] // tq
    heads = m_ref.shape[0]
    row = lax.broadcasted_iota(jnp.int32, (tq, tq), 0)
    col = lax.broadcasted_iota(jnp.int32, (tq, tq), 1)
    causal = col <= row
    sls = [slice(hd * HEAD_PAD, (hd + 1) * HEAD_PAD) for hd in range(heads)]
    lane = lax.broadcasted_iota(jnp.int32, (tq, HEAD_PAD), 1)

    def tile(i):
        return pl.ds(pl.multiple_of(i * tq, tq), tq)

    def scores(hd, qi, j):
        return _dot_nt(q_ref[tile(qi), sls[hd]], k_ref[tile(j), sls[hd]])

    def softmax_pv(hd, s, j, masked):
        if masked:
            s = jnp.where(causal, s, -jnp.inf)
        m = m_ref[hd]
        m_new = jnp.maximum(m, jnp.max(s, axis=1, keepdims=True)).astype(MXU_DTYPE)
        alpha = jnp.exp2(m - m_new.astype(F32))
        p = jnp.exp2(s.astype(MXU_DTYPE) - jnp.concatenate([m_new] * (tq // LANES), axis=1))
        m_ref[hd] = m_new.astype(F32)
        acc_ref[hd] = alpha * acc_ref[hd] + _dot(p, v_ref[tile(j), sls[hd]])

    def step(qi, j, masked, next_qi, next_j):
        s_cur = s_ref[...]
        for hd in range(heads):
            s_next = scores(hd + 1, qi, j) if hd + 1 < heads else scores(0, next_qi, next_j)
            softmax_pv(hd, s_cur, j, masked)
            s_cur = s_next
        s_ref[...] = s_cur

    s_ref[...] = scores(0, 0, 0)

    def q_tile(qi, _):
        m_ref[...] = jnp.full(m_ref.shape, -jnp.inf, F32)
        acc_ref[...] = jnp.zeros(acc_ref.shape, F32)

        def body(j, _):
            step(qi, j, False, qi, j + 1)
            return 0

        lax.fori_loop(0, qi, body, 0)
        step(qi, qi, True, jnp.minimum(qi + 1, n_q - 1), 0)
        for pair in range(heads // 2):
            o_a, o_b = (acc_ref[hd] / acc_ref[hd][:, MLA_V:MLA_V + 1]
                        for hd in (2 * pair, 2 * pair + 1))
            o_ref[tile(qi), pair * LANES:(pair + 1) * LANES] = jnp.where(
                lane < MLA_V, o_a, pltpu.roll(o_b, MLA_V, axis=1)).astype(o_ref.dtype)
        return 0

    lax.fori_loop(0, n_q, q_tile, 0)


def _attention(q, k, v, *, batch):
    tokens = q.shape[0]
    seq = tokens // batch
    tq = ATTN_Q
    q3 = q.reshape(batch, seq, -1)
    k3 = k.reshape(batch, seq, -1)
    v3 = v.reshape(batch, seq, -1)
    nh = ATTN_HEADS
    group = lambda w, **kw: pl.BlockSpec((None, seq, nh * w), lambda b, g: (b, 0, g), **kw)
    once = dict(pipeline_mode=pl.Buffered(1))
    out = pl.pallas_call(
        functools.partial(_attn_kernel, tq=tq),
        grid=(batch, MLA_HEADS // nh),
        in_specs=[group(HEAD_PAD, **once), group(HEAD_PAD), group(HEAD_PAD)],
        out_specs=group(MLA_V, **once),
        out_shape=jax.ShapeDtypeStruct((batch, seq, MLA_HEADS * MLA_V), MXU_DTYPE),
        scratch_shapes=[pltpu.VMEM((tq, tq), F32), pltpu.VMEM((nh, tq, LANES), F32),
                        pltpu.VMEM((nh, tq, HEAD_PAD), F32)],
        compiler_params=_params("parallel", "parallel"),
        name="attention",
    )(q3, k3, v3)
    return out.reshape(tokens, -1)


def _gla_out_kernel(lq_ref, lk_ref, lv_ref, lg_ref, lr_ref, oa_ref, h_ref, gn_ref, wo_ref,
                    o_ref, state_ref, y_ref):
    @pl.when(pl.program_id(1) == 0)
    def _():
        state_ref[...] = jnp.zeros_like(state_ref)

    tt, width = lg_ref.shape
    chunk = GLA_CHUNK
    pos = lax.broadcasted_iota(jnp.int32, (tt, width), 0) & (chunk - 1)
    b = lg_ref[...]
    d = 1
    while d < chunk:
        b = b + jnp.where(pos >= d, pltpu.roll(b, d, axis=0), 0.0)
        d *= 2
    lk = lk_ref[...]
    parity = lambda rows: (lax.broadcasted_iota(jnp.int32, (rows, width), 1) // GLA_DK) % 2
    q_e = lq_ref[...] * (GLA_DK ** -0.5) * jnp.exp(b)
    q_par = [jnp.where(parity(tt) == par, q_e, 0.0).astype(MXU_DTYPE) for par in range(2)]
    k_e = (lk * jnp.exp(-b)).astype(MXU_DTYPE)
    tri = (lax.broadcasted_iota(jnp.int32, (chunk, chunk), 1)
           <= lax.broadcasted_iota(jnp.int32, (chunk, chunk), 0))
    gn = gn_ref[...]
    for c in range(tt // chunk):
        rows = slice(c * chunk, (c + 1) * chunk)
        b_c = b[rows]
        b_last = b_c[chunk - 1:chunk]
        k_end = lk[rows] * jnp.exp(b_last - b_c)
        k_end_par = [jnp.where(parity(chunk) == par, k_end, 0.0).astype(MXU_DTYPE)
                     for par in range(2)]
        decay = jnp.exp(b_last)
        sls = [slice((hd // 2) * LANES, (hd // 2 + 1) * LANES) for hd in range(GLA_HEADS)]
        vss = [slice(hd * GLA_DV, (hd + 1) * GLA_DV) for hd in range(GLA_HEADS)]
        v_cs = [lv_ref[rows, vs] for vs in vss]
        q_hs = [q_par[hd % 2][rows, sls[hd]] for hd in range(GLA_HEADS)]
        atts = [_dot_nt(q_hs[hd], k_e[rows, sls[hd]]) for hd in range(GLA_HEADS)]
        sts = [state_ref[hd] for hd in range(GLA_HEADS)]
        inter = [_dot_nt(q_hs[hd], sts[hd].astype(MXU_DTYPE)) for hd in range(GLA_HEADS)]
        kvs = [_dot_tn(v_cs[hd], k_end_par[hd % 2][:, sls[hd]]) for hd in range(GLA_HEADS)]
        for hd in range(GLA_HEADS):
            state_ref[hd] = sts[hd] * decay[:, sls[hd]] + kvs[hd]
        for hd in range(GLA_HEADS):
            att = jnp.where(tri, atts[hd], 0.0)
            o = _rms(_dot(att.astype(MXU_DTYPE), v_cs[hd]) + inter[hd], gn)
            r = lr_ref[rows, vss[hd]]
            y_ref[rows, vss[hd]] = (o * (r * _sigmoid(r))).astype(y_ref.dtype)
    n_a = oa_ref.shape[1]
    o_ref[...] = (h_ref[...] + _dot(oa_ref[...], wo_ref[:n_a, :])
                  + _dot(y_ref[...], wo_ref[n_a:, :]))


def _gla_out(lq, lk, lv, lg, lr, o_attn, h, out_norm, w_out, *, batch):
    tokens, d = h.shape
    seq = tokens // batch
    tt = GLA_ROWS
    n_s = seq // tt
    row = lambda w: pl.BlockSpec((tt, w), lambda b, i: (b * n_s + i, 0))
    ins = [lq, lk, lv, lg, lr, o_attn, h]
    return pl.pallas_call(
        _gla_out_kernel,
        grid=(batch, n_s),
        in_specs=[row(a.shape[1]) for a in ins] + [_resident((1, GLA_DV)),
                                                   _resident(w_out.shape)],
        out_specs=row(d),
        out_shape=jax.ShapeDtypeStruct((tokens, d), F32),
        scratch_shapes=[pltpu.VMEM((GLA_HEADS, GLA_DV, HEAD_PAD), F32),
                        pltpu.VMEM((tt, GLA_HEADS * GLA_DV), MXU_DTYPE)],
        compiler_params=_params("parallel", "arbitrary"),
        name="gla_out",
    )(*ins, out_norm.reshape(1, -1), w_out.astype(MXU_DTYPE))


def _interleave(src_ref, slab_ref):
    batch, steps, d = src_ref.shape
    for s in range(d // LANES):
        for b in range(batch):
            slab_ref[s, pl.ds(b, steps, stride=batch), :] = src_ref[b, :, s * LANES:(s + 1) * LANES]
    return jnp.concatenate([slab_ref[s] for s in range(d // LANES)], axis=1)


def _deinterleave(x, slab_ref, dst_ref):
    batch, steps, d = dst_ref.shape
    for s in range(d // LANES):
        slab_ref[s] = x[:, s * LANES:(s + 1) * LANES]
    for s in range(d // LANES):
        for b in range(batch):
            dst_ref[b, :, s * LANES:(s + 1) * LANES] = slab_ref[s, pl.ds(b, steps, stride=batch), :]

def _prev_step(cur, prev_rolled, sub):
    cur_rolled = pltpu.roll(cur, SUBLANES // 2, axis=0)
    return jnp.where(sub < SUBLANES // 2, prev_rolled, cur_rolled), cur_rolled


def _odd_kernel(h_ref, gm_ref, win_ref, cw_ref, cb_ref, wgate_ref, bgate_ref, lam_ref,
                sa_ref, bw_ref, cwt_ref, sd_ref, wglu_ref, bglu_ref, wo_ref,
                o_ref,
                halo_ref, rg_a_ref, rg_b_ref, rg_x_ref, rg_carry_ref,
                uh_ref, cx_ref, s5_carry_ref, slab_ref):
    first = pl.program_id(0) == 0

    @pl.when(first)
    def _():
        halo_ref[...] = jnp.zeros_like(halo_ref)
        rg_carry_ref[...] = jnp.zeros_like(rg_carry_ref)
        s5_carry_ref[...] = jnp.zeros_like(s5_carry_ref)
        uh_ref[...] = jnp.zeros_like(uh_ref)

    tm = h_ref.shape[0] * h_ref.shape[1]
    groups = tm // SUBLANES
    h = _interleave(h_ref, slab_ref)
    xn = _rms(h, gm_ref[...]).astype(MXU_DTYPE)
    y = _dot(xn, win_ref[...])
    x_gate, x_rg, u = (y[:, i * RG_WIDTH:(i + 1) * RG_WIDTH] for i in range(3))

    step = SUBLANES // 2

    halo_rows = halo_ref.shape[0]
    ext = jnp.concatenate([halo_ref[...], x_rg], axis=0)
    halo_ref[...] = x_rg[tm - halo_rows:, :]
    cw = cw_ref[...]
    xc = cb_ref[...] + x_rg * cw[RG_CONV - 1:RG_CONV]
    for j in range(RG_CONV - 1):
        back = (RG_CONV - 1 - j) * step
        xc = xc + ext[halo_rows - back:halo_rows - back + tm, :] * cw[j:j + 1]

    xcb = xc.astype(MXU_DTYPE)
    half_rg = RG_WIDTH // 2
    g_lo = _dot(xcb[:, :half_rg], wgate_ref[0])
    g_hi = _dot(xcb[:, half_rg:], wgate_ref[1])
    gate_pre = jnp.concatenate([g_lo[:, :half_rg], g_hi[:, :half_rg],
                                g_lo[:, half_rg:], g_hi[:, half_rg:]], axis=1) + bgate_ref[...]

    ub = u.astype(MXU_DTYPE)
    half_u = S5_WIDTH // 2
    half_x = S5_LANES // 2
    u_halo = uh_ref.shape[0]
    u_ext = jnp.concatenate([uh_ref[...], u], axis=0)
    uh_ref[...] = u[tm - u_halo:, :]
    ub_prev = u_ext[u_halo - step:u_halo - step + tm, :].astype(MXU_DTYPE)
    for hf in range(2):
        cols = slice(hf * half_u, (hf + 1) * half_u)
        cx_ref[:, 2 * hf * half_x:2 * (hf + 1) * half_x] = _dot(
            jnp.concatenate([ub[:, cols], ub_prev[:, cols]], axis=1), bw_ref[hf])

    gates = _sigmoid(gate_pre)
    r_gate, i_gate = gates[:, :RG_WIDTH], gates[:, RG_WIDTH:]
    lam = lam_ref[...]
    softplus_neg = jnp.maximum(-lam, 0.0) + jnp.log1p(jnp.exp(-jnp.abs(lam)))
    log_a = (-RG_C * softplus_neg) * r_gate
    a = jnp.exp(log_a)
    rg_a_ref[...] = a
    var = -jnp.tanh(log_a) * (a * a + 1.0)
    rg_b_ref[...] = jnp.where(var > 0.0, var * lax.rsqrt(var), 0.0) * (i_gate * xc)

    sub = lax.broadcasted_iota(jnp.int32, (SUBLANES, RG_WIDTH), 0)

    x, ra_prev, rb_prev = rg_carry_ref[0], rg_carry_ref[1], rg_carry_ref[2]
    for g in range(groups):
        rows = slice(g * SUBLANES, (g + 1) * SUBLANES)
        a = rg_a_ref[rows, :]
        bx = rg_b_ref[rows, :]
        a_prev, ra_prev = _prev_step(a, ra_prev, sub)
        b_prev, rb_prev = _prev_step(bx, rb_prev, sub)
        x = (a * a_prev) * x + (a * b_prev + bx)
        rg_x_ref[rows, :] = x
    for i, v in enumerate((x, ra_prev, rb_prev)):
        rg_carry_ref[i] = v
    y_c = rg_x_ref[...] * _gelu_tanh(x_gate)

    for lc in range(S5_LANES // SCAN_LANES):
        hf, within = divmod(lc * SCAN_LANES, half_x)
        coef = slice(lc * SCAN_LANES, (lc + 1) * SCAN_LANES)
        re = slice(2 * hf * half_x + within, 2 * hf * half_x + within + SCAN_LANES)
        im = slice(re.start + half_x, re.stop + half_x)
        a2_re, a2_im = sa_ref[0, :, coef], sa_ref[1, :, coef]

        x_re, x_im = s5_carry_ref[0, :, coef], s5_carry_ref[1, :, coef]
        for g in range(groups):
            rows = slice(g * SUBLANES, (g + 1) * SUBLANES)
            x_re, x_im = (a2_re * x_re - a2_im * x_im + cx_ref[rows, re],
                          a2_re * x_im + a2_im * x_re + cx_ref[rows, im])
            cx_ref[rows, re] = x_re
            cx_ref[rows, im] = x_im
        s5_carry_ref[0, :, coef] = x_re
        s5_carry_ref[1, :, coef] = x_im

    ys = jnp.concatenate(
        [_dot(cx_ref[:, 2 * hf * half_x:2 * (hf + 1) * half_x].astype(MXU_DTYPE), cwt_ref[hf])
         for hf in range(2)], axis=1) + sd_ref[...] * u
    ys = _gelu_tanh(ys)
    y_d = ys * _sigmoid(_dot(ys.astype(MXU_DTYPE), wglu_ref[...]) + bglu_ref[...])
    out = (h + _dot(y_c.astype(MXU_DTYPE), wo_ref[:RG_WIDTH, :])
           + _dot(y_d.astype(MXU_DTYPE), wo_ref[RG_WIDTH:, :]))
    _deinterleave(out, slab_ref, o_ref)


def _block_diag(blocks):
    n, r, c = blocks.shape
    eye = jnp.eye(n, dtype=blocks.dtype)
    return (eye[:, None, :, None] * blocks[:, :, None, :]).reshape(n * r, n * c)


def _odd_mixer(h, mix_norm, w_in, conv_w, conv_b, w_a, b_a, w_i, b_i, lam,
               a_re, a_im, log_dt, b_re, b_im, c_re, c_im, d_skip, w_glu, b_glu, w_out, *,
               batch):
    assert batch == SUBLANES // 2
    tokens, d = h.shape
    seq = tokens // batch
    tm = ODD_ROWS
    steps = tm // batch

    dt = jnp.exp(log_dt.astype(F32))[:, None]
    lr, li = a_re.astype(F32), a_im.astype(F32)
    mag = jnp.exp(lr * dt)
    ab_re = mag * jnp.cos(li * dt)
    ab_im = mag * jnp.sin(li * dt)
    den = lr * lr + li * li
    nr, ni = ab_re - 1.0, ab_im
    coef_re = (nr * lr + ni * li) / den
    coef_im = (ni * lr - nr * li) / den
    bb_re = coef_re[..., None] * b_re - coef_im[..., None] * b_im
    bb_im = coef_re[..., None] * b_im + coef_im[..., None] * b_re
    abb_re = ab_re[..., None] * bb_re - ab_im[..., None] * bb_im
    abb_im = ab_re[..., None] * bb_im + ab_im[..., None] * bb_re
    halves = [slice(0, S5_GROUPS // 2), slice(S5_GROUPS // 2, S5_GROUPS)]

    def in_map(m_re, m_im, g):
        return jnp.concatenate([_block_diag(m_re[g].transpose(0, 2, 1)),
                                _block_diag(m_im[g].transpose(0, 2, 1))], axis=1)

    bw = jnp.stack([jnp.concatenate([in_map(bb_re, bb_im, g), in_map(abb_re, abb_im, g)],
                                    axis=0) for g in halves])
    cwt = jnp.stack([jnp.concatenate([_block_diag(c_re[g].transpose(0, 2, 1)),
                                      -_block_diag(c_im[g].transpose(0, 2, 1))], axis=0)
                     for g in halves])
    a2_re = ab_re * ab_re - ab_im * ab_im
    a2_im = 2.0 * ab_re * ab_im
    sa = jnp.stack([jnp.broadcast_to(v.reshape(1, -1), (SUBLANES, S5_LANES))
                    for v in (a2_re, a2_im)])
    hb = RG_BLOCKS // 2
    wgate = jnp.stack([jnp.concatenate([_block_diag(w_a[g]), _block_diag(w_i[g])], axis=1)
                       for g in (slice(0, hb), slice(hb, RG_BLOCKS))])
    bgate = jnp.concatenate([b_a, b_i]).reshape(1, -1)

    cast = lambda w: w.astype(MXU_DTYPE)
    weights = [mix_norm.reshape(1, d), cast(w_in), conv_w, conv_b.reshape(1, -1), cast(wgate),
               bgate, lam.reshape(1, -1), sa, cast(bw), cast(cwt), d_skip.reshape(1, -1),
               cast(w_glu), b_glu.reshape(1, -1), cast(w_out)]
    row = pl.BlockSpec((batch, steps, d), lambda r: (0, r, 0))
    out = pl.pallas_call(
        _odd_kernel,
        grid=(seq // steps,),
        in_specs=[row] + [_resident(w.shape) for w in weights],
        out_specs=row,
        out_shape=jax.ShapeDtypeStruct((batch, seq, d), F32),
        scratch_shapes=[
            pltpu.VMEM((2 * SUBLANES, RG_WIDTH), F32),
            pltpu.VMEM((tm, RG_WIDTH), F32),
            pltpu.VMEM((tm, RG_WIDTH), F32),
            pltpu.VMEM((tm, RG_WIDTH), F32),
            pltpu.VMEM((3, SUBLANES, RG_WIDTH), F32),
            pltpu.VMEM((SUBLANES, S5_WIDTH), F32),
            pltpu.VMEM((tm, 2 * S5_LANES), F32),
            pltpu.VMEM((2, SUBLANES, S5_LANES), F32),
            pltpu.VMEM((d // LANES, tm, LANES), F32),
        ],
        compiler_params=_params("arbitrary"),
        name="odd_mixer",
    )(h.reshape(batch, seq, d), *weights)
    return out.reshape(tokens, d)


def kernel(x, p, ffn_a_norm, ffn_a_w1, ffn_a_w3, ffn_a_w2, mix_norm, ffn_b_norm, ffn_b_w1, ffn_b_w3, ffn_b_w2, ple_norm, ple_w_gate, ple_w_up, ev_w_in, mla_q_norm, mla_w_q_up, mla_kv_norm, mla_w_kv_up, gla_w_gate_up, gla_b_gate, gla_out_norm, ev_w_out, od_w_in, rg_conv_w, rg_conv_b, rg_w_a, rg_b_a, rg_w_i, rg_b_i, rg_lambda, s5_a_re, s5_a_im, s5_log_dt, s5_b_re, s5_b_im, s5_c_re, s5_c_im, s5_d, s5_w_glu, s5_b_glu, od_w_out, final_norm):
    batch, seq, d = x.shape
    depth = p.shape[0]
    h = x.reshape(batch * seq, d)
    cast = lambda w: w.astype(MXU_DTYPE)
    ffn_a = (ffn_a_norm, cast(ffn_a_w1), cast(ffn_a_w3), cast(ffn_a_w2))
    ffn_b = (ffn_b_norm, cast(ffn_b_w1), cast(ffn_b_w3), cast(ffn_b_w2))
    ple = (p, ple_norm, cast(ple_w_gate), cast(ple_w_up))
    for i in range(depth):
        j = i // 2
        h = _ffn(h, i, *ffn_a)
        if i % 2 == 0:
            q, k, v, lq, lk, lv, lg, lr = _even_proj(
                h, mix_norm[i], ev_w_in[j], mla_q_norm[j], mla_w_q_up[j], mla_kv_norm[j],
                mla_w_kv_up[j], gla_w_gate_up[j], gla_b_gate[j], batch=batch)
            o_attn = _attention(q, k, v, batch=batch)
            h = _gla_out(lq, lk, lv, lg, lr, o_attn, h, gla_out_norm[j], ev_w_out[j],
                         batch=batch)
        else:
            h = _odd_mixer(h, mix_norm[i], od_w_in[j], rg_conv_w[j], rg_conv_b[j], rg_w_a[j],
                           rg_b_a[j], rg_w_i[j], rg_b_i[j], rg_lambda[j], s5_a_re[j],
                           s5_a_im[j], s5_log_dt[j], s5_b_re[j], s5_b_im[j], s5_c_re[j],
                           s5_c_im[j], s5_d[j], s5_w_glu[j], s5_b_glu[j], od_w_out[j],
                           batch=batch)
        last = i == depth - 1
        h = _ffn(h, i, *ffn_b, ple=ple, final_norm=final_norm if last else None)
    return h.reshape(batch, seq, d)
```

```python
import functools
import math

import jax
import jax.numpy as jnp
from jax import lax
from jax.experimental import pallas as pl
from jax.experimental.pallas import tpu as pltpu

MXU_DTYPE = jnp.bfloat16
F32 = jnp.float32

EPS = 1e-6
MLA_HEADS = 8
MLA_NOPE = 64
MLA_ROPE = 32
MLA_V = 64
MLA_Q_RANK = 384
MLA_KV_RANK = 256
ROPE_BASE = 10000.0
GLA_HEADS = 4
GLA_DK = 64
GLA_DV = 128
GLA_GATE_RANK = 16
GLA_TAU = 16.0
GLA_CHUNK = 64
RG_WIDTH = 512
RG_BLOCKS = 8
RG_CONV = 4
RG_C = 8.0
S5_GROUP = 16
S5_GROUPS = 32
S5_STATE = 64
S5_WIDTH = S5_GROUP * S5_GROUPS
S5_LANES = S5_GROUPS * S5_STATE

LANES = 128
SUBLANES = 8
MXU_TILE = 256
HEAD_PAD = LANES
VMEM_LIMIT = 56 * 1024 * 1024

FFN_ROWS = 512
FFN_CHUNKS = 2
PROJ_ROWS = 1024
ATTN_Q = 512
ATTN_HEADS = 4
GLA_ROWS = 1024
ODD_ROWS = 512
SCAN_LANES = 512


def _rms(x, g):
    return x * lax.rsqrt(jnp.mean(x * x, axis=-1, keepdims=True) + EPS) * g


def _dot(a, b):
    return jnp.dot(a, b, preferred_element_type=F32)


def _dot_nt(a, b):
    return lax.dot_general(a, b, (((1,), (1,)), ((), ())), preferred_element_type=F32)


def _dot_tn(a, b):
    return lax.dot_general(a, b, (((0,), (0,)), ((), ())), preferred_element_type=F32)


def _sigmoid(x):
    return 1.0 / (1.0 + jnp.exp(-x))


def _gelu_tanh(x):
    return 0.5 * x * (1.0 + jnp.tanh(math.sqrt(2.0 / math.pi) * (x + 0.044715 * (x * x * x))))


def _log_sigmoid(x):
    return jnp.minimum(x, 0.0) - jnp.log1p(jnp.exp(-jnp.abs(x)))


def _resident(shape):
    return pl.BlockSpec(shape, lambda *_: (0,) * len(shape), pipeline_mode=pl.Buffered(1))


def _params(*semantics):
    return pltpu.CompilerParams(dimension_semantics=semantics, vmem_limit_bytes=VMEM_LIMIT)


def _ffn_kernel(*refs, ple, final):
    h_ref, g_ref, w1_ref, w3_ref, w2_ref = refs[:5]
    rest = list(refs[5:])
    o_ref = rest.pop()
    h = h_ref[...]
    xn = _rms(h, g_ref[...]).astype(MXU_DTYPE)
    tiles = w1_ref.shape[1] // MXU_TILE
    bounds = [MXU_TILE * ((tiles * c + FFN_CHUNKS - 1) // FFN_CHUNKS) for c in range(FFN_CHUNKS + 1)]
    y = None
    for lo, hi in zip(bounds[:-1], bounds[1:]):
        a = _dot(xn, w1_ref[:, lo:hi])
        b = _dot(xn, w3_ref[:, lo:hi])
        t = (a * _sigmoid(a) * b).astype(MXU_DTYPE)
        yc = _dot(t, w2_ref[lo:hi, :])
        y = yc if y is None else y + yc
    h = h + 0.5 * y
    if ple:
        p_ref, gp_ref, wg_ref, wu_ref = rest[:4]
        rest = rest[4:]
        gate = _sigmoid(_dot(_rms(h, gp_ref[...]).astype(MXU_DTYPE), wg_ref[...]))
        up = _dot(p_ref[...].astype(MXU_DTYPE), wu_ref[...])
        h = h + up * gate
    if final:
        h = _rms(h, rest[0][...])
    o_ref[...] = h


def _ffn(h, layer, norm, w1, w3, w2, *, ple=None, final_norm=None):
    tokens, d = h.shape
    tm = FFN_ROWS
    hspec = pl.BlockSpec((tm, d), lambda r: (r, 0))

    def of_layer(a):
        rest = a.shape[1:]
        return pl.BlockSpec((None,) + rest, lambda r: (layer,) + (0,) * len(rest),
                            pipeline_mode=pl.Buffered(1))

    gains = lambda g: g.reshape(g.shape[0], 1, d)
    args = [h, gains(norm), w1, w3, w2]
    specs = [hspec] + [of_layer(a) for a in args[1:]]
    if ple is not None:
        p, gp, wg, wu = ple
        p = p.reshape(p.shape[0], tokens, p.shape[-1])
        args += [p, gains(gp), wg, wu]
        specs += [pl.BlockSpec((None, tm, p.shape[-1]), lambda r: (layer, r, 0)),
                  of_layer(args[-3]), of_layer(wg), of_layer(wu)]
    if final_norm is not None:
        args.append(final_norm.reshape(1, d))
        specs.append(_resident((1, d)))
    return pl.pallas_call(
        functools.partial(_ffn_kernel, ple=ple is not None, final=final_norm is not None),
        grid=(tokens // tm,),
        in_specs=specs,
        out_specs=hspec,
        out_shape=jax.ShapeDtypeStruct((tokens, d), F32),
        compiler_params=_params("parallel"),
        name="ffn",
    )(*args)


_EV_COLS = (MLA_Q_RANK, MLA_KV_RANK, HEAD_PAD, GLA_HEADS * GLA_DK, GLA_HEADS * GLA_DK,
            GLA_HEADS * GLA_DV, GLA_HEADS * GLA_DV)
ROT_SHIFT = HEAD_PAD - MLA_ROPE


def _even_proj_kernel(h_ref, gm_ref, win_ref, gq_ref, wq_ref, gkv_ref, wk_ref, wv_ref,
                      wg_ref, bg_ref, cq_ref, sq_ref, ck_ref, sk_ref,
                      q_ref, k_ref, v_ref, lq_ref, lk_ref, lv_ref, lg_ref, lr_ref):
    xn = _rms(h_ref[...], gm_ref[...]).astype(MXU_DTYPE)
    y = _dot(xn, win_ref[...])
    offs = [0]
    for w in _EV_COLS:
        offs.append(offs[-1] + w)
    c_q, c_kv, kr, lq, lk, lv, lr = (
        y[:, offs[i]:offs[i + 1]] for i in range(len(_EV_COLS)))
    g_low = kr

    cqn = _rms(c_q, gq_ref[...]).astype(MXU_DTYPE)
    qf = _dot(cqn, wq_ref[...])
    ckn = _rms(c_kv, gkv_ref[...]).astype(MXU_DTYPE)
    kn = _dot(ckn, wk_ref[...])
    v_lane = lax.broadcasted_iota(jnp.int32, (1, MLA_HEADS * HEAD_PAD), 1) % HEAD_PAD
    v_ref[...] = (_dot(ckn, wv_ref[...])
                  + jnp.where(v_lane == MLA_V, 1.0, 0.0)).astype(v_ref.dtype)
    k_rope = kr * ck_ref[...] + pltpu.roll(kr, ROT_SHIFT, axis=1) * sk_ref[...]
    cq, sq = cq_ref[...], sq_ref[...]
    for hd in range(MLA_HEADS):
        sl = slice(hd * HEAD_PAD, (hd + 1) * HEAD_PAD)
        q_h = qf[:, sl]
        q_ref[:, sl] = (q_h * cq + pltpu.roll(q_h, ROT_SHIFT, axis=1) * sq).astype(q_ref.dtype)
        k_ref[:, sl] = (kn[:, sl] + k_rope).astype(k_ref.dtype)

    gate = _dot(g_low.astype(MXU_DTYPE), wg_ref[...]) + bg_ref[...]
    lg_ref[...] = _log_sigmoid(gate) * (1.0 / GLA_TAU)
    lq_ref[...] = lq
    lk_ref[...] = lk
    lv_ref[...] = lv.astype(lv_ref.dtype)
    lr_ref[...] = lr


def _pad_heads(w, heads, width):
    lead = w.shape[:-1]
    w = w.reshape(lead + (heads, width))
    w = jnp.pad(w, [(0, 0)] * len(lead) + [(0, 0), (0, HEAD_PAD - width)])
    return w.reshape(lead + (heads * HEAD_PAD,))


def _rot_half(w):
    half = MLA_ROPE // 2
    return jnp.concatenate([-w[..., half:], w[..., :half]], axis=-1)


def _head_slot(w_nope, w_rope):
    return jnp.concatenate([w_nope, w_rope, _rot_half(w_rope)], axis=-1)


def _even_proj(h, mix_norm, w_in, q_norm, w_q_up, kv_norm, w_kv_up, w_gate_up, b_gate, *,
               batch):
    tokens, d = h.shape
    seq = tokens // batch
    tm = PROJ_ROWS
    n_s = seq // tm
    splits = [MLA_Q_RANK, MLA_KV_RANK, MLA_ROPE, GLA_HEADS * GLA_DK, GLA_HEADS * GLA_DK,
              GLA_HEADS * GLA_DV, GLA_GATE_RANK, GLA_HEADS * GLA_DV]
    offs = [0]
    for s in splits:
        offs.append(offs[-1] + s)
    wc_q, wc_kv, w_kr, w_lq, w_lk, w_lv, w_gl, w_lr = (
        w_in[:, offs[i]:offs[i + 1]] for i in range(len(splits)))
    gl_lanes = jnp.pad(w_gl, [(0, 0), (0, MLA_NOPE - GLA_GATE_RANK)])
    win = jnp.concatenate([
        wc_q, wc_kv, _head_slot(gl_lanes, w_kr), w_lq, w_lk, w_lv, w_lr], axis=1)

    wq = w_q_up.reshape(MLA_Q_RANK, MLA_HEADS, MLA_NOPE + MLA_ROPE)
    wq_full = _head_slot(wq[..., :MLA_NOPE], wq[..., MLA_NOPE:]).reshape(MLA_Q_RANK, -1)
    wkv = w_kv_up.reshape(MLA_KV_RANK, MLA_HEADS, MLA_NOPE + MLA_V)
    wk = _pad_heads(wkv[..., :MLA_NOPE].reshape(MLA_KV_RANK, -1), MLA_HEADS, MLA_NOPE)
    wv = _pad_heads(wkv[..., MLA_NOPE:].reshape(MLA_KV_RANK, -1), MLA_HEADS, MLA_V)
    wg = jnp.pad(w_gate_up, [(0, HEAD_PAD - GLA_GATE_RANK), (0, 0)])
    bg = b_gate.reshape(1, -1)

    half = MLA_ROPE // 2
    inv = ROPE_BASE ** (-jnp.arange(half, dtype=F32) / half)
    ang = jnp.arange(seq).astype(F32)[:, None] * inv[None, :]
    cos2 = jnp.tile(jnp.cos(ang), (1, 2))
    sin2 = jnp.tile(jnp.sin(ang), (1, 2))
    scale = (MLA_NOPE + MLA_ROPE) ** -0.5 * math.log2(math.e)
    ones = jnp.ones((seq, MLA_NOPE), F32)
    tail = jnp.zeros((seq, HEAD_PAD - MLA_NOPE - MLA_ROPE), F32)
    cq = jnp.concatenate([ones, cos2, tail], axis=1) * scale
    sq = jnp.concatenate([0 * ones, sin2, tail], axis=1) * scale
    ck = jnp.concatenate([0 * ones, cos2, tail], axis=1)
    sk = jnp.concatenate([0 * ones, sin2, tail], axis=1)

    cast = lambda w: w.astype(MXU_DTYPE)
    row = lambda w: pl.BlockSpec((tm, w), lambda r: (r, 0))
    tab = pl.BlockSpec((tm, HEAD_PAD), lambda r: (r % n_s, 0))
    gl_w = GLA_HEADS * GLA_DK
    gv_w = GLA_HEADS * GLA_DV
    qk_w = MLA_HEADS * HEAD_PAD
    outs = [(qk_w, MXU_DTYPE), (qk_w, MXU_DTYPE), (qk_w, MXU_DTYPE),
            (gl_w, F32), (gl_w, F32), (gv_w, MXU_DTYPE), (gl_w, F32), (gv_w, F32)]
    weights = [mix_norm.reshape(1, d), cast(win), q_norm.reshape(1, -1), cast(wq_full),
               kv_norm.reshape(1, -1), cast(wk), cast(wv), cast(wg), bg]
    return pl.pallas_call(
        _even_proj_kernel,
        grid=(tokens // tm,),
        in_specs=[row(d)] + [_resident(w.shape) for w in weights] + [tab] * 4,
        out_specs=[row(w) for w, _ in outs],
        out_shape=[jax.ShapeDtypeStruct((tokens, w), dt) for w, dt in outs],
        compiler_params=_params("parallel"),
        name="even_proj",
    )(h, *weights, cq, sq, ck, sk)


def _attn_kernel(q_ref, k_ref, v_ref, o_ref, s_ref, m_ref, acc_ref, *, tq):
    n_q = q_ref.shape[0] // tq
    heads = m_ref.shape[0]
    row = lax.broadcasted_iota(jnp.int32, (tq, tq), 0)
    col = lax.broadcasted_iota(jnp.int32, (tq, tq), 1)
    causal = col <= row
    sls = [slice(hd * HEAD_PAD, (hd + 1) * HEAD_PAD) for hd in range(heads)]
    lane = lax.broadcasted_iota(jnp.int32, (tq, HEAD_PAD), 1)

    def tile(i):
        return pl.ds(pl.multiple_of(i * tq, tq), tq)

    def scores(hd, qi, j):
        return _dot_nt(q_ref[tile(qi), sls[hd]], k_ref[tile(j), sls[hd]])

    def softmax_pv(hd, s, j, masked):
        if masked:
            s = jnp.where(causal, s, -jnp.inf)
        m = m_ref[hd]
        s = s.astype(MXU_DTYPE)
        m_new = jnp.maximum(m.astype(MXU_DTYPE), jnp.max(s, axis=1, keepdims=True))
        alpha = jnp.exp2(m - m_new.astype(F32))
        p = jnp.exp2(s - jnp.concatenate([m_new] * (tq // LANES), axis=1))
        m_ref[hd] = m_new.astype(F32)
        acc_ref[hd] = alpha * acc_ref[hd] + _dot(p, v_ref[tile(j), sls[hd]])

    def step(qi, j, masked, next_qi, next_j):
        s_cur = s_ref[...]
        for hd in range(heads):
            s_next = scores(hd + 1, qi, j) if hd + 1 < heads else scores(0, next_qi, next_j)
            softmax_pv(hd, s_cur, j, masked)
            s_cur = s_next
        s_ref[...] = s_cur

    s_ref[...] = scores(0, 0, 0)

    def q_tile(qi, _):
        m_ref[...] = jnp.full(m_ref.shape, -jnp.inf, F32)
        acc_ref[...] = jnp.zeros(acc_ref.shape, F32)

        def body(j, _):
            step(qi, j, False, qi, j + 1)
            return 0

        lax.fori_loop(0, qi, body, 0)
        step(qi, qi, True, jnp.minimum(qi + 1, n_q - 1), 0)
        for pair in range(heads // 2):
            o_a, o_b = (acc_ref[hd] / acc_ref[hd][:, MLA_V:MLA_V + 1]
                        for hd in (2 * pair, 2 * pair + 1))
            o_ref[tile(qi), pair * LANES:(pair + 1) * LANES] = jnp.where(
                lane < MLA_V, o_a, pltpu.roll(o_b, MLA_V, axis=1)).astype(o_ref.dtype)
        return 0

    lax.fori_loop(0, n_q, q_tile, 0)


def _attention(q, k, v, *, batch):
    tokens = q.shape[0]
    seq = tokens // batch
    tq = ATTN_Q
    q3 = q.reshape(batch, seq, -1)
    k3 = k.reshape(batch, seq, -1)
    v3 = v.reshape(batch, seq, -1)
    nh = ATTN_HEADS
    group = lambda w, **kw: pl.BlockSpec((None, seq, nh * w), lambda b, g: (b, 0, g), **kw)
    once = dict(pipeline_mode=pl.Buffered(1))
    out = pl.pallas_call(
        functools.partial(_attn_kernel, tq=tq),
        grid=(batch, MLA_HEADS // nh),
        in_specs=[group(HEAD_PAD, **once), group(HEAD_PAD), group(HEAD_PAD)],
        out_specs=group(MLA_V, **once),
        out_shape=jax.ShapeDtypeStruct((batch, seq, MLA_HEADS * MLA_V), MXU_DTYPE),
        scratch_shapes=[pltpu.VMEM((tq, tq), F32), pltpu.VMEM((nh, tq, LANES), F32),
                        pltpu.VMEM((nh, tq, HEAD_PAD), F32)],
        compiler_params=_params("parallel", "parallel"),
        name="attention",
    )(q3, k3, v3)
    return out.reshape(tokens, -1)


def _gla_out_kernel(lq_ref, lk_ref, lv_ref, lg_ref, lr_ref, oa_ref, h_ref, gn_ref, wo_ref,
                    o_ref, state_ref, y_ref):
    @pl.when(pl.program_id(1) == 0)
    def _():
        state_ref[...] = jnp.zeros_like(state_ref)

    tt, width = lg_ref.shape
    chunk = GLA_CHUNK
    pos = lax.broadcasted_iota(jnp.int32, (tt, width), 0) & (chunk - 1)
    b = lg_ref[...]
    d = 1
    while d < chunk:
        b = b + jnp.where(pos >= d, pltpu.roll(b, d, axis=0), 0.0)
        d *= 2
    lk = lk_ref[...]
    parity = lambda rows: (lax.broadcasted_iota(jnp.int32, (rows, width), 1) // GLA_DK) % 2
    q_e = lq_ref[...] * (GLA_DK ** -0.5) * jnp.exp(b)
    q_par = [jnp.where(parity(tt) == par, q_e, 0.0).astype(MXU_DTYPE) for par in range(2)]
    k_e = (lk * jnp.exp(-b)).astype(MXU_DTYPE)
    tri = (lax.broadcasted_iota(jnp.int32, (chunk, chunk), 1)
           <= lax.broadcasted_iota(jnp.int32, (chunk, chunk), 0))
    gn = gn_ref[...]
    for c in range(tt // chunk):
        rows = slice(c * chunk, (c + 1) * chunk)
        b_c = b[rows]
        b_last = b_c[chunk - 1:chunk]
        k_end = lk[rows] * jnp.exp(b_last - b_c)
        k_end_par = [jnp.where(parity(chunk) == par, k_end, 0.0).astype(MXU_DTYPE)
                     for par in range(2)]
        decay = jnp.exp(b_last)
        sls = [slice((hd // 2) * LANES, (hd // 2 + 1) * LANES) for hd in range(GLA_HEADS)]
        vss = [slice(hd * GLA_DV, (hd + 1) * GLA_DV) for hd in range(GLA_HEADS)]
        v_cs = [lv_ref[rows, vs] for vs in vss]
        q_hs = [q_par[hd % 2][rows, sls[hd]] for hd in range(GLA_HEADS)]
        atts = [_dot_nt(q_hs[hd], k_e[rows, sls[hd]]) for hd in range(GLA_HEADS)]
        sts = [state_ref[hd] for hd in range(GLA_HEADS)]
        inter = [_dot_nt(q_hs[hd], sts[hd].astype(MXU_DTYPE)) for hd in range(GLA_HEADS)]
        kvs = [_dot_tn(v_cs[hd], k_end_par[hd % 2][:, sls[hd]]) for hd in range(GLA_HEADS)]
        for hd in range(GLA_HEADS):
            state_ref[hd] = sts[hd] * decay[:, sls[hd]] + kvs[hd]
        for hd in range(GLA_HEADS):
            att = jnp.where(tri, atts[hd], 0.0)
            o = _rms(_dot(att.astype(MXU_DTYPE), v_cs[hd]) + inter[hd], gn)
            r = lr_ref[rows, vss[hd]]
            y_ref[rows, vss[hd]] = (o * (r * _sigmoid(r))).astype(y_ref.dtype)
    n_a = oa_ref.shape[1]
    o_ref[...] = (h_ref[...] + _dot(oa_ref[...], wo_ref[:n_a, :])
                  + _dot(y_ref[...], wo_ref[n_a:, :]))


def _gla_out(lq, lk, lv, lg, lr, o_attn, h, out_norm, w_out, *, batch):
    tokens, d = h.shape
    seq = tokens // batch
    tt = GLA_ROWS
    n_s = seq // tt
    row = lambda w: pl.BlockSpec((tt, w), lambda b, i: (b * n_s + i, 0))
    ins = [lq, lk, lv, lg, lr, o_attn, h]
    return pl.pallas_call(
        _gla_out_kernel,
        grid=(batch, n_s),
        in_specs=[row(a.shape[1]) for a in ins] + [_resident((1, GLA_DV)),
                                                   _resident(w_out.shape)],
        out_specs=row(d),
        out_shape=jax.ShapeDtypeStruct((tokens, d), F32),
        scratch_shapes=[pltpu.VMEM((GLA_HEADS, GLA_DV, HEAD_PAD), F32),
                        pltpu.VMEM((tt, GLA_HEADS * GLA_DV), MXU_DTYPE)],
        compiler_params=_params("parallel", "arbitrary"),
        name="gla_out",
    )(*ins, out_norm.reshape(1, -1), w_out.astype(MXU_DTYPE))


def _interleave(src_ref, slab_ref):
    batch, steps, d = src_ref.shape
    for s in range(d // LANES):
        for b in range(batch):
            slab_ref[s, pl.ds(b, steps, stride=batch), :] = src_ref[b, :, s * LANES:(s + 1) * LANES]
    return jnp.concatenate([slab_ref[s] for s in range(d // LANES)], axis=1)


def _deinterleave(x, slab_ref, dst_ref):
    batch, steps, d = dst_ref.shape
    for s in range(d // LANES):
        slab_ref[s] = x[:, s * LANES:(s + 1) * LANES]
    for s in range(d // LANES):
        for b in range(batch):
            dst_ref[b, :, s * LANES:(s + 1) * LANES] = slab_ref[s, pl.ds(b, steps, stride=batch), :]

def _prev_step(cur, prev_rolled, sub):
    cur_rolled = pltpu.roll(cur, SUBLANES // 2, axis=0)
    return jnp.where(sub < SUBLANES // 2, prev_rolled, cur_rolled), cur_rolled


def _odd_kernel(h_ref, gm_ref, win_ref, cw_ref, cb_ref, wgate_ref, bgate_ref, lam_ref,
                sa_ref, bw_ref, cwt_ref, sd_ref, wglu_ref, bglu_ref, wo_ref,
                o_ref,
                halo_ref, rg_a_ref, rg_b_ref, rg_x_ref, rg_carry_ref,
                uh_ref, cx_ref, s5_carry_ref, slab_ref):
    first = pl.program_id(0) == 0

    @pl.when(first)
    def _():
        halo_ref[...] = jnp.zeros_like(halo_ref)
        rg_carry_ref[...] = jnp.zeros_like(rg_carry_ref)
        s5_carry_ref[...] = jnp.zeros_like(s5_carry_ref)
        uh_ref[...] = jnp.zeros_like(uh_ref)

    tm = h_ref.shape[0] * h_ref.shape[1]
    groups = tm // SUBLANES
    h = _interleave(h_ref, slab_ref)
    xn = _rms(h, gm_ref[...]).astype(MXU_DTYPE)
    y = _dot(xn, win_ref[...])
    x_gate, x_rg, u = (y[:, i * RG_WIDTH:(i + 1) * RG_WIDTH] for i in range(3))

    step = SUBLANES // 2

    halo_rows = halo_ref.shape[0]
    ext = jnp.concatenate([halo_ref[...], x_rg], axis=0)
    halo_ref[...] = x_rg[tm - halo_rows:, :]
    cw = cw_ref[...]
    xc = cb_ref[...] + x_rg * cw[RG_CONV - 1:RG_CONV]
    for j in range(RG_CONV - 1):
        back = (RG_CONV - 1 - j) * step
        xc = xc + ext[halo_rows - back:halo_rows - back + tm, :] * cw[j:j + 1]

    xcb = xc.astype(MXU_DTYPE)
    half_rg = RG_WIDTH // 2
    g_lo = _dot(xcb[:, :half_rg], wgate_ref[0])
    g_hi = _dot(xcb[:, half_rg:], wgate_ref[1])
    gate_pre = jnp.concatenate([g_lo[:, :half_rg], g_hi[:, :half_rg],
                                g_lo[:, half_rg:], g_hi[:, half_rg:]], axis=1) + bgate_ref[...]

    ub = u.astype(MXU_DTYPE)
    half_u = S5_WIDTH // 2
    half_x = S5_LANES // 2
    u_halo = uh_ref.shape[0]
    u_ext = jnp.concatenate([uh_ref[...], u], axis=0)
    uh_ref[...] = u[tm - u_halo:, :]
    ub_prev = u_ext[u_halo - step:u_halo - step + tm, :].astype(MXU_DTYPE)
    for hf in range(2):
        cols = slice(hf * half_u, (hf + 1) * half_u)
        cx_ref[:, 2 * hf * half_x:2 * (hf + 1) * half_x] = _dot(
            jnp.concatenate([ub[:, cols], ub_prev[:, cols]], axis=1), bw_ref[hf])

    gates = _sigmoid(gate_pre)
    r_gate, i_gate = gates[:, :RG_WIDTH], gates[:, RG_WIDTH:]
    lam = lam_ref[...]
    softplus_neg = jnp.maximum(-lam, 0.0) + jnp.log1p(jnp.exp(-jnp.abs(lam)))
    log_a = (-RG_C * softplus_neg) * r_gate
    a = jnp.exp(log_a)
    rg_a_ref[...] = a
    var = -jnp.tanh(log_a) * (a * a + 1.0)
    rg_b_ref[...] = jnp.where(var > 0.0, var * lax.rsqrt(var), 0.0) * (i_gate * xc)

    sub = lax.broadcasted_iota(jnp.int32, (SUBLANES, RG_WIDTH), 0)

    x, ra_prev, rb_prev = rg_carry_ref[0], rg_carry_ref[1], rg_carry_ref[2]
    for g in range(groups):
        rows = slice(g * SUBLANES, (g + 1) * SUBLANES)
        a = rg_a_ref[rows, :]
        bx = rg_b_ref[rows, :]
        a_prev, ra_prev = _prev_step(a, ra_prev, sub)
        b_prev, rb_prev = _prev_step(bx, rb_prev, sub)
        x = (a * a_prev) * x + (a * b_prev + bx)
        rg_x_ref[rows, :] = x
    for i, v in enumerate((x, ra_prev, rb_prev)):
        rg_carry_ref[i] = v
    y_c = rg_x_ref[...] * _gelu_tanh(x_gate)

    for lc in range(S5_LANES // SCAN_LANES):
        hf, within = divmod(lc * SCAN_LANES, half_x)
        coef = slice(lc * SCAN_LANES, (lc + 1) * SCAN_LANES)
        re = slice(2 * hf * half_x + within, 2 * hf * half_x + within + SCAN_LANES)
        im = slice(re.start + half_x, re.stop + half_x)
        a2_re, a2_im = sa_ref[0, :, coef], sa_ref[1, :, coef]

        x_re, x_im = s5_carry_ref[0, :, coef], s5_carry_ref[1, :, coef]
        for g in range(groups):
            rows = slice(g * SUBLANES, (g + 1) * SUBLANES)
            x_re, x_im = (a2_re * x_re - a2_im * x_im + cx_ref[rows, re],
                          a2_re * x_im + a2_im * x_re + cx_ref[rows, im])
            cx_ref[rows, re] = x_re
            cx_ref[rows, im] = x_im
        s5_carry_ref[0, :, coef] = x_re
        s5_carry_ref[1, :, coef] = x_im

    ys = jnp.concatenate(
        [_dot(cx_ref[:, 2 * hf * half_x:2 * (hf + 1) * half_x].astype(MXU_DTYPE), cwt_ref[hf])
         for hf in range(2)], axis=1) + sd_ref[...] * u
    ys = _gelu_tanh(ys)
    y_d = ys * _sigmoid(_dot(ys.astype(MXU_DTYPE), wglu_ref[...]) + bglu_ref[...])
    out = (h + _dot(y_c.astype(MXU_DTYPE), wo_ref[:RG_WIDTH, :])
           + _dot(y_d.astype(MXU_DTYPE), wo_ref[RG_WIDTH:, :]))
    _deinterleave(out, slab_ref, o_ref)


def _block_diag(blocks):
    n, r, c = blocks.shape
    eye = jnp.eye(n, dtype=blocks.dtype)
    return (eye[:, None, :, None] * blocks[:, :, None, :]).reshape(n * r, n * c)


def _odd_mixer(h, mix_norm, w_in, conv_w, conv_b, w_a, b_a, w_i, b_i, lam,
               a_re, a_im, log_dt, b_re, b_im, c_re, c_im, d_skip, w_glu, b_glu, w_out, *,
               batch):
    assert batch == SUBLANES // 2
    tokens, d = h.shape
    seq = tokens // batch
    tm = ODD_ROWS
    steps = tm // batch

    dt = jnp.exp(log_dt.astype(F32))[:, None]
    lr, li = a_re.astype(F32), a_im.astype(F32)
    mag = jnp.exp(lr * dt)
    ab_re = mag * jnp.cos(li * dt)
    ab_im = mag * jnp.sin(li * dt)
    den = lr * lr + li * li
    nr, ni = ab_re - 1.0, ab_im
    coef_re = (nr * lr + ni * li) / den
    coef_im = (ni * lr - nr * li) / den
    bb_re = coef_re[..., None] * b_re - coef_im[..., None] * b_im
    bb_im = coef_re[..., None] * b_im + coef_im[..., None] * b_re
    abb_re = ab_re[..., None] * bb_re - ab_im[..., None] * bb_im
    abb_im = ab_re[..., None] * bb_im + ab_im[..., None] * bb_re
    halves = [slice(0, S5_GROUPS // 2), slice(S5_GROUPS // 2, S5_GROUPS)]

    def in_map(m_re, m_im, g):
        return jnp.concatenate([_block_diag(m_re[g].transpose(0, 2, 1)),
                                _block_diag(m_im[g].transpose(0, 2, 1))], axis=1)

    bw = jnp.stack([jnp.concatenate([in_map(bb_re, bb_im, g), in_map(abb_re, abb_im, g)],
                                    axis=0) for g in halves])
    cwt = jnp.stack([jnp.concatenate([_block_diag(c_re[g].transpose(0, 2, 1)),
                                      -_block_diag(c_im[g].transpose(0, 2, 1))], axis=0)
                     for g in halves])
    a2_re = ab_re * ab_re - ab_im * ab_im
    a2_im = 2.0 * ab_re * ab_im
    sa = jnp.stack([jnp.broadcast_to(v.reshape(1, -1), (SUBLANES, S5_LANES))
                    for v in (a2_re, a2_im)])
    hb = RG_BLOCKS // 2
    wgate = jnp.stack([jnp.concatenate([_block_diag(w_a[g]), _block_diag(w_i[g])], axis=1)
                       for g in (slice(0, hb), slice(hb, RG_BLOCKS))])
    bgate = jnp.concatenate([b_a, b_i]).reshape(1, -1)

    cast = lambda w: w.astype(MXU_DTYPE)
    weights = [mix_norm.reshape(1, d), cast(w_in), conv_w, conv_b.reshape(1, -1), cast(wgate),
               bgate, lam.reshape(1, -1), sa, cast(bw), cast(cwt), d_skip.reshape(1, -1),
               cast(w_glu), b_glu.reshape(1, -1), cast(w_out)]
    row = pl.BlockSpec((batch, steps, d), lambda r: (0, r, 0))
    out = pl.pallas_call(
        _odd_kernel,
        grid=(seq // steps,),
        in_specs=[row] + [_resident(w.shape) for w in weights],
        out_specs=row,
        out_shape=jax.ShapeDtypeStruct((batch, seq, d), F32),
        scratch_shapes=[
            pltpu.VMEM((2 * SUBLANES, RG_WIDTH), F32),
            pltpu.VMEM((tm, RG_WIDTH), F32),
            pltpu.VMEM((tm, RG_WIDTH), F32),
            pltpu.VMEM((tm, RG_WIDTH), F32),
            pltpu.VMEM((3, SUBLANES, RG_WIDTH), F32),
            pltpu.VMEM((SUBLANES, S5_WIDTH), F32),
            pltpu.VMEM((tm, 2 * S5_LANES), F32),
            pltpu.VMEM((2, SUBLANES, S5_LANES), F32),
            pltpu.VMEM((d // LANES, tm, LANES), F32),
        ],
        compiler_params=_params("arbitrary"),
        name="odd_mixer",
    )(h.reshape(batch, seq, d), *weights)
    return out.reshape(tokens, d)


def kernel(x, p, ffn_a_norm, ffn_a_w1, ffn_a_w3, ffn_a_w2, mix_norm, ffn_b_norm, ffn_b_w1, ffn_b_w3, ffn_b_w2, ple_norm, ple_w_gate, ple_w_up, ev_w_in, mla_q_norm, mla_w_q_up, mla_kv_norm, mla_w_kv_up, gla_w_gate_up, gla_b_gate, gla_out_norm, ev_w_out, od_w_in, rg_conv_w, rg_conv_b, rg_w_a, rg_b_a, rg_w_i, rg_b_i, rg_lambda, s5_a_re, s5_a_im, s5_log_dt, s5_b_re, s5_b_im, s5_c_re, s5_c_im, s5_d, s5_w_glu, s5_b_glu, od_w_out, final_norm):
    batch, seq, d = x.shape
    depth = p.shape[0]
    h = x.reshape(batch * seq, d)
    cast = lambda w: w.astype(MXU_DTYPE)
    ffn_a = (ffn_a_norm, cast(ffn_a_w1), cast(ffn_a_w3), cast(ffn_a_w2))
    ffn_b = (ffn_b_norm, cast(ffn_b_w1), cast(ffn_b_w3), cast(ffn_b_w2))
    ple = (p, ple_norm, cast(ple_w_gate), cast(ple_w_up))
    for i in range(depth):
        j = i // 2
        h = _ffn(h, i, *ffn_a)
        if i % 2 == 0:
            q, k, v, lq, lk, lv, lg, lr = _even_proj(
                h, mix_norm[i], ev_w_in[j], mla_q_norm[j], mla_w_q_up[j], mla_kv_norm[j],
                mla_w_kv_up[j], gla_w_gate_up[j], gla_b_gate[j], batch=batch)
            o_attn = _attention(q, k, v, batch=batch)
            h = _gla_out(lq, lk, lv, lg, lr, o_attn, h, gla_out_norm[j], ev_w_out[j],
                         batch=batch)
        else:
            h = _odd_mixer(h, mix_norm[i], od_w_in[j], rg_conv_w[j], rg_conv_b[j], rg_w_a[j],
                           rg_b_a[j], rg_w_i[j], rg_b_i[j], rg_lambda[j], s5_a_re[j],
                           s5_a_im[j], s5_log_dt[j], s5_b_re[j], s5_b_im[j], s5_c_re[j],
                           s5_c_im[j], s5_d[j], s5_w_glu[j], s5_b_glu[j], od_w_out[j],
                           batch=batch)
        last = i == depth - 1
        h = _ffn(h, i, *ffn_b, ple=ple, final_norm=final_norm if last else None)
    return h.reshape(batch, seq, d)
```

```python
import functools
import math

import jax
import jax.numpy as jnp
from jax import lax
from jax.experimental import pallas as pl
from jax.experimental.pallas import tpu as pltpu

MXU_DTYPE = jnp.bfloat16
F32 = jnp.float32

EPS = 1e-6
MLA_HEADS = 8
MLA_NOPE = 64
MLA_ROPE = 32
MLA_V = 64
MLA_Q_RANK = 384
MLA_KV_RANK = 256
ROPE_BASE = 10000.0
GLA_HEADS = 4
GLA_DK = 64
GLA_DV = 128
GLA_GATE_RANK = 16
GLA_TAU = 16.0
GLA_CHUNK = 64
RG_WIDTH = 512
RG_BLOCKS = 8
RG_CONV = 4
RG_C = 8.0
S5_GROUP = 16
S5_GROUPS = 32
S5_STATE = 64
S5_WIDTH = S5_GROUP * S5_GROUPS
S5_LANES = S5_GROUPS * S5_STATE

LANES = 128
SUBLANES = 8
MXU_TILE = 256
HEAD_PAD = LANES
VMEM_LIMIT = 56 * 1024 * 1024

FFN_ROWS = 512
FFN_CHUNKS = 2
PROJ_ROWS = 1024
ATTN_Q = 512
ATTN_HEADS = 4
GLA_ROWS = 1024
ODD_ROWS = 512
SCAN_LANES = 512


def _rms(x, g):
    return x * lax.rsqrt(jnp.mean(x * x, axis=-1, keepdims=True) + EPS) * g


def _dot(a, b):
    return jnp.dot(a, b, preferred_element_type=F32)


def _dot_nt(a, b):
    return lax.dot_general(a, b, (((1,), (1,)), ((), ())), preferred_element_type=F32)


def _dot_tn(a, b):
    return lax.dot_general(a, b, (((0,), (0,)), ((), ())), preferred_element_type=F32)


def _sigmoid(x):
    return 1.0 / (1.0 + jnp.exp(-x))


def _gelu_tanh(x):
    return 0.5 * x * (1.0 + jnp.tanh(math.sqrt(2.0 / math.pi) * (x + 0.044715 * (x * x * x))))


def _log_sigmoid(x):
    return jnp.minimum(x, 0.0) - jnp.log1p(jnp.exp(-jnp.abs(x)))


def _resident(shape):
    return pl.BlockSpec(shape, lambda *_: (0,) * len(shape), pipeline_mode=pl.Buffered(1))


def _params(*semantics):
    return pltpu.CompilerParams(dimension_semantics=semantics, vmem_limit_bytes=VMEM_LIMIT)


def _ffn_kernel(*refs, ple, final):
    h_ref, g_ref, w1_ref, w3_ref, w2_ref = refs[:5]
    rest = list(refs[5:])
    o_ref = rest.pop()
    h = h_ref[...]
    xn = _rms(h, g_ref[...]).astype(MXU_DTYPE)
    tiles = w1_ref.shape[1] // MXU_TILE
    bounds = [MXU_TILE * ((tiles * c + FFN_CHUNKS - 1) // FFN_CHUNKS) for c in range(FFN_CHUNKS + 1)]
    y = None
    for lo, hi in zip(bounds[:-1], bounds[1:]):
        a = _dot(xn, w1_ref[:, lo:hi])
        b = _dot(xn, w3_ref[:, lo:hi])
        t = (a * _sigmoid(a) * b).astype(MXU_DTYPE)
        yc = _dot(t, w2_ref[lo:hi, :])
        y = yc if y is None else y + yc
    h = h + 0.5 * y
    if ple:
        p_ref, gp_ref, wg_ref, wu_ref = rest[:4]
        rest = rest[4:]
        gate = _sigmoid(_dot(_rms(h, gp_ref[...]).astype(MXU_DTYPE), wg_ref[...]))
        up = _dot(p_ref[...].astype(MXU_DTYPE), wu_ref[...])
        h = h + up * gate
    if final:
        h = _rms(h, rest[0][...])
    o_ref[...] = h


def _ffn(h, layer, norm, w1, w3, w2, *, ple=None, final_norm=None):
    tokens, d = h.shape
    tm = FFN_ROWS
    hspec = pl.BlockSpec((tm, d), lambda r: (r, 0))

    def of_layer(a):
        rest = a.shape[1:]
        return pl.BlockSpec((None,) + rest, lambda r: (layer,) + (0,) * len(rest),
                            pipeline_mode=pl.Buffered(1))

    gains = lambda g: g.reshape(g.shape[0], 1, d)
    args = [h, gains(norm), w1, w3, w2]
    specs = [hspec] + [of_layer(a) for a in args[1:]]
    if ple is not None:
        p, gp, wg, wu = ple
        p = p.reshape(p.shape[0], tokens, p.shape[-1])
        args += [p, gains(gp), wg, wu]
        specs += [pl.BlockSpec((None, tm, p.shape[-1]), lambda r: (layer, r, 0)),
                  of_layer(args[-3]), of_layer(wg), of_layer(wu)]
    if final_norm is not None:
        args.append(final_norm.reshape(1, d))
        specs.append(_resident((1, d)))
    return pl.pallas_call(
        functools.partial(_ffn_kernel, ple=ple is not None, final=final_norm is not None),
        grid=(tokens // tm,),
        in_specs=specs,
        out_specs=hspec,
        out_shape=jax.ShapeDtypeStruct((tokens, d), F32),
        compiler_params=_params("parallel"),
        name="ffn",
    )(*args)


_EV_COLS = (MLA_Q_RANK, MLA_KV_RANK, HEAD_PAD, GLA_HEADS * GLA_DK, GLA_HEADS * GLA_DK,
            GLA_HEADS * GLA_DV, GLA_HEADS * GLA_DV)
ROT_SHIFT = HEAD_PAD - MLA_ROPE


def _even_proj_kernel(h_ref, gm_ref, win_ref, gq_ref, wq_ref, gkv_ref, wk_ref, wv_ref,
                      wg_ref, bg_ref, cq_ref, sq_ref, ck_ref, sk_ref,
                      q_ref, k_ref, v_ref, lq_ref, lk_ref, lv_ref, lg_ref, lr_ref):
    xn = _rms(h_ref[...], gm_ref[...]).astype(MXU_DTYPE)
    y = _dot(xn, win_ref[...])
    offs = [0]
    for w in _EV_COLS:
        offs.append(offs[-1] + w)
    c_q, c_kv, kr, lq, lk, lv, lr = (
        y[:, offs[i]:offs[i + 1]] for i in range(len(_EV_COLS)))
    g_low = kr

    cqn = _rms(c_q, gq_ref[...]).astype(MXU_DTYPE)
    qf = _dot(cqn, wq_ref[...])
    ckn = _rms(c_kv, gkv_ref[...]).astype(MXU_DTYPE)
    kn = _dot(ckn, wk_ref[...])
    v_lane = lax.broadcasted_iota(jnp.int32, (1, MLA_HEADS * HEAD_PAD), 1) % HEAD_PAD
    v_ref[...] = (_dot(ckn, wv_ref[...])
                  + jnp.where(v_lane == MLA_V, 1.0, 0.0)).astype(v_ref.dtype)
    k_rope = kr * ck_ref[...] + pltpu.roll(kr, ROT_SHIFT, axis=1) * sk_ref[...]
    cq, sq = cq_ref[...], sq_ref[...]
    for hd in range(MLA_HEADS):
        sl = slice(hd * HEAD_PAD, (hd + 1) * HEAD_PAD)
        q_h = qf[:, sl]
        q_ref[:, sl] = (q_h * cq + pltpu.roll(q_h, ROT_SHIFT, axis=1) * sq).astype(q_ref.dtype)
        k_ref[:, sl] = (kn[:, sl] + k_rope).astype(k_ref.dtype)

    gate = _dot(g_low.astype(MXU_DTYPE), wg_ref[...]) + bg_ref[...]
    lg_ref[...] = _log_sigmoid(gate) * (1.0 / GLA_TAU)
    lq_ref[...] = lq
    lk_ref[...] = lk
    lv_ref[...] = lv.astype(lv_ref.dtype)
    lr_ref[...] = lr


def _pad_heads(w, heads, width):
    lead = w.shape[:-1]
    w = w.reshape(lead + (heads, width))
    w = jnp.pad(w, [(0, 0)] * len(lead) + [(0, 0), (0, HEAD_PAD - width)])
    return w.reshape(lead + (heads * HEAD_PAD,))


def _rot_half(w):
    half = MLA_ROPE // 2
    return jnp.concatenate([-w[..., half:], w[..., :half]], axis=-1)


def _head_slot(w_nope, w_rope):
    return jnp.concatenate([w_nope, w_rope, _rot_half(w_rope)], axis=-1)


def _even_proj(h, mix_norm, w_in, q_norm, w_q_up, kv_norm, w_kv_up, w_gate_up, b_gate, *,
               batch):
    tokens, d = h.shape
    seq = tokens // batch
    tm = PROJ_ROWS
    n_s = seq // tm
    splits = [MLA_Q_RANK, MLA_KV_RANK, MLA_ROPE, GLA_HEADS * GLA_DK, GLA_HEADS * GLA_DK,
              GLA_HEADS * GLA_DV, GLA_GATE_RANK, GLA_HEADS * GLA_DV]
    offs = [0]
    for s in splits:
        offs.append(offs[-1] + s)
    wc_q, wc_kv, w_kr, w_lq, w_lk, w_lv, w_gl, w_lr = (
        w_in[:, offs[i]:offs[i + 1]] for i in range(len(splits)))
    gl_lanes = jnp.pad(w_gl, [(0, 0), (0, MLA_NOPE - GLA_GATE_RANK)])
    win = jnp.concatenate([
        wc_q, wc_kv, _head_slot(gl_lanes, w_kr), w_lq, w_lk, w_lv, w_lr], axis=1)

    wq = w_q_up.reshape(MLA_Q_RANK, MLA_HEADS, MLA_NOPE + MLA_ROPE)
    wq_full = _head_slot(wq[..., :MLA_NOPE], wq[..., MLA_NOPE:]).reshape(MLA_Q_RANK, -1)
    wkv = w_kv_up.reshape(MLA_KV_RANK, MLA_HEADS, MLA_NOPE + MLA_V)
    wk = _pad_heads(wkv[..., :MLA_NOPE].reshape(MLA_KV_RANK, -1), MLA_HEADS, MLA_NOPE)
    wv = _pad_heads(wkv[..., MLA_NOPE:].reshape(MLA_KV_RANK, -1), MLA_HEADS, MLA_V)
    wg = jnp.pad(w_gate_up, [(0, HEAD_PAD - GLA_GATE_RANK), (0, 0)])
    bg = b_gate.reshape(1, -1)

    half = MLA_ROPE // 2
    inv = ROPE_BASE ** (-jnp.arange(half, dtype=F32) / half)
    ang = jnp.arange(seq).astype(F32)[:, None] * inv[None, :]
    cos2 = jnp.tile(jnp.cos(ang), (1, 2))
    sin2 = jnp.tile(jnp.sin(ang), (1, 2))
    scale = (MLA_NOPE + MLA_ROPE) ** -0.5 * math.log2(math.e)
    ones = jnp.ones((seq, MLA_NOPE), F32)
    tail = jnp.zeros((seq, HEAD_PAD - MLA_NOPE - MLA_ROPE), F32)
    cq = jnp.concatenate([ones, cos2, tail], axis=1) * scale
    sq = jnp.concatenate([0 * ones, sin2, tail], axis=1) * scale
    ck = jnp.concatenate([0 * ones, cos2, tail], axis=1)
    sk = jnp.concatenate([0 * ones, sin2, tail], axis=1)

    cast = lambda w: w.astype(MXU_DTYPE)
    row = lambda w: pl.BlockSpec((tm, w), lambda r: (r, 0))
    tab = pl.BlockSpec((tm, HEAD_PAD), lambda r: (r % n_s, 0))
    gl_w = GLA_HEADS * GLA_DK
    gv_w = GLA_HEADS * GLA_DV
    qk_w = MLA_HEADS * HEAD_PAD
    outs = [(qk_w, MXU_DTYPE), (qk_w, MXU_DTYPE), (qk_w, MXU_DTYPE),
            (gl_w, F32), (gl_w, F32), (gv_w, MXU_DTYPE), (gl_w, F32), (gv_w, F32)]
    weights = [mix_norm.reshape(1, d), cast(win), q_norm.reshape(1, -1), cast(wq_full),
               kv_norm.reshape(1, -1), cast(wk), cast(wv), cast(wg), bg]
    return pl.pallas_call(
        _even_proj_kernel,
        grid=(tokens // tm,),
        in_specs=[row(d)] + [_resident(w.shape) for w in weights] + [tab] * 4,
        out_specs=[row(w) for w, _ in outs],
        out_shape=[jax.ShapeDtypeStruct((tokens, w), dt) for w, dt in outs],
        compiler_params=_params("parallel"),
        name="even_proj",
    )(h, *weights, cq, sq, ck, sk)


def _attn_kernel(q_ref, k_ref, v_ref, o_ref, s_ref, m_ref, acc_ref, *, tq):
    n_q = q_ref.shape[0] // tq
    heads = m_ref.shape[0]
    row = lax.broadcasted_iota(jnp.int32, (tq, tq), 0)
    col = lax.broadcasted_iota(jnp.int32, (tq, tq), 1)
    causal = col <= row
    sls = [slice(hd * HEAD_PAD, (hd + 1) * HEAD_PAD) for hd in range(heads)]
    lane = lax.broadcasted_iota(jnp.int32, (tq, HEAD_PAD), 1)

    def tile(i):
        return pl.ds(pl.multiple_of(i * tq, tq), tq)

    def scores(hd, qi, j):
        return _dot_nt(q_ref[tile(qi), sls[hd]], k_ref[tile(j), sls[hd]])

    def softmax_pv(hd, s, j, masked):
        if masked:
            s = jnp.where(causal, s, -jnp.inf)
        s = s.astype(MXU_DTYPE)
        half = tq // 2
        for rows in (slice(0, half), slice(half, tq)):
            s_r = s[rows]
            m = m_ref[hd, rows]
            m_new = jnp.maximum(m.astype(MXU_DTYPE), jnp.max(s_r, axis=1, keepdims=True))
            alpha = jnp.exp2(m - m_new.astype(F32))
            p = jnp.exp2(s_r - jnp.concatenate([m_new] * (tq // LANES), axis=1))
            m_ref[hd, rows] = m_new.astype(F32)
            acc_ref[hd, rows] = alpha * acc_ref[hd, rows] + _dot(p, v_ref[tile(j), sls[hd]])

    def step(qi, j, masked, next_qi, next_j):
        s_cur = s_ref[...]
        for hd in range(heads):
            s_next = scores(hd + 1, qi, j) if hd + 1 < heads else scores(0, next_qi, next_j)
            softmax_pv(hd, s_cur, j, masked)
            s_cur = s_next
        s_ref[...] = s_cur

    s_ref[...] = scores(0, 0, 0)

    def q_tile(qi, _):
        m_ref[...] = jnp.full(m_ref.shape, -jnp.inf, F32)
        acc_ref[...] = jnp.zeros(acc_ref.shape, F32)

        def body(j, _):
            step(qi, j, False, qi, j + 1)
            return 0

        lax.fori_loop(0, qi, body, 0)
        step(qi, qi, True, jnp.minimum(qi + 1, n_q - 1), 0)
        for pair in range(heads // 2):
            o_a, o_b = (acc_ref[hd] / acc_ref[hd][:, MLA_V:MLA_V + 1]
                        for hd in (2 * pair, 2 * pair + 1))
            o_ref[tile(qi), pair * LANES:(pair + 1) * LANES] = jnp.where(
                lane < MLA_V, o_a, pltpu.roll(o_b, MLA_V, axis=1)).astype(o_ref.dtype)
        return 0

    lax.fori_loop(0, n_q, q_tile, 0)


def _attention(q, k, v, *, batch):
    tokens = q.shape[0]
    seq = tokens // batch
    tq = ATTN_Q
    q3 = q.reshape(batch, seq, -1)
    k3 = k.reshape(batch, seq, -1)
    v3 = v.reshape(batch, seq, -1)
    nh = ATTN_HEADS
    group = lambda w, **kw: pl.BlockSpec((None, seq, nh * w), lambda b, g: (b, 0, g), **kw)
    once = dict(pipeline_mode=pl.Buffered(1))
    out = pl.pallas_call(
        functools.partial(_attn_kernel, tq=tq),
        grid=(batch, MLA_HEADS // nh),
        in_specs=[group(HEAD_PAD, **once), group(HEAD_PAD), group(HEAD_PAD)],
        out_specs=group(MLA_V, **once),
        out_shape=jax.ShapeDtypeStruct((batch, seq, MLA_HEADS * MLA_V), MXU_DTYPE),
        scratch_shapes=[pltpu.VMEM((tq, tq), F32), pltpu.VMEM((nh, tq, LANES), F32),
                        pltpu.VMEM((nh, tq, HEAD_PAD), F32)],
        compiler_params=_params("parallel", "parallel"),
        name="attention",
    )(q3, k3, v3)
    return out.reshape(tokens, -1)


def _gla_out_kernel(lq_ref, lk_ref, lv_ref, lg_ref, lr_ref, oa_ref, h_ref, gn_ref, wo_ref,
                    o_ref, state_ref, y_ref):
    @pl.when(pl.program_id(1) == 0)
    def _():
        state_ref[...] = jnp.zeros_like(state_ref)

    tt, width = lg_ref.shape
    chunk = GLA_CHUNK
    pos = lax.broadcasted_iota(jnp.int32, (tt, width), 0) & (chunk - 1)
    b = lg_ref[...]
    d = 1
    while d < chunk:
        b = b + jnp.where(pos >= d, pltpu.roll(b, d, axis=0), 0.0)
        d *= 2
    lk = lk_ref[...]
    parity = lambda rows: (lax.broadcasted_iota(jnp.int32, (rows, width), 1) // GLA_DK) % 2
    q_e = lq_ref[...] * (GLA_DK ** -0.5) * jnp.exp(b)
    q_par = [jnp.where(parity(tt) == par, q_e, 0.0).astype(MXU_DTYPE) for par in range(2)]
    k_e = (lk * jnp.exp(-b)).astype(MXU_DTYPE)
    tri = (lax.broadcasted_iota(jnp.int32, (chunk, chunk), 1)
           <= lax.broadcasted_iota(jnp.int32, (chunk, chunk), 0))
    gn = gn_ref[...]
    for c in range(tt // chunk):
        rows = slice(c * chunk, (c + 1) * chunk)
        b_c = b[rows]
        b_last = b_c[chunk - 1:chunk]
        k_end = lk[rows] * jnp.exp(b_last - b_c)
        k_end_par = [jnp.where(parity(chunk) == par, k_end, 0.0).astype(MXU_DTYPE)
                     for par in range(2)]
        decay = jnp.exp(b_last)
        sls = [slice((hd // 2) * LANES, (hd // 2 + 1) * LANES) for hd in range(GLA_HEADS)]
        vss = [slice(hd * GLA_DV, (hd + 1) * GLA_DV) for hd in range(GLA_HEADS)]
        v_cs = [lv_ref[rows, vs] for vs in vss]
        q_hs = [q_par[hd % 2][rows, sls[hd]] for hd in range(GLA_HEADS)]
        atts = [_dot_nt(q_hs[hd], k_e[rows, sls[hd]]) for hd in range(GLA_HEADS)]
        sts = [state_ref[hd] for hd in range(GLA_HEADS)]
        inter = [_dot_nt(q_hs[hd], sts[hd].astype(MXU_DTYPE)) for hd in range(GLA_HEADS)]
        kvs = [_dot_tn(v_cs[hd], k_end_par[hd % 2][:, sls[hd]]) for hd in range(GLA_HEADS)]
        for hd in range(GLA_HEADS):
            state_ref[hd] = sts[hd] * decay[:, sls[hd]] + kvs[hd]
        for hd in range(GLA_HEADS):
            att = jnp.where(tri, atts[hd], 0.0)
            o = _rms(_dot(att.astype(MXU_DTYPE), v_cs[hd]) + inter[hd], gn)
            r = lr_ref[rows, vss[hd]]
            y_ref[rows, vss[hd]] = (o * (r * _sigmoid(r))).astype(y_ref.dtype)
    n_a = oa_ref.shape[1]
    o_ref[...] = (h_ref[...] + _dot(oa_ref[...], wo_ref[:n_a, :])
                  + _dot(y_ref[...], wo_ref[n_a:, :]))


def _gla_out(lq, lk, lv, lg, lr, o_attn, h, out_norm, w_out, *, batch):
    tokens, d = h.shape
    seq = tokens // batch
    tt = GLA_ROWS
    n_s = seq // tt
    row = lambda w: pl.BlockSpec((tt, w), lambda b, i: (b * n_s + i, 0))
    ins = [lq, lk, lv, lg, lr, o_attn, h]
    return pl.pallas_call(
        _gla_out_kernel,
        grid=(batch, n_s),
        in_specs=[row(a.shape[1]) for a in ins] + [_resident((1, GLA_DV)),
                                                   _resident(w_out.shape)],
        out_specs=row(d),
        out_shape=jax.ShapeDtypeStruct((tokens, d), F32),
        scratch_shapes=[pltpu.VMEM((GLA_HEADS, GLA_DV, HEAD_PAD), F32),
                        pltpu.VMEM((tt, GLA_HEADS * GLA_DV), MXU_DTYPE)],
        compiler_params=_params("parallel", "arbitrary"),
        name="gla_out",
    )(*ins, out_norm.reshape(1, -1), w_out.astype(MXU_DTYPE))


def _interleave(src_ref, slab_ref):
    batch, steps, d = src_ref.shape
    for s in range(d // LANES):
        for b in range(batch):
            slab_ref[s, pl.ds(b, steps, stride=batch), :] = src_ref[b, :, s * LANES:(s + 1) * LANES]
    return jnp.concatenate([slab_ref[s] for s in range(d // LANES)], axis=1)


def _deinterleave(x, slab_ref, dst_ref):
    batch, steps, d = dst_ref.shape
    for s in range(d // LANES):
        slab_ref[s] = x[:, s * LANES:(s + 1) * LANES]
    for s in range(d // LANES):
        for b in range(batch):
            dst_ref[b, :, s * LANES:(s + 1) * LANES] = slab_ref[s, pl.ds(b, steps, stride=batch), :]

def _prev_step(cur, prev_rolled, sub):
    cur_rolled = pltpu.roll(cur, SUBLANES // 2, axis=0)
    return jnp.where(sub < SUBLANES // 2, prev_rolled, cur_rolled), cur_rolled


def _odd_kernel(h_ref, gm_ref, win_ref, cw_ref, cb_ref, wgate_ref, bgate_ref, lam_ref,
                sa_ref, bw_ref, cwt_ref, sd_ref, wglu_ref, bglu_ref, wo_ref,
                o_ref,
                halo_ref, rg_a_ref, rg_b_ref, rg_x_ref, rg_carry_ref,
                uh_ref, cx_ref, s5_carry_ref, slab_ref):
    first = pl.program_id(0) == 0

    @pl.when(first)
    def _():
        halo_ref[...] = jnp.zeros_like(halo_ref)
        rg_carry_ref[...] = jnp.zeros_like(rg_carry_ref)
        s5_carry_ref[...] = jnp.zeros_like(s5_carry_ref)
        uh_ref[...] = jnp.zeros_like(uh_ref)

    tm = h_ref.shape[0] * h_ref.shape[1]
    groups = tm // SUBLANES
    h = _interleave(h_ref, slab_ref)
    xn = _rms(h, gm_ref[...]).astype(MXU_DTYPE)
    y = _dot(xn, win_ref[...])
    x_gate, x_rg, u = (y[:, i * RG_WIDTH:(i + 1) * RG_WIDTH] for i in range(3))

    step = SUBLANES // 2

    halo_rows = halo_ref.shape[0]
    ext = jnp.concatenate([halo_ref[...], x_rg], axis=0)
    halo_ref[...] = x_rg[tm - halo_rows:, :]
    cw = cw_ref[...]
    xc = cb_ref[...] + x_rg * cw[RG_CONV - 1:RG_CONV]
    for j in range(RG_CONV - 1):
        back = (RG_CONV - 1 - j) * step
        xc = xc + ext[halo_rows - back:halo_rows - back + tm, :] * cw[j:j + 1]

    xcb = xc.astype(MXU_DTYPE)
    half_rg = RG_WIDTH // 2
    g_lo = _dot(xcb[:, :half_rg], wgate_ref[0])
    g_hi = _dot(xcb[:, half_rg:], wgate_ref[1])
    gate_pre = jnp.concatenate([g_lo[:, :half_rg], g_hi[:, :half_rg],
                                g_lo[:, half_rg:], g_hi[:, half_rg:]], axis=1) + bgate_ref[...]

    ub = u.astype(MXU_DTYPE)
    half_u = S5_WIDTH // 2
    half_x = S5_LANES // 2
    u_halo = uh_ref.shape[0]
    u_ext = jnp.concatenate([uh_ref[...], u], axis=0)
    uh_ref[...] = u[tm - u_halo:, :]
    ub_prev = u_ext[u_halo - step:u_halo - step + tm, :].astype(MXU_DTYPE)
    for hf in range(2):
        cols = slice(hf * half_u, (hf + 1) * half_u)
        cx_ref[:, 2 * hf * half_x:2 * (hf + 1) * half_x] = _dot(
            jnp.concatenate([ub[:, cols], ub_prev[:, cols]], axis=1), bw_ref[hf])

    gates = _sigmoid(gate_pre)
    r_gate, i_gate = gates[:, :RG_WIDTH], gates[:, RG_WIDTH:]
    lam = lam_ref[...]
    softplus_neg = jnp.maximum(-lam, 0.0) + jnp.log1p(jnp.exp(-jnp.abs(lam)))
    log_a = (-RG_C * softplus_neg) * r_gate
    a = jnp.exp(log_a)
    rg_a_ref[...] = a
    var = -jnp.tanh(log_a) * (a * a + 1.0)
    rg_b_ref[...] = jnp.where(var > 0.0, var * lax.rsqrt(var), 0.0) * (i_gate * xc)

    sub = lax.broadcasted_iota(jnp.int32, (SUBLANES, RG_WIDTH), 0)

    x, ra_prev, rb_prev = rg_carry_ref[0], rg_carry_ref[1], rg_carry_ref[2]
    for g in range(groups):
        rows = slice(g * SUBLANES, (g + 1) * SUBLANES)
        a = rg_a_ref[rows, :]
        bx = rg_b_ref[rows, :]
        a_prev, ra_prev = _prev_step(a, ra_prev, sub)
        b_prev, rb_prev = _prev_step(bx, rb_prev, sub)
        x = (a * a_prev) * x + (a * b_prev + bx)
        rg_x_ref[rows, :] = x
    for i, v in enumerate((x, ra_prev, rb_prev)):
        rg_carry_ref[i] = v
    y_c = rg_x_ref[...] * _gelu_tanh(x_gate)

    for lc in range(S5_LANES // SCAN_LANES):
        hf, within = divmod(lc * SCAN_LANES, half_x)
        coef = slice(lc * SCAN_LANES, (lc + 1) * SCAN_LANES)
        re = slice(2 * hf * half_x + within, 2 * hf * half_x + within + SCAN_LANES)
        im = slice(re.start + half_x, re.stop + half_x)
        a2_re, a2_im = sa_ref[0, :, coef], sa_ref[1, :, coef]

        x_re, x_im = s5_carry_ref[0, :, coef], s5_carry_ref[1, :, coef]
        for g in range(groups):
            rows = slice(g * SUBLANES, (g + 1) * SUBLANES)
            x_re, x_im = (a2_re * x_re - a2_im * x_im + cx_ref[rows, re],
                          a2_re * x_im + a2_im * x_re + cx_ref[rows, im])
            cx_ref[rows, re] = x_re
            cx_ref[rows, im] = x_im
        s5_carry_ref[0, :, coef] = x_re
        s5_carry_ref[1, :, coef] = x_im

    ys = jnp.concatenate(
        [_dot(cx_ref[:, 2 * hf * half_x:2 * (hf + 1) * half_x].astype(MXU_DTYPE), cwt_ref[hf])
         for hf in range(2)], axis=1) + sd_ref[...] * u
    ys = _gelu_tanh(ys)
    y_d = ys * _sigmoid(_dot(ys.astype(MXU_DTYPE), wglu_ref[...]) + bglu_ref[...])
    out = (h + _dot(y_c.astype(MXU_DTYPE), wo_ref[:RG_WIDTH, :])
           + _dot(y_d.astype(MXU_DTYPE), wo_ref[RG_WIDTH:, :]))
    _deinterleave(out, slab_ref, o_ref)


def _block_diag(blocks):
    n, r, c = blocks.shape
    eye = jnp.eye(n, dtype=blocks.dtype)
    return (eye[:, None, :, None] * blocks[:, :, None, :]).reshape(n * r, n * c)


def _odd_mixer(h, mix_norm, w_in, conv_w, conv_b, w_a, b_a, w_i, b_i, lam,
               a_re, a_im, log_dt, b_re, b_im, c_re, c_im, d_skip, w_glu, b_glu, w_out, *,
               batch):
    assert batch == SUBLANES // 2
    tokens, d = h.shape
    seq = tokens // batch
    tm = ODD_ROWS
    steps = tm // batch

    dt = jnp.exp(log_dt.astype(F32))[:, None]
    lr, li = a_re.astype(F32), a_im.astype(F32)
    mag = jnp.exp(lr * dt)
    ab_re = mag * jnp.cos(li * dt)
    ab_im = mag * jnp.sin(li * dt)
    den = lr * lr + li * li
    nr, ni = ab_re - 1.0, ab_im
    coef_re = (nr * lr + ni * li) / den
    coef_im = (ni * lr - nr * li) / den
    bb_re = coef_re[..., None] * b_re - coef_im[..., None] * b_im
    bb_im = coef_re[..., None] * b_im + coef_im[..., None] * b_re
    abb_re = ab_re[..., None] * bb_re - ab_im[..., None] * bb_im
    abb_im = ab_re[..., None] * bb_im + ab_im[..., None] * bb_re
    halves = [slice(0, S5_GROUPS // 2), slice(S5_GROUPS // 2, S5_GROUPS)]

    def in_map(m_re, m_im, g):
        return jnp.concatenate([_block_diag(m_re[g].transpose(0, 2, 1)),
                                _block_diag(m_im[g].transpose(0, 2, 1))], axis=1)

    bw = jnp.stack([jnp.concatenate([in_map(bb_re, bb_im, g), in_map(abb_re, abb_im, g)],
                                    axis=0) for g in halves])
    cwt = jnp.stack([jnp.concatenate([_block_diag(c_re[g].transpose(0, 2, 1)),
                                      -_block_diag(c_im[g].transpose(0, 2, 1))], axis=0)
                     for g in halves])
    a2_re = ab_re * ab_re - ab_im * ab_im
    a2_im = 2.0 * ab_re * ab_im
    sa = jnp.stack([jnp.broadcast_to(v.reshape(1, -1), (SUBLANES, S5_LANES))
                    for v in (a2_re, a2_im)])
    hb = RG_BLOCKS // 2
    wgate = jnp.stack([jnp.concatenate([_block_diag(w_a[g]), _block_diag(w_i[g])], axis=1)
                       for g in (slice(0, hb), slice(hb, RG_BLOCKS))])
    bgate = jnp.concatenate([b_a, b_i]).reshape(1, -1)

    cast = lambda w: w.astype(MXU_DTYPE)
    weights = [mix_norm.reshape(1, d), cast(w_in), conv_w, conv_b.reshape(1, -1), cast(wgate),
               bgate, lam.reshape(1, -1), sa, cast(bw), cast(cwt), d_skip.reshape(1, -1),
               cast(w_glu), b_glu.reshape(1, -1), cast(w_out)]
    row = pl.BlockSpec((batch, steps, d), lambda r: (0, r, 0))
    out = pl.pallas_call(
        _odd_kernel,
        grid=(seq // steps,),
        in_specs=[row] + [_resident(w.shape) for w in weights],
        out_specs=row,
        out_shape=jax.ShapeDtypeStruct((batch, seq, d), F32),
        scratch_shapes=[
            pltpu.VMEM((2 * SUBLANES, RG_WIDTH), F32),
            pltpu.VMEM((tm, RG_WIDTH), F32),
            pltpu.VMEM((tm, RG_WIDTH), F32),
            pltpu.VMEM((tm, RG_WIDTH), F32),
            pltpu.VMEM((3, SUBLANES, RG_WIDTH), F32),
            pltpu.VMEM((SUBLANES, S5_WIDTH), F32),
            pltpu.VMEM((tm, 2 * S5_LANES), F32),
            pltpu.VMEM((2, SUBLANES, S5_LANES), F32),
            pltpu.VMEM((d // LANES, tm, LANES), F32),
        ],
        compiler_params=_params("arbitrary"),
        name="odd_mixer",
    )(h.reshape(batch, seq, d), *weights)
    return out.reshape(tokens, d)


def kernel(x, p, ffn_a_norm, ffn_a_w1, ffn_a_w3, ffn_a_w2, mix_norm, ffn_b_norm, ffn_b_w1, ffn_b_w3, ffn_b_w2, ple_norm, ple_w_gate, ple_w_up, ev_w_in, mla_q_norm, mla_w_q_up, mla_kv_norm, mla_w_kv_up, gla_w_gate_up, gla_b_gate, gla_out_norm, ev_w_out, od_w_in, rg_conv_w, rg_conv_b, rg_w_a, rg_b_a, rg_w_i, rg_b_i, rg_lambda, s5_a_re, s5_a_im, s5_log_dt, s5_b_re, s5_b_im, s5_c_re, s5_c_im, s5_d, s5_w_glu, s5_b_glu, od_w_out, final_norm):
    batch, seq, d = x.shape
    depth = p.shape[0]
    h = x.reshape(batch * seq, d)
    cast = lambda w: w.astype(MXU_DTYPE)
    ffn_a = (ffn_a_norm, cast(ffn_a_w1), cast(ffn_a_w3), cast(ffn_a_w2))
    ffn_b = (ffn_b_norm, cast(ffn_b_w1), cast(ffn_b_w3), cast(ffn_b_w2))
    ple = (p, ple_norm, cast(ple_w_gate), cast(ple_w_up))
    for i in range(depth):
        j = i // 2
        h = _ffn(h, i, *ffn_a)
        if i % 2 == 0:
            q, k, v, lq, lk, lv, lg, lr = _even_proj(
                h, mix_norm[i], ev_w_in[j], mla_q_norm[j], mla_w_q_up[j], mla_kv_norm[j],
                mla_w_kv_up[j], gla_w_gate_up[j], gla_b_gate[j], batch=batch)
            o_attn = _attention(q, k, v, batch=batch)
            h = _gla_out(lq, lk, lv, lg, lr, o_attn, h, gla_out_norm[j], ev_w_out[j],
                         batch=batch)
        else:
            h = _odd_mixer(h, mix_norm[i], od_w_in[j], rg_conv_w[j], rg_conv_b[j], rg_w_a[j],
                           rg_b_a[j], rg_w_i[j], rg_b_i[j], rg_lambda[j], s5_a_re[j],
                           s5_a_im[j], s5_log_dt[j], s5_b_re[j], s5_b_im[j], s5_c_re[j],
                           s5_c_im[j], s5_d[j], s5_w_glu[j], s5_b_glu[j], od_w_out[j],
                           batch=batch)
        last = i == depth - 1
        h = _ffn(h, i, *ffn_b, ple=ple, final_norm=final_norm if last else None)
    return h.reshape(batch, seq, d)
```
